```python
import jax, jax.numpy as jnp
from jax import lax
import numpy as np

D_MODEL = 1024
BATCH = 4
SEQ = 4096
DEPTH = 2

GRID_W = 64
CTX_LEN = 256
NORM_EPS = 1e-6
NEG_INF = -1e30
MIX_W = D_MODEL // 2
N_BRANCH = 3
GLA_HEADS = 4
GLA_DV = MIX_W // GLA_HEADS
GLA_DK = GLA_DV // 2
GLA_GATE_RANK = 16
GLA_GATE_NORM = 16.0
GLA_CHUNK = 64
SWA_HEAD_DIM = 64
SWA_Q_HEADS = MIX_W // SWA_HEAD_DIM
SWA_KV_HEADS = SWA_Q_HEADS // 4
SWA_GROUP = SWA_Q_HEADS // SWA_KV_HEADS
SWA_WINDOW = 128
SWA_BLOCK = 128
ROPE_FREQS = SWA_HEAD_DIM // 4
ROPE_BASE = 10000.0
RWKV_HEAD = 64
RWKV_HEADS = MIX_W // RWKV_HEAD
RWKV_DECAY_RANK = 64
RWKV_A_RANK = 64
RWKV_GATE_RANK = 128
RWKV_GN_EPS = 64e-5
FFN_HIDDEN = -(-8 * D_MODEL // (3 * 256)) * 256
GLA_SPLIT = (GLA_HEADS * GLA_DK, GLA_HEADS * GLA_DK, MIX_W, GLA_GATE_RANK, GLA_GATE_RANK, MIX_W)
SWA_SPLIT = (SWA_Q_HEADS * SWA_HEAD_DIM, SWA_KV_HEADS * SWA_HEAD_DIM, SWA_KV_HEADS * SWA_HEAD_DIM)
RWKV_SPLIT = (MIX_W, MIX_W, MIX_W, RWKV_DECAY_RANK, RWKV_A_RANK, RWKV_GATE_RANK)
GROUP_SPLIT = (sum(GLA_SPLIT), sum(SWA_SPLIT), sum(RWKV_SPLIT), N_BRANCH * D_MODEL)
IN_WIDTH = sum(GROUP_SPLIT)

kernel_name = 'hybrid_gla_swa_rwkv7_dit_block'


def rmsnorm(x, g):
    x32 = x.astype(jnp.float32)
    y = x32 * lax.rsqrt(jnp.mean(x32 * x32, axis=-1, keepdims=True) + NORM_EPS)
    return (y * g.astype(jnp.float32)).astype(x.dtype)


def modulate(h, shift, scale):
    return h * (1 + scale[:, None]) + shift[:, None]


def swiglu(h, w1, w3, w2):
    return (jax.nn.silu(h @ w1) * (h @ w3)) @ w2


def split_cols(a, widths):
    return jnp.split(a, np.cumsum(widths)[:-1].tolist(), axis=-1)


def flip(a):
    return a[:, ::-1]


def centred_shift(f):
    z = jnp.zeros_like(f[:, :1])
    return 0.5 * (jnp.concatenate([z, f[:, :-1]], axis=1) + jnp.concatenate([f[:, 1:], z], axis=1))


def axial_rope(T):
    rows = T // GRID_W
    row = jnp.repeat(jnp.arange(rows), GRID_W).astype(jnp.float32)
    col = jnp.tile(jnp.arange(GRID_W), rows).astype(jnp.float32)
    inv = ROPE_BASE ** (-jnp.arange(ROPE_FREQS, dtype=jnp.float32) / ROPE_FREQS)
    ang = jnp.stack([row[:, None] * inv, col[:, None] * inv], axis=1)
    return jnp.cos(ang), jnp.sin(ang)


def apply_rope(x, cos, sin):
    B, T, H, Dh = x.shape
    xs = x.reshape(B, T, H, 2, 2, ROPE_FREQS)
    x1, x2 = xs[..., 0, :], xs[..., 1, :]
    cs, sn = cos[None, :, None].astype(x.dtype), sin[None, :, None].astype(x.dtype)
    return jnp.stack([x1 * cs - x2 * sn, x2 * cs + x1 * sn], axis=-2).reshape(B, T, H, Dh)


def gla_features(p, w_gk2, b_gk):
    q, k, v, gkf, gkb, og = split_cols(p, GLA_SPLIT)
    B, T, _ = q.shape
    heads_k = lambda t: t.reshape(B, T, GLA_HEADS, GLA_DK)
    log_gate = lambda lr, d: heads_k(jax.nn.log_sigmoid((lr @ w_gk2[d] + b_gk[d]).astype(jnp.float32)) / GLA_GATE_NORM)
    return (heads_k(q) * GLA_DK ** -0.5, heads_k(k), v.reshape(B, T, GLA_HEADS, GLA_DV),
            log_gate(gkf, 0), log_gate(gkb, 1), og)


def gla_scan(q, k, v, log_g, s0, with_out):
    B, T, H, K = k.shape
    V = v.shape[-1]
    n = T // GLA_CHUNK
    chunks = lambda a: a.astype(jnp.float32).reshape(B, n, GLA_CHUNK, H, a.shape[-1]).transpose(1, 0, 3, 2, 4)
    lower = jnp.tril(jnp.ones((GLA_CHUNK, GLA_CHUNK), bool))[:, :, None]
    xs = (chunks(k), chunks(v), chunks(log_g)) + ((chunks(q),) if with_out else ())

    def step(s, inp):
        kc, vc, gc = inp[:3]
        b = jnp.cumsum(gc, axis=2)
        b_last = b[:, :, -1:]
        s_new = jnp.exp(b_last[:, :, 0])[..., None] * s + jnp.einsum('bhck,bhcv->bhkv', kc * jnp.exp(b_last - b), vc)
        if not with_out:
            return s_new, None
        qc = inp[3]
        o_inter = jnp.einsum('bhck,bhkv->bhcv', qc * jnp.exp(b), s)
        rel = jnp.exp(jnp.where(lower, b[:, :, :, None] - b[:, :, None], -jnp.inf))
        att = jnp.einsum('bhik,bhjk,bhijk->bhij', qc, kc, rel)
        return s_new, o_inter + jnp.einsum('bhij,bhjv->bhiv', att, vc)

    s_fin, o = lax.scan(step, s0, xs)
    if not with_out:
        return s_fin, None
    return s_fin, o.transpose(1, 0, 3, 2, 4).reshape(B, T, H, V)


def gla_out(o, og, g_gla):
    B, T, H, V = o.shape
    return (rmsnorm(o, g_gla).reshape(B, T, H * V) * jax.nn.silu(og.astype(jnp.float32))).astype(og.dtype)


def gla_branch(fz, fx, g_gla, need_ctx):
    qz, kz, vz, lfz, lbz, ogz = fz
    qx, kx, vx, lfx, lbx, ogx = fx
    s0 = jnp.zeros((kx.shape[0], GLA_HEADS, GLA_DK, GLA_DV), jnp.float32)
    sf, ozf = gla_scan(qz, kz, vz, lfz, s0, need_ctx)
    sb, ozb = gla_scan(flip(qz), flip(kz), flip(vz), flip(lbz), s0, need_ctx)
    _, oxf = gla_scan(qx, kx, vx, lfx, sf, True)
    _, oxb = gla_scan(flip(qx), flip(kx), flip(vx), flip(lbx), sb, True)
    yx = gla_out(oxf + flip(oxb), ogx, g_gla)
    yz = gla_out(ozf + flip(ozb), ogz, g_gla) if need_ctx else None
    return yz, yx


def swa_features(p):
    q, k, v = split_cols(p, SWA_SPLIT)
    B, T, _ = q.shape
    return (q.reshape(B, T, SWA_Q_HEADS, SWA_HEAD_DIM), k.reshape(B, T, SWA_KV_HEADS, SWA_HEAD_DIM),
            v.reshape(B, T, SWA_KV_HEADS, SWA_HEAD_DIM))


def windowed_attention(q, k, v, k_ctx, v_ctx, sink):
    B, T, Hq, Dh = q.shape
    nb = T // SWA_BLOCK
    qb = q.reshape(B, nb, SWA_BLOCK, SWA_KV_HEADS, SWA_GROUP, Dh) * Dh ** -0.5

    def neighbours(a):
        ab = jnp.pad(a, ((0, 0), (SWA_BLOCK, SWA_BLOCK), (0, 0), (0, 0))).reshape(B, nb + 2, SWA_BLOCK, SWA_KV_HEADS, Dh)
        return jnp.concatenate([ab[:, :-2], ab[:, 1:-1], ab[:, 2:]], axis=2)

    kw, vw = neighbours(k), neighbours(v)
    s_loc = jnp.einsum('bnqhgd,bnkhd->bhgnqk', qb, kw).astype(jnp.float32)
    s_ctx = jnp.einsum('bnqhgd,bchd->bhgnqc', qb, k_ctx).astype(jnp.float32)
    blk = jnp.arange(nb)[:, None]
    qpos = (blk * SWA_BLOCK + jnp.arange(SWA_BLOCK)[None, :])[:, :, None]
    kpos = ((blk - 1) * SWA_BLOCK + jnp.arange(3 * SWA_BLOCK)[None, :])[:, None, :]
    band = (jnp.abs(kpos - qpos) <= SWA_WINDOW) & (kpos >= 0) & (kpos < T)
    sink_b = jnp.broadcast_to(sink.astype(jnp.float32).reshape(1, SWA_KV_HEADS, SWA_GROUP, 1, 1, 1), s_loc.shape[:-1] + (1,))
    p = jax.nn.softmax(jnp.concatenate([jnp.where(band, s_loc, NEG_INF), s_ctx, sink_b], axis=-1), axis=-1)
    p_loc = p[..., :3 * SWA_BLOCK].astype(v.dtype)
    p_ctx = p[..., 3 * SWA_BLOCK:-1].astype(v.dtype)
    out = jnp.einsum('bhgnqk,bnkhd->bnqhgd', p_loc, vw) + jnp.einsum('bhgnqc,bchd->bnqhgd', p_ctx, v_ctx)
    return out.reshape(B, T, Hq * Dh)


def context_attention(q, k, v, sink):
    B, C, Hq, Dh = q.shape
    qg = q.reshape(B, C, SWA_KV_HEADS, SWA_GROUP, Dh) * Dh ** -0.5
    s = jnp.einsum('bqhgd,bkhd->bhgqk', qg, k).astype(jnp.float32)
    sink_b = jnp.broadcast_to(sink.astype(jnp.float32).reshape(1, SWA_KV_HEADS, SWA_GROUP, 1, 1), s.shape[:-1] + (1,))
    p = jax.nn.softmax(jnp.concatenate([s, sink_b], axis=-1), axis=-1)[..., :-1].astype(v.dtype)
    return jnp.einsum('bhgqk,bkhd->bqhgd', p, v).reshape(B, C, Hq * Dh)


def rwkv_features(p, mu_shift, w0, w_w2, a0, w_a2, w_g2, k_k, k_a):
    p = p + mu_shift * (centred_shift(p) - p)
    r, k, v, wl, al, gl = split_cols(p, RWKV_SPLIT)
    B, T, _ = r.shape
    heads = lambda t: t.reshape(B, T, RWKV_HEADS, RWKV_HEAD)
    g = jax.nn.sigmoid(gl) @ w_g2
    kk = heads(k * k_k).astype(jnp.float32)
    kk = kk / jnp.maximum(jnp.sqrt(jnp.sum(kk * kk, axis=-1, keepdims=True)), 1e-12)
    dirs = []
    for d in range(2):
        w_raw = -jax.nn.softplus(-(w0[d] + jnp.tanh(wl) @ w_w2[d]).astype(jnp.float32)) - 0.5
        decay = jnp.exp(-jnp.exp(w_raw))
        a = jax.nn.sigmoid(a0[d] + al @ w_a2[d])
        dirs.append((heads(decay), heads(k * (1 + (a - 1) * k_a)), heads(a)))
    return heads(r), heads(v), kk, g, dirs


def rwkv7_scan(r, w, k, v, kk, a, s0, with_out):
    tm = lambda t: jnp.moveaxis(t.astype(jnp.float32), 1, 0)
    xs = tuple(tm(t) for t in (w, k, v, kk, a)) + ((tm(r),) if with_out else ())

    def step(S, inp):
        w_t, k_t, v_t, kk_t, a_t = inp[:5]
        sa = jnp.einsum('bhvk,bhk->bhv', S, kk_t)
        S = S * w_t[:, :, None, :] - sa[..., None] * (kk_t * a_t)[:, :, None, :] + v_t[..., None] * k_t[:, :, None, :]
        if not with_out:
            return S, None
        return S, jnp.einsum('bhvk,bhk->bhv', S, inp[5])

    s_fin, y = lax.scan(step, s0, xs)
    if not with_out:
        return s_fin, None
    return s_fin, jnp.moveaxis(y, 0, 1)


def rwkv_out(y, r, kf, kb, v, g, r_k, gn_w, gn_b):
    B, T, H, N = y.shape
    mean = jnp.mean(y, axis=-1, keepdims=True)
    var = jnp.mean(jnp.square(y - mean), axis=-1, keepdims=True)
    yn = ((y - mean) * lax.rsqrt(var + RWKV_GN_EPS)).reshape(B, T, H * N) * gn_w + gn_b
    bonus = jnp.sum(r.astype(jnp.float32) * (kf + kb).astype(jnp.float32) * r_k, axis=-1, keepdims=True) * v.astype(jnp.float32)
    return ((yn + bonus.reshape(B, T, H * N)) * g.astype(jnp.float32)).astype(g.dtype)


def rwkv_branch(fz, fx, r_k, gn_w, gn_b, need_ctx):
    def run(f, s_f, s_b, with_out):
        r, v, kk, g, ((wf, kf, af), (wb, kb, ab)) = f
        sf, yf = rwkv7_scan(r, wf, kf, v, kk, af, s_f, with_out)
        sb, yb = rwkv7_scan(flip(r), flip(wb), flip(kb), flip(v), flip(kk), flip(ab), s_b, with_out)
        if not with_out:
            return sf, sb, None
        return sf, sb, rwkv_out(yf + flip(yb), r, kf, kb, v, g, r_k, gn_w, gn_b)

    s0 = jnp.zeros((fx[0].shape[0], RWKV_HEADS, RWKV_HEAD, RWKV_HEAD), jnp.float32)
    szf, szb, yz = run(fz, s0, s0, need_ctx)
    _, _, yx = run(fx, szf, szb, True)
    return yz, yx


def merge_branches(ys, gate_logits, w_branch, w_out):
    gates = jnp.split(gate_logits, N_BRANCH, axis=-1)
    m = jax.nn.sigmoid(gates[0]) * (ys[0] @ w_branch[0])
    for i in range(1, N_BRANCH):
        m = m + jax.nn.sigmoid(gates[i]) * (ys[i] @ w_branch[i])
    return m @ w_out


def mixer_sublayer(hz, hx, cos, sin, w_in, w_gk2, b_gk, g_gla, sink, mu_shift, w0, w_w2, a0, w_a2,
                   w_g2, k_k, k_a, r_k, gn_w, gn_b, w_branch, w_out, need_ctx):
    gla_z, swa_z, rwkv_z, gate_z = split_cols(hz @ w_in, GROUP_SPLIT)
    gla_x, swa_x, rwkv_x, gate_x = split_cols(hx @ w_in, GROUP_SPLIT)
    ya_z, ya_x = gla_branch(gla_features(gla_z, w_gk2, b_gk), gla_features(gla_x, w_gk2, b_gk), g_gla, need_ctx)
    qz, kz, vz = swa_features(swa_z)
    qx, kx, vx = swa_features(swa_x)
    yb_x = windowed_attention(apply_rope(qx, cos, sin), apply_rope(kx, cos, sin), vx, kz, vz, sink)
    rw = (mu_shift, w0, w_w2, a0, w_a2, w_g2, k_k, k_a)
    yc_z, yc_x = rwkv_branch(rwkv_features(rwkv_z, *rw), rwkv_features(rwkv_x, *rw), r_k, gn_w, gn_b, need_ctx)
    mx = merge_branches((ya_x, yb_x, yc_x), gate_x, w_branch, w_out)
    if not need_ctx:
        return None, mx
    yb_z = context_attention(qz, kz, vz, sink)
    mz = merge_branches((ya_z, yb_z, yc_z), gate_z, w_branch, w_out)
    return mz, mx


def setup_inputs(seed: int = 0) -> dict:
    key = jax.random.key(seed)
    ks = iter(jax.random.split(key, 32))
    f32 = jnp.float32
    nrm = lambda shape, scale: jax.random.normal(next(ks), shape, f32) * scale
    L, D = DEPTH, D_MODEL
    return {
        'x': nrm((BATCH, SEQ, D), 1.0),
        'c': nrm((BATCH, D), 1.0),
        'ctx': nrm((BATCH, CTX_LEN, D), 1.0),
        'c_ctx': nrm((D,), 1.0),
        'w_ada': nrm((L, D, 6 * D), 0.5 * D ** -0.5),
        'b_ada': nrm((L, 6 * D), 0.02),
        'g_mix': 1.0 + nrm((L, D), 0.02),
        'g_ffn': 1.0 + nrm((L, D), 0.02),
        'w_in': nrm((L, D, IN_WIDTH), D ** -0.5),
        'w_gk2': nrm((L, 2, GLA_GATE_RANK, GLA_HEADS * GLA_DK), GLA_GATE_RANK ** -0.5),
        'b_gk': nrm((L, 2, GLA_HEADS * GLA_DK), 0.5),
        'g_gla': 1.0 + nrm((L, GLA_DV), 0.02),
        'sink': nrm((L, SWA_Q_HEADS), 0.5),
        'mu_shift': jax.random.uniform(next(ks), (L, sum(RWKV_SPLIT)), f32),
        'w0': jax.random.uniform(next(ks), (L, 2, MIX_W), f32, -6.0, 1.0),
        'w_w2': nrm((L, 2, RWKV_DECAY_RANK, MIX_W), 0.5 * RWKV_DECAY_RANK ** -0.5),
        'a0': nrm((L, 2, MIX_W), 0.5),
        'w_a2': nrm((L, 2, RWKV_A_RANK, MIX_W), 0.5 * RWKV_A_RANK ** -0.5),
        'w_g2': nrm((L, RWKV_GATE_RANK, MIX_W), RWKV_GATE_RANK ** -0.5),
        'k_k': 0.85 + nrm((L, MIX_W), 0.05),
        'k_a': 1.0 + nrm((L, MIX_W), 0.05),
        'r_k': nrm((L, RWKV_HEADS, RWKV_HEAD), 0.1),
        'gn_w': 1.0 + nrm((L, MIX_W), 0.02),
        'gn_b': nrm((L, MIX_W), 0.02),
        'w_branch': nrm((L, N_BRANCH, MIX_W, D), MIX_W ** -0.5),
        'w_out': nrm((L, D, D), D ** -0.5),
        'w_ffn1': nrm((L, D, FFN_HIDDEN), D ** -0.5),
        'w_ffn3': nrm((L, D, FFN_HIDDEN), D ** -0.5),
        'w_ffn2': nrm((L, FFN_HIDDEN, D), FFN_HIDDEN ** -0.5),
        'g_final': 1.0 + nrm((D,), 0.02),
    }


def reference(x, c, ctx, c_ctx, w_ada, b_ada, g_mix, g_ffn, w_in, w_gk2, b_gk, g_gla, sink, mu_shift,
              w0, w_w2, a0, w_a2, w_g2, k_k, k_a, r_k, gn_w, gn_b, w_branch, w_out, w_ffn1, w_ffn3,
              w_ffn2, g_final):
    cos, sin = axial_rope(x.shape[1])
    z = ctx
    for l in range(DEPTH):
        need_ctx = l < DEPTH - 1
        mod_x = jax.nn.silu(c) @ w_ada[l] + b_ada[l]
        mod_z = (jax.nn.silu(c_ctx) @ w_ada[l] + b_ada[l])[None]
        sh1x, sc1x, ga1x, sh2x, sc2x, ga2x = jnp.split(mod_x, 6, axis=-1)
        sh1z, sc1z, ga1z, sh2z, sc2z, ga2z = jnp.split(mod_z, 6, axis=-1)
        hx = modulate(rmsnorm(x, g_mix[l]), sh1x, sc1x)
        hz = modulate(rmsnorm(z, g_mix[l]), sh1z, sc1z)
        mz, mx = mixer_sublayer(hz, hx, cos, sin, w_in[l], w_gk2[l], b_gk[l], g_gla[l], sink[l], mu_shift[l],
                                w0[l], w_w2[l], a0[l], w_a2[l], w_g2[l], k_k[l], k_a[l], r_k[l], gn_w[l],
                                gn_b[l], w_branch[l], w_out[l], need_ctx)
        x = x + ga1x[:, None] * mx
        x = x + ga2x[:, None] * swiglu(modulate(rmsnorm(x, g_ffn[l]), sh2x, sc2x), w_ffn1[l], w_ffn3[l], w_ffn2[l])
        if need_ctx:
            z = z + ga1z[:, None] * mz
            z = z + ga2z[:, None] * swiglu(modulate(rmsnorm(z, g_ffn[l]), sh2z, sc2z), w_ffn1[l], w_ffn3[l], w_ffn2[l])
    return rmsnorm(x, g_final)
```

```python
import functools

import jax
import jax.numpy as jnp
import numpy as np
from jax import lax
from jax.experimental import pallas as pl
from jax.experimental.pallas import tpu as pltpu

F32 = jnp.float32
BF16 = jnp.bfloat16
HI = lax.Precision.HIGHEST

D_MODEL = 1024
GRID_W = 64
NORM_EPS = 1e-6
NEG_INF = -1e30
MIX_W = D_MODEL // 2
N_BRANCH = 3
GLA_HEADS = 4
GLA_DV = MIX_W // GLA_HEADS
GLA_DK = GLA_DV // 2
GLA_GATE_RANK = 16
GLA_GATE_NORM = 16.0
SWA_HEAD_DIM = 64
SWA_Q_HEADS = MIX_W // SWA_HEAD_DIM
SWA_KV_HEADS = SWA_Q_HEADS // 4
SWA_GROUP = SWA_Q_HEADS // SWA_KV_HEADS
SWA_WINDOW = 128
SWA_BLOCK = 128
ROPE_FREQS = SWA_HEAD_DIM // 4
ROPE_BASE = 10000.0
RWKV_HEAD = 64
RWKV_HEADS = MIX_W // RWKV_HEAD
RWKV_DECAY_RANK = 64
RWKV_A_RANK = 64
RWKV_GATE_RANK = 128
RWKV_GN_EPS = 64e-5
FFN_HIDDEN = -(-8 * D_MODEL // (3 * 256)) * 256
GLA_SPLIT = (GLA_HEADS * GLA_DK, GLA_HEADS * GLA_DK, MIX_W, GLA_GATE_RANK, GLA_GATE_RANK, MIX_W)
SWA_SPLIT = (SWA_Q_HEADS * SWA_HEAD_DIM, SWA_KV_HEADS * SWA_HEAD_DIM, SWA_KV_HEADS * SWA_HEAD_DIM)
RWKV_SPLIT = (MIX_W, MIX_W, MIX_W, RWKV_DECAY_RANK, RWKV_A_RANK, RWKV_GATE_RANK)
GROUP_SPLIT = (sum(GLA_SPLIT), sum(SWA_SPLIT), sum(RWKV_SPLIT), N_BRANCH * D_MODEL)

LANE = 128
ROW_TILE = 256
SCAN_CHUNK = 64
GLA_W = 2 * GLA_HEADS * GLA_DK + 2 * MIX_W + LANE
SWA_Q_W = 1024
SWA_KV_W = 384
RWKV_W = sum(RWKV_SPLIT)
MIXER_W = GLA_W + SWA_Q_W + SWA_KV_W + RWKV_W
VMEM_LIMIT = 56 * 1024 * 1024


def _mm(a, b):
    return jnp.dot(a.astype(BF16), b.astype(BF16), preferred_element_type=F32)


def _mm_nt(a, b):
    return lax.dot_general(a.astype(BF16), b.astype(BF16), (((1,), (1,)), ((), ())), preferred_element_type=F32)


def _mm_tn(a, b):
    return lax.dot_general(a.astype(BF16), b.astype(BF16), (((0,), (0,)), ((), ())), preferred_element_type=F32)


def _mm_hi(a, b):
    return jnp.dot(a, b, precision=HI, preferred_element_type=F32)


def _sigmoid(x):
    return 1.0 / (1.0 + jnp.exp(-x))


def _silu(x):
    return x * _sigmoid(x)


def _softplus(x):
    return jnp.maximum(x, 0.0) + jnp.log(1.0 + jnp.exp(-jnp.abs(x)))


def _norm_mod(x, g, shift, scale):
    y = x * lax.rsqrt(jnp.mean(x * x, axis=-1, keepdims=True) + NORM_EPS)
    return (y * g) * (1.0 + scale) + shift


def _segsum(x, e):
    hi = x.astype(BF16)
    lo = (x - hi.astype(F32)).astype(BF16)
    return jnp.dot(hi, e, preferred_element_type=F32) + jnp.dot(lo, e, preferred_element_type=F32)


def _params(*sem):
    return pltpu.CompilerParams(dimension_semantics=sem, vmem_limit_bytes=VMEM_LIMIT)


def _full(shape):
    nd = len(shape)
    return pl.BlockSpec(shape, lambda *_: (0,) * nd)


def _ada_kernel(c_ref, w_ref, b_ref, o_ref):
    o_ref[...] = _mm_hi(_silu(c_ref[...]), w_ref[...]) + b_ref[...]


def _ada(c8, w, b):
    n = w.shape[1]
    tn = 1536
    return pl.pallas_call(
        _ada_kernel,
        grid=(n // tn,),
        in_specs=[_full(c8.shape), pl.BlockSpec((D_MODEL, tn), lambda j: (0, j)), pl.BlockSpec((1, tn), lambda j: (0, j))],
        out_specs=pl.BlockSpec((c8.shape[0], tn), lambda j: (0, j)),
        out_shape=jax.ShapeDtypeStruct((c8.shape[0], n), F32),
        compiler_params=_params("arbitrary"),
        name="ada",
    )(c8, w, b.reshape(1, n))


def _inproj_kernel(s_ref, mod_ref, g_ref, w_ref, o_gla, o_q, o_kv, o_rwkv):
    m = mod_ref[...]
    h = _norm_mod(s_ref[...], g_ref[...], m[0:1], m[1:2]).astype(BF16)
    p = jnp.dot(h, w_ref[...], preferred_element_type=F32)
    o0 = 0
    for o_ref in (o_gla, o_q, o_kv, o_rwkv):
        wdt = o_ref.shape[-1]
        o_ref[...] = p[:, o0:o0 + wdt]
        o0 += wdt


def _mod_spec(nzt, nb, off=0):
    return pl.BlockSpec((None, 6, D_MODEL), lambda b, i: (jnp.where(i + off < nzt, nb, b), 0, 0))


def _inproj(s, mod3, g, w, nzt):
    B, L, _ = s.shape
    widths = (GLA_W, SWA_Q_W, SWA_KV_W, RWKV_W)
    return pl.pallas_call(
        _inproj_kernel,
        grid=(B, L // ROW_TILE),
        in_specs=[pl.BlockSpec((None, ROW_TILE, D_MODEL), lambda b, i: (b, i, 0)), _mod_spec(nzt, B),
                  _full((1, D_MODEL)), _full(w.shape)],
        out_specs=[pl.BlockSpec((None, ROW_TILE, wd), lambda b, i: (b, i, 0)) for wd in widths],
        out_shape=[jax.ShapeDtypeStruct((B, L, wd), F32) for wd in widths],
        compiler_params=_params("parallel", "parallel"),
        name="inproj",
    )(s, mod3, g, w)


def _chunk_index(reverse, nzc, nc):
    if not reverse:
        return lambda c: c
    return lambda c: jnp.where(c < nzc, nzc - 1 - c, nzc + nc - 1 - c)


def _order_masks(n, reverse):
    ii = lax.broadcasted_iota(jnp.int32, (n, n), 0)
    jj = lax.broadcasted_iota(jnp.int32, (n, n), 1)
    strict = (jj > ii) if reverse else (jj < ii)
    return strict, strict | (ii == jj), ii == jj


def _gla_kernel(reverse, final, *refs):
    if final:
        p_ref, wgk_ref, bgk_ref, oprev_ref, ggla_ref, o_ref, h_ref = refs
    else:
        p_ref, wgk_ref, bgk_ref, o_ref, h_ref = refs

    @pl.when(pl.program_id(1) == 0)
    def _():
        h_ref[...] = jnp.zeros_like(h_ref)

    p = p_ref[...]
    C = p.shape[0]
    nk = GLA_HEADS * GLA_DK
    q = p[:, 0:nk] * GLA_DK ** -0.5
    k = p[:, nk:2 * nk]
    v = p[:, 2 * nk:2 * nk + MIX_W]
    og = p[:, 2 * nk + MIX_W:2 * nk + 2 * MIX_W]
    gk = p[:, 2 * nk + 2 * MIX_W:]
    lg = -_softplus(-(_mm(gk, wgk_ref[...]) + bgk_ref[...])) / GLA_GATE_NORM
    _, incl, eye = _order_masks(C, reverse)
    cum = _mm_hi(incl.astype(F32), lg)
    tot = jnp.sum(lg, axis=0, keepdims=True)
    q_t = q * jnp.exp(cum)
    k_t = k * jnp.exp(-cum)
    k_e = k * jnp.exp(tot - cum)
    eye_k = eye[:GLA_DK, :GLA_DK]
    outs = []
    for h in range(GLA_HEADS):
        ks = slice(h * GLA_DK, (h + 1) * GLA_DK)
        vh = v[:, h * GLA_DV:(h + 1) * GLA_DV]
        hs = h_ref[h]
        att = jnp.where(incl, _mm_nt(q_t[:, ks], k_t[:, ks]), 0.0)
        outs.append(_mm(att, vh) + _mm(q_t[:, ks], hs))
        dec = jnp.sum(jnp.where(eye_k, jnp.exp(tot[:, ks]), 0.0), axis=1, keepdims=True)
        h_ref[h] = dec * hs + _mm_tn(k_e[:, ks], vh)
    o = jnp.concatenate(outs, axis=1)
    if final:
        o = o + oprev_ref[...]
        g = ggla_ref[...]
        ys = []
        for h in range(GLA_HEADS):
            oh = o[:, h * GLA_DV:(h + 1) * GLA_DV]
            ys.append(oh * lax.rsqrt(jnp.mean(oh * oh, axis=-1, keepdims=True) + NORM_EPS) * g)
        o = jnp.concatenate(ys, axis=1) * _silu(og)
    o_ref[...] = o


def _gla_scan(p, wgk, bgk, nzc, reverse, oprev=None, ggla=None):
    B, L, _ = p.shape
    C = SCAN_CHUNK
    nc = L // C
    cidx = _chunk_index(reverse, nzc, nc)
    tok = lambda wd: pl.BlockSpec((None, C, wd), lambda b, c: (b, cidx(c), 0))
    final = oprev is not None
    ins = [p, wgk, bgk]
    specs = [tok(GLA_W), _full(wgk.shape), _full(bgk.shape)]
    if final:
        ins += [oprev, ggla]
        specs += [tok(MIX_W), _full(ggla.shape)]
    return pl.pallas_call(
        functools.partial(_gla_kernel, reverse, final),
        grid=(B, nc),
        in_specs=specs,
        out_specs=tok(MIX_W),
        out_shape=jax.ShapeDtypeStruct((B, L, MIX_W), F32),
        scratch_shapes=[pltpu.VMEM((GLA_HEADS, GLA_DK, GLA_DV), F32)],
        compiler_params=_params("parallel", "arbitrary"),
        name="gla_bwd" if reverse else "gla_fwd",
    )(*ins)


def _swa_kernel(nzb, q_ref, kvm_ref, kv0_ref, kvp_ref, kvz_ref, csq_ref, csm_ref, cs0_ref, csp_ref, sink_ref, o_ref):
    n = pl.program_id(1)
    nblk = pl.num_programs(1)
    BL = SWA_BLOCK
    kvw = SWA_KV_HEADS * SWA_HEAD_DIM
    qw = SWA_Q_HEADS * SWA_HEAD_DIM

    def rope(x, xs, cs, reps):
        cos = jnp.concatenate([cs[:, :LANE]] * reps, axis=1)
        sin = jnp.concatenate([cs[:, LANE:]] * reps, axis=1)
        return x * cos + xs * sin

    qq = q_ref[...]
    q = rope(qq[:, :qw], qq[:, qw:], csq_ref[...], qw // LANE) * SWA_HEAD_DIM ** -0.5
    ks, vs = [], []
    for kv_ref, cs_ref in ((kvm_ref, csm_ref), (kv0_ref, cs0_ref), (kvp_ref, csp_ref)):
        kv = kv_ref[...]
        ks.append(rope(kv[:, :kvw], kv[:, kvw:2 * kvw], cs_ref[...], kvw // LANE))
        vs.append(kv[:, 2 * kvw:])
    kvz = kvz_ref[...]
    ks.append(kvz[:, :kvw])
    vs.append(kvz[:, 2 * kvw:])
    keys = jnp.concatenate(ks, axis=0)
    vals = jnp.concatenate(vs, axis=0)
    nloc = 3 * BL
    nkeys = keys.shape[0]
    qpos = n * BL + (lax.broadcasted_iota(jnp.int32, (SWA_GROUP * BL, nkeys), 0) & (BL - 1))
    col = lax.broadcasted_iota(jnp.int32, (SWA_GROUP * BL, nkeys), 1)
    kpos = (n - 1) * BL + col
    lo = jnp.where(n >= nzb, nzb * BL, nblk * BL)
    band = (jnp.abs(kpos - qpos) <= SWA_WINDOW) & (kpos >= lo) & (kpos < nblk * BL)
    mask = band | (col >= nloc)
    rhead = lax.broadcasted_iota(jnp.int32, (SWA_GROUP * BL, 1), 0) // BL
    outs = [None] * SWA_Q_HEADS
    for g in range(SWA_KV_HEADS):
        hd = slice(g * SWA_HEAD_DIM, (g + 1) * SWA_HEAD_DIM)
        qs = jnp.concatenate([q[:, (g * SWA_GROUP + j) * SWA_HEAD_DIM:(g * SWA_GROUP + j + 1) * SWA_HEAD_DIM]
                              for j in range(SWA_GROUP)], axis=0)
        s = jnp.where(mask, _mm_nt(qs, keys[:, hd]), NEG_INF)
        sk = jnp.zeros((SWA_GROUP * BL, 1), F32)
        for j in range(SWA_GROUP):
            sk = jnp.where(rhead == j, sink_ref[g * SWA_GROUP + j], sk)
        m = jnp.maximum(jnp.max(s, axis=-1, keepdims=True), sk)
        e = jnp.exp(s - m)
        den = jnp.sum(e, axis=-1, keepdims=True) + jnp.exp(sk - m)
        og = _mm(e, vals[:, hd]) / den
        for j in range(SWA_GROUP):
            outs[g * SWA_GROUP + j] = og[j * BL:(j + 1) * BL]
    o_ref[...] = jnp.concatenate(outs, axis=1)


def _swa(pq, pkv, cs, sink, nzb):
    B, L, _ = pq.shape
    BL = SWA_BLOCK
    nblk = L // BL
    lz = nzb * BL
    prev = lambda n: jnp.maximum(n - 1, 0)
    nxt = lambda n: jnp.minimum(n + 1, nblk - 1)
    kv_spec = lambda f: pl.BlockSpec((None, BL, SWA_KV_W), lambda b, n: (b, f(n), 0))
    cs_spec = lambda f: pl.BlockSpec((BL, 2 * LANE), lambda b, n: (f(n), 0))
    same = lambda n: n
    return pl.pallas_call(
        functools.partial(_swa_kernel, nzb),
        grid=(B, nblk),
        in_specs=[pl.BlockSpec((None, BL, SWA_Q_W), lambda b, n: (b, n, 0)),
                  kv_spec(prev), kv_spec(same), kv_spec(nxt),
                  pl.BlockSpec((None, lz, SWA_KV_W), lambda b, n: (b, 0, 0)),
                  cs_spec(same), cs_spec(prev), cs_spec(same), cs_spec(nxt),
                  pl.BlockSpec(memory_space=pltpu.SMEM)],
        out_specs=pl.BlockSpec((None, BL, MIX_W), lambda b, n: (b, n, 0)),
        out_shape=jax.ShapeDtypeStruct((B, L, MIX_W), F32),
        compiler_params=_params("parallel", "parallel"),
        name="swa",
    )(pq, pkv, pkv, pkv, pkv, cs, cs, cs, cs, sink)


def _rwkv_feat_kernel(nzt, p_ref, pp_ref, pn_ref, vec_ref, ww_ref, wa_ref, wg_ref, e_ref,
                      o_r, o_v, o_kk, o_g, o_gb, o_lwf, o_kf, o_bf, o_lwb, o_kb, o_bb):
    i = pl.program_id(1)
    n = pl.num_programs(1)
    p = p_ref[...]
    TM = p.shape[0]
    first = (i == 0) | (i == nzt)
    last = (i == nzt - 1) | (i == n - 1)
    prow = jnp.where(first, 0.0, pp_ref[7:8, :])
    nrow = jnp.where(last, 0.0, pn_ref[0:1, :])
    rid = lax.broadcasted_iota(jnp.int32, (TM, 1), 0)
    up = jnp.where(rid == 0, prow, pltpu.roll(p, 1, 0))
    dn = jnp.where(rid == TM - 1, nrow, pltpu.roll(p, TM - 1, 0))
    vec = vec_ref[...]
    mu = jnp.concatenate([vec[8 + j:9 + j] for j in range(4)], axis=1)[:, :RWKV_W]
    pm = p + mu * (0.5 * (up + dn) - p)
    W = MIX_W
    r, k, v = pm[:, :W], pm[:, W:2 * W], pm[:, 2 * W:3 * W]
    wa = pm[:, 3 * W:3 * W + LANE]
    gl = pm[:, 3 * W + LANE:]
    e = e_ref[...]
    k_k, k_a, r_k = vec[4:5], vec[5:6], vec[6:7]
    g = _mm(_sigmoid(gl), wg_ref[...])
    kkn = k * k_k
    kk = kkn / jnp.maximum(jnp.sqrt(_segsum(kkn * kkn, e)), 1e-12)
    twa = jnp.tanh(wa)
    ksum = None
    for d, (o_lw, o_k, o_b) in enumerate(((o_lwf, o_kf, o_bf), (o_lwb, o_kb, o_bb))):
        w_raw = -_softplus(-(vec[d:d + 1] + _mm(twa, ww_ref[d]))) - 0.5
        a = _sigmoid(vec[2 + d:3 + d] + _mm(wa, wa_ref[d]))
        kd = k * (1.0 + (a - 1.0) * k_a)
        o_lw[...] = -jnp.exp(w_raw)
        o_k[...] = kd
        o_b[...] = kk * a
        ksum = kd if ksum is None else ksum + kd
    o_r[...] = r
    o_v[...] = v
    o_kk[...] = kk
    o_g[...] = g
    o_gb[...] = _segsum(r * ksum * r_k, e) * v * g


def _rwkv_feat(p, vec, ww, wa, wg, e, nzt):
    B, L, _ = p.shape
    TM = ROW_TILE
    r8 = TM // 8
    tok = pl.BlockSpec((None, TM, RWKV_W), lambda b, i: (b, i, 0))
    halo_p = pl.BlockSpec((None, 8, RWKV_W), lambda b, i: (b, jnp.maximum(i * r8 - 1, 0), 0))
    halo_n = pl.BlockSpec((None, 8, RWKV_W), lambda b, i: (b, jnp.minimum((i + 1) * r8, L // 8 - 1), 0))
    out = pl.BlockSpec((None, TM, MIX_W), lambda b, i: (b, i, 0))
    return pl.pallas_call(
        functools.partial(_rwkv_feat_kernel, nzt),
        grid=(B, L // TM),
        in_specs=[tok, halo_p, halo_n, _full(vec.shape), _full(ww.shape), _full(wa.shape), _full(wg.shape), _full(e.shape)],
        out_specs=[out] * 11,
        out_shape=[jax.ShapeDtypeStruct((B, L, MIX_W), F32)] * 11,
        compiler_params=_params("parallel", "parallel"),
        name="rwkv_feat",
    )(p, p, p, vec, ww, wa, wg, e)


def _rwkv_scan_kernel(reverse, final, *refs):
    if final:
        r_ref, v_ref, kk_ref, lw_ref, k_ref, b_ref, yprev_ref, g_ref, gb_ref, gn_ref, o_ref, h_ref = refs
    else:
        r_ref, v_ref, kk_ref, lw_ref, k_ref, b_ref, o_ref, h_ref = refs

    @pl.when(pl.program_id(1) == 0)
    def _():
        h_ref[...] = jnp.zeros_like(h_ref)

    lw = lw_ref[...]
    C = lw.shape[0]
    N = RWKV_HEAD
    strict, incl, eye = _order_masks(C, reverse)
    cum = _mm_hi(incl.astype(F32), lw)
    tot = jnp.sum(lw, axis=0, keepdims=True)
    e_inv = jnp.exp(-cum)
    e_end = jnp.exp(tot - cum)
    e_tot = jnp.exp(tot)
    kap_t = kk_ref[...] * jnp.exp(cum - lw)
    r_t = r_ref[...] * jnp.exp(cum)
    k_all, b_all, v_all = k_ref[...], b_ref[...], v_ref[...]
    k_t, b_t = k_all * e_inv, b_all * e_inv
    k_e, b_e = k_all * e_end, b_all * e_end
    eye_n = eye[:N, :N]
    ys = []
    for h in range(RWKV_HEADS):
        sl = slice(h * N, (h + 1) * N)
        vh = v_all[:, sl]
        G = _mm_nt(jnp.concatenate([kap_t[:, sl], r_t[:, sl]], axis=0),
                   jnp.concatenate([b_t[:, sl], k_t[:, sl]], axis=0))
        a_ab = jnp.where(strict, G[:C, :C], 0.0)
        a_ak = jnp.where(strict, G[:C, C:], 0.0)
        a_rb = jnp.where(incl, G[C:, :C], 0.0)
        a_rk = jnp.where(incl, G[C:, C:], 0.0)
        x = jnp.concatenate([kap_t[:, sl], _mm(a_ak, vh)], axis=1)
        x = x - _mm(a_ab, x)
        pw = _mm(a_ab, a_ab)
        span = 2
        while span < C:
            x = x + _mm(pw, x)
            span *= 2
            if span < C:
                pw = _mm(pw, pw)
        w_h, u_h = x[:, :N], x[:, N:]
        hs = h_ref[h]
        dec = jnp.sum(jnp.where(eye_n, e_tot[:, sl], 0.0), axis=1, keepdims=True)
        vu = jnp.concatenate([vh, u_h], axis=0)
        y = _mm(r_t[:, sl] - _mm(a_rb, w_h), hs) + _mm(jnp.concatenate([a_rk, -a_rb], axis=1), vu)
        h_new = dec * hs - _mm(_mm_tn(b_e[:, sl], w_h), hs) + _mm_tn(jnp.concatenate([k_e[:, sl], -b_e[:, sl]], axis=0), vu)
        h_ref[h] = h_new
        ys.append(y)
    if final:
        yprev = yprev_ref[...]
        outs = []
        for h in range(RWKV_HEADS):
            y = ys[h] + yprev[:, h * N:(h + 1) * N]
            mean = jnp.mean(y, axis=-1, keepdims=True)
            yc = y - mean
            var = jnp.mean(yc * yc, axis=-1, keepdims=True)
            outs.append(yc * lax.rsqrt(var + RWKV_GN_EPS))
        gn = gn_ref[...]
        o_ref[...] = (jnp.concatenate(outs, axis=1) * gn[0:1] + gn[1:2]) * g_ref[...] + gb_ref[...]
    else:
        o_ref[...] = jnp.concatenate(ys, axis=1)


def _rwkv_scan(r, v, kk, lw, k, b, nzc, reverse, yprev=None, g=None, gb=None, gn=None):
    B, L, _ = r.shape
    C = SCAN_CHUNK
    nc = L // C
    cidx = _chunk_index(reverse, nzc, nc)
    tok = pl.BlockSpec((None, C, MIX_W), lambda b_, c: (b_, cidx(c), 0))
    final = yprev is not None
    ins = [r, v, kk, lw, k, b]
    specs = [tok] * 6
    if final:
        ins += [yprev, g, gb, gn]
        specs += [tok] * 3 + [_full(gn.shape)]
    return pl.pallas_call(
        functools.partial(_rwkv_scan_kernel, reverse, final),
        grid=(B, nc),
        in_specs=specs,
        out_specs=tok,
        out_shape=jax.ShapeDtypeStruct((B, L, MIX_W), F32),
        scratch_shapes=[pltpu.VMEM((RWKV_HEADS, RWKV_HEAD, RWKV_HEAD), F32)],
        compiler_params=_params("parallel", "arbitrary"),
        name="rwkv_bwd" if reverse else "rwkv_fwd",
    )(*ins)


def _merge_kernel(s_ref, mod_ref, g_ref, ya_ref, yb_ref, yc_ref, wg_ref, wb_ref, wo_ref, o_ref):
    m = mod_ref[...]
    s = s_ref[...]
    h = _norm_mod(s, g_ref[...], m[0:1], m[1:2]).astype(BF16)
    gates = jnp.dot(h, wg_ref[...], preferred_element_type=F32)
    acc = None
    for i, y_ref in enumerate((ya_ref, yb_ref, yc_ref)):
        t = _sigmoid(gates[:, i * D_MODEL:(i + 1) * D_MODEL]) * _mm(y_ref[...], wb_ref[i])
        acc = t if acc is None else acc + t
    o_ref[...] = s + m[2:3] * _mm(acc, wo_ref[...])


def _merge(s, mod3, g, ya, yb, yc, wg, wb, wo, nzt, off):
    B, L, _ = s.shape
    TM = ROW_TILE
    nt = L // TM - off
    tok_in = lambda wd: pl.BlockSpec((None, TM, wd), lambda b, i: (b, i + off, 0))
    return pl.pallas_call(
        _merge_kernel,
        grid=(B, nt),
        in_specs=[tok_in(D_MODEL), _mod_spec(nzt, B, off), _full((1, D_MODEL)), tok_in(MIX_W), tok_in(MIX_W), tok_in(MIX_W),
                  _full(wg.shape), _full(wb.shape), _full(wo.shape)],
        out_specs=pl.BlockSpec((None, TM, D_MODEL), lambda b, i: (b, i, 0)),
        out_shape=jax.ShapeDtypeStruct((B, nt * TM, D_MODEL), F32),
        compiler_params=_params("parallel", "parallel"),
        name="merge",
    )(s, mod3, g, ya, yb, yc, wg, wb, wo)


def _ffn_kernel(final, s_ref, mod_ref, g_ref, w1_ref, w3_ref, w2_ref, gf_ref, o_ref):
    m = mod_ref[...]
    s = s_ref[...]
    h = _norm_mod(s, g_ref[...], m[3:4], m[4:5]).astype(BF16)
    a = jnp.dot(h, w1_ref[...], preferred_element_type=F32)
    b = jnp.dot(h, w3_ref[...], preferred_element_type=F32)
    o = s + m[5:6] * _mm(_silu(a) * b, w2_ref[...])
    if final:
        o = o * lax.rsqrt(jnp.mean(o * o, axis=-1, keepdims=True) + NORM_EPS) * gf_ref[...]
    o_ref[...] = o


def _ffn(s, mod3, g, w1, w3, w2, gf, nzt, off, final):
    B, L, _ = s.shape
    TM = ROW_TILE
    nt = L // TM
    const = lambda shape: pl.BlockSpec(shape, lambda b, i: (0,) * len(shape), pipeline_mode=pl.Buffered(1))
    return pl.pallas_call(
        functools.partial(_ffn_kernel, final),
        grid=(B, nt),
        in_specs=[pl.BlockSpec((None, TM, D_MODEL), lambda b, i: (b, i, 0)), _mod_spec(nzt, B, off), _full((1, D_MODEL)),
                  const(w1.shape), const(w3.shape), const(w2.shape), _full((1, D_MODEL))],
        out_specs=pl.BlockSpec((None, TM, D_MODEL), lambda b, i: (b, i, 0)),
        out_shape=jax.ShapeDtypeStruct((B, L, D_MODEL), F32),
        compiler_params=_params("parallel", "parallel"),
        name="ffn",
    )(s, mod3, g, w1, w3, w2, gf)


def _rope_tables(T, lz):
    rows = T // GRID_W
    row = jnp.repeat(jnp.arange(rows), GRID_W).astype(F32)
    col = jnp.tile(jnp.arange(GRID_W), rows).astype(F32)
    inv = ROPE_BASE ** (-jnp.arange(ROPE_FREQS, dtype=F32) / ROPE_FREQS)
    ar, ac = row[:, None] * inv, col[:, None] * inv
    cos = jnp.concatenate([jnp.cos(ar), jnp.cos(ar), jnp.cos(ac), jnp.cos(ac)], axis=1)
    sin = jnp.concatenate([-jnp.sin(ar), jnp.sin(ar), -jnp.sin(ac), jnp.sin(ac)], axis=1)
    reps = LANE // SWA_HEAD_DIM
    x_tab = jnp.concatenate([jnp.tile(cos, (1, reps)), jnp.tile(sin, (1, reps))], axis=1)
    z_tab = jnp.concatenate([jnp.ones((lz, LANE), F32), jnp.zeros((lz, LANE), F32)], axis=1)
    return jnp.concatenate([z_tab, x_tab], axis=0)


def _partner_cols(w):
    return w.reshape(w.shape[0], -1, 2, ROPE_FREQS)[:, :, ::-1, :].reshape(w.shape)


def _mixer_weight(w_in):
    g0, g1, g2 = np.cumsum(GROUP_SPLIT)[:3]
    gla, swa, rwkv = w_in[:, :g0], w_in[:, g0:g1], w_in[:, g1:g2]
    nk = GLA_HEADS * GLA_DK
    a = 2 * nk + MIX_W
    gla_p = jnp.concatenate([gla[:, :a], gla[:, a + 2 * GLA_GATE_RANK:], gla[:, a:a + 2 * GLA_GATE_RANK],
                             jnp.zeros((D_MODEL, LANE - 2 * GLA_GATE_RANK), F32)], axis=1)
    qw = SWA_Q_HEADS * SWA_HEAD_DIM
    kvw = SWA_KV_HEADS * SWA_HEAD_DIM
    q, k, v = swa[:, :qw], swa[:, qw:qw + kvw], swa[:, qw + kvw:]
    w = jnp.concatenate([gla_p, q, _partner_cols(q), k, _partner_cols(k), v, rwkv], axis=1)
    return w.astype(BF16), w_in[:, g2:].astype(BF16)


def _pad_rows(w, top, total):
    return jnp.concatenate([jnp.zeros((top, w.shape[1]), F32), w, jnp.zeros((total - top - w.shape[0], w.shape[1]), F32)], axis=0)


def kernel(x, c, ctx, c_ctx, w_ada, b_ada, g_mix, g_ffn, w_in, w_gk2, b_gk, g_gla, sink, mu_shift, w0, w_w2, a0, w_a2,
           w_g2, k_k, k_a, r_k, gn_w, gn_b, w_branch, w_out, w_ffn1, w_ffn3, w_ffn2, g_final):
    B, T, D = x.shape
    lz = ctx.shape[1]
    depth = w_in.shape[0]
    assert D == D_MODEL and lz % ROW_TILE == 0 and T % ROW_TILE == 0 and T % GRID_W == 0
    nzt, nzc, nzb = lz // ROW_TILE, lz // SCAN_CHUNK, lz // SWA_BLOCK
    s = jnp.concatenate([ctx, x], axis=1)
    cs = _rope_tables(T, lz)
    c8 = jnp.concatenate([c, c_ctx[None], jnp.zeros((8 - (B + 1) % 8 if (B + 1) % 8 else 0, D), F32)], axis=0)
    hidx = np.arange(MIX_W) // RWKV_HEAD
    seg = jnp.asarray(hidx[:, None] == hidx[None, :], BF16)
    out = None
    for l in range(depth):
        last = l == depth - 1
        mod3 = _ada(c8, w_ada[l], b_ada[l]).reshape(c8.shape[0], 6, D)
        w_mix, w_gate = _mixer_weight(w_in[l])
        p_gla, p_q, p_kv, p_rwkv = _inproj(s, mod3, g_mix[l][None], w_mix, nzt)

        wgk = [_pad_rows(w_gk2[l, d], d * GLA_GATE_RANK, LANE).astype(BF16) for d in range(2)]
        o_f = _gla_scan(p_gla, wgk[0], b_gk[l, 0][None], nzc, False)
        ya = _gla_scan(p_gla, wgk[1], b_gk[l, 1][None], nzc, True, o_f, g_gla[l][None])

        yb = _swa(p_q, p_kv, cs, sink[l], nzb)

        zero = jnp.zeros((MIX_W,), F32)
        mu4 = jnp.concatenate([mu_shift[l], jnp.zeros((4 * MIX_W - RWKV_W,), F32)]).reshape(4, MIX_W)
        vec = jnp.stack([w0[l, 0], w0[l, 1], a0[l, 0], a0[l, 1], k_k[l], k_a[l], r_k[l].reshape(-1), zero], axis=0)
        vec = jnp.concatenate([vec, mu4, jnp.zeros((4, MIX_W), F32)], axis=0)
        ww = jnp.stack([_pad_rows(w_w2[l, d], 0, LANE) for d in range(2)]).astype(BF16)
        wa = jnp.stack([_pad_rows(w_a2[l, d], RWKV_DECAY_RANK, LANE) for d in range(2)]).astype(BF16)
        f_r, f_v, f_kk, f_g, f_gb, lw_f, k_f, b_f, lw_b, k_b, b_b = _rwkv_feat(
            p_rwkv, vec, ww, wa, w_g2[l].astype(BF16), seg, nzt)
        y_f = _rwkv_scan(f_r, f_v, f_kk, lw_f, k_f, b_f, nzc, False)
        yc = _rwkv_scan(f_r, f_v, f_kk, lw_b, k_b, b_b, nzc, True, y_f, f_g, f_gb, jnp.stack([gn_w[l], gn_b[l]]))

        off = nzt if last else 0
        s_mix = _merge(s, mod3, g_mix[l][None], ya, yb, yc, w_gate, w_branch[l].astype(BF16), w_out[l].astype(BF16), nzt, off)
        s = _ffn(s_mix, mod3, g_ffn[l][None], w_ffn1[l].astype(BF16), w_ffn3[l].astype(BF16), w_ffn2[l].astype(BF16),
                 g_final[None], nzt, off, last)
        out = s
    return out
```

```python
import functools

import jax
import jax.numpy as jnp
import numpy as np
from jax import lax
from jax.experimental import pallas as pl
from jax.experimental.pallas import tpu as pltpu

F32 = jnp.float32
BF16 = jnp.bfloat16
HI = lax.Precision.HIGHEST

D_MODEL = 1024
GRID_W = 64
NORM_EPS = 1e-6
NEG_INF = -1e30
MIX_W = D_MODEL // 2
N_BRANCH = 3
GLA_HEADS = 4
GLA_DV = MIX_W // GLA_HEADS
GLA_DK = GLA_DV // 2
GLA_GATE_RANK = 16
GLA_GATE_NORM = 16.0
SWA_HEAD_DIM = 64
SWA_Q_HEADS = MIX_W // SWA_HEAD_DIM
SWA_KV_HEADS = SWA_Q_HEADS // 4
SWA_GROUP = SWA_Q_HEADS // SWA_KV_HEADS
SWA_WINDOW = 128
SWA_BLOCK = 128
ROPE_FREQS = SWA_HEAD_DIM // 4
ROPE_BASE = 10000.0
RWKV_HEAD = 64
RWKV_HEADS = MIX_W // RWKV_HEAD
RWKV_DECAY_RANK = 64
RWKV_A_RANK = 64
RWKV_GATE_RANK = 128
RWKV_GN_EPS = 64e-5
FFN_HIDDEN = -(-8 * D_MODEL // (3 * 256)) * 256
GLA_SPLIT = (GLA_HEADS * GLA_DK, GLA_HEADS * GLA_DK, MIX_W, GLA_GATE_RANK, GLA_GATE_RANK, MIX_W)
SWA_SPLIT = (SWA_Q_HEADS * SWA_HEAD_DIM, SWA_KV_HEADS * SWA_HEAD_DIM, SWA_KV_HEADS * SWA_HEAD_DIM)
RWKV_SPLIT = (MIX_W, MIX_W, MIX_W, RWKV_DECAY_RANK, RWKV_A_RANK, RWKV_GATE_RANK)
GROUP_SPLIT = (sum(GLA_SPLIT), sum(SWA_SPLIT), sum(RWKV_SPLIT), N_BRANCH * D_MODEL)

LANE = 128
ROW_TILE = 256
SCAN_CHUNK = 64
GLA_W = 2 * GLA_HEADS * GLA_DK + 2 * MIX_W + LANE
SWA_Q_W = 1024
SWA_KV_W = 384
RWKV_W = sum(RWKV_SPLIT)
MIXER_W = GLA_W + SWA_Q_W + SWA_KV_W + RWKV_W
VMEM_LIMIT = 56 * 1024 * 1024


def _mm(a, b):
    return jnp.dot(a.astype(BF16), b.astype(BF16), preferred_element_type=F32)


def _mm_nt(a, b):
    return lax.dot_general(a.astype(BF16), b.astype(BF16), (((1,), (1,)), ((), ())), preferred_element_type=F32)


def _mm_tn(a, b):
    return lax.dot_general(a.astype(BF16), b.astype(BF16), (((0,), (0,)), ((), ())), preferred_element_type=F32)


def _mm_hi(a, b):
    return jnp.dot(a, b, precision=HI, preferred_element_type=F32)


def _sigmoid(x):
    return 1.0 / (1.0 + jnp.exp(-x))


def _silu(x):
    return x * _sigmoid(x)


def _softplus(x):
    return jnp.maximum(x, 0.0) + jnp.log(1.0 + jnp.exp(-jnp.abs(x)))


def _norm_mod(x, g, shift, scale):
    y = x * lax.rsqrt(jnp.mean(x * x, axis=-1, keepdims=True) + NORM_EPS)
    return (y * g) * (1.0 + scale) + shift


def _segsum(x, e):
    hi = x.astype(BF16)
    lo = (x - hi.astype(F32)).astype(BF16)
    return jnp.dot(hi, e, preferred_element_type=F32) + jnp.dot(lo, e, preferred_element_type=F32)


def _params(*sem):
    return pltpu.CompilerParams(dimension_semantics=sem, vmem_limit_bytes=VMEM_LIMIT)


def _full(shape):
    nd = len(shape)
    return pl.BlockSpec(shape, lambda *_: (0,) * nd)


def _ada_kernel(c_ref, w_ref, b_ref, o_ref):
    o_ref[...] = _mm_hi(_silu(c_ref[...]), w_ref[...]) + b_ref[...]


def _ada(c8, w, b):
    n = w.shape[1]
    tn = 1536
    return pl.pallas_call(
        _ada_kernel,
        grid=(n // tn,),
        in_specs=[_full(c8.shape), pl.BlockSpec((D_MODEL, tn), lambda j: (0, j)), pl.BlockSpec((1, tn), lambda j: (0, j))],
        out_specs=pl.BlockSpec((c8.shape[0], tn), lambda j: (0, j)),
        out_shape=jax.ShapeDtypeStruct((c8.shape[0], n), F32),
        compiler_params=_params("arbitrary"),
        name="ada",
    )(c8, w, b.reshape(1, n))


def _inproj_kernel(s_ref, mod_ref, g_ref, w_ref, o_gla, o_q, o_kv, o_rwkv):
    m = mod_ref[...]
    h = _norm_mod(s_ref[...], g_ref[...], m[0:1], m[1:2]).astype(BF16)
    p = jnp.dot(h, w_ref[...], preferred_element_type=F32)
    o0 = 0
    for o_ref in (o_gla, o_q, o_kv, o_rwkv):
        wdt = o_ref.shape[-1]
        o_ref[...] = p[:, o0:o0 + wdt]
        o0 += wdt


def _mod_spec(nzt, nb, off=0):
    return pl.BlockSpec((None, 6, D_MODEL), lambda b, i: (jnp.where(i + off < nzt, nb, b), 0, 0))


def _inproj(s, mod3, g, w, nzt):
    B, L, _ = s.shape
    widths = (GLA_W, SWA_Q_W, SWA_KV_W, RWKV_W)
    return pl.pallas_call(
        _inproj_kernel,
        grid=(B, L // ROW_TILE),
        in_specs=[pl.BlockSpec((None, ROW_TILE, D_MODEL), lambda b, i: (b, i, 0)), _mod_spec(nzt, B),
                  _full((1, D_MODEL)), _full(w.shape)],
        out_specs=[pl.BlockSpec((None, ROW_TILE, wd), lambda b, i: (b, i, 0)) for wd in widths],
        out_shape=[jax.ShapeDtypeStruct((B, L, wd), F32) for wd in widths],
        compiler_params=_params("parallel", "parallel"),
        name="inproj",
    )(s, mod3, g, w)


def _chunk_index(reverse, nzc, nc):
    if not reverse:
        return lambda c: c
    return lambda c: jnp.where(c < nzc, nzc - 1 - c, nzc + nc - 1 - c)


def _order_masks(n, reverse):
    ii = lax.broadcasted_iota(jnp.int32, (n, n), 0)
    jj = lax.broadcasted_iota(jnp.int32, (n, n), 1)
    strict = (jj > ii) if reverse else (jj < ii)
    return strict, strict | (ii == jj), ii == jj


def _gla_kernel(reverse, final, *refs):
    if final:
        p_ref, wgk_ref, bgk_ref, oprev_ref, ggla_ref, o_ref, h_ref = refs
    else:
        p_ref, wgk_ref, bgk_ref, o_ref, h_ref = refs

    @pl.when(pl.program_id(1) == 0)
    def _():
        h_ref[...] = jnp.zeros_like(h_ref)

    p = p_ref[...]
    C = p.shape[0]
    nk = GLA_HEADS * GLA_DK
    q = p[:, 0:nk] * GLA_DK ** -0.5
    k = p[:, nk:2 * nk]
    v = p[:, 2 * nk:2 * nk + MIX_W]
    og = p[:, 2 * nk + MIX_W:2 * nk + 2 * MIX_W]
    gk = p[:, 2 * nk + 2 * MIX_W:]
    lg = -_softplus(-(_mm(gk, wgk_ref[...]) + bgk_ref[...])) / GLA_GATE_NORM
    _, incl, eye = _order_masks(C, reverse)
    cum = _mm_hi(incl.astype(F32), lg)
    tot = jnp.sum(lg, axis=0, keepdims=True)
    q_t = q * jnp.exp(cum)
    k_t = k * jnp.exp(-cum)
    k_e = k * jnp.exp(tot - cum)
    eye_k = eye[:GLA_DK, :GLA_DK]
    heads = range(GLA_HEADS)
    kss = [slice(h * GLA_DK, (h + 1) * GLA_DK) for h in heads]
    vh = [v[:, h * GLA_DV:(h + 1) * GLA_DV] for h in heads]
    hs = [h_ref[h] for h in heads]
    att = [jnp.where(incl, _mm_nt(q_t[:, ks], k_t[:, ks]), 0.0) for ks in kss]
    inter = [_mm(q_t[:, kss[h]], hs[h]) for h in heads]
    kv = [_mm_tn(k_e[:, kss[h]], vh[h]) for h in heads]
    outs = [_mm(att[h], vh[h]) + inter[h] for h in heads]
    e_tot = jnp.exp(tot)
    for h in heads:
        dec = jnp.sum(jnp.where(eye_k, e_tot[:, kss[h]], 0.0), axis=1, keepdims=True)
        h_ref[h] = dec * hs[h] + kv[h]
    o = jnp.concatenate(outs, axis=1)
    if final:
        o = o + oprev_ref[...]
        g = ggla_ref[...]
        ys = []
        for h in range(GLA_HEADS):
            oh = o[:, h * GLA_DV:(h + 1) * GLA_DV]
            ys.append(oh * lax.rsqrt(jnp.mean(oh * oh, axis=-1, keepdims=True) + NORM_EPS) * g)
        o = jnp.concatenate(ys, axis=1) * _silu(og)
    o_ref[...] = o


def _gla_scan(p, wgk, bgk, nzc, reverse, oprev=None, ggla=None):
    B, L, _ = p.shape
    C = SCAN_CHUNK
    nc = L // C
    cidx = _chunk_index(reverse, nzc, nc)
    tok = lambda wd: pl.BlockSpec((None, C, wd), lambda b, c: (b, cidx(c), 0))
    final = oprev is not None
    ins = [p, wgk, bgk]
    specs = [tok(GLA_W), _full(wgk.shape), _full(bgk.shape)]
    if final:
        ins += [oprev, ggla]
        specs += [tok(MIX_W), _full(ggla.shape)]
    return pl.pallas_call(
        functools.partial(_gla_kernel, reverse, final),
        grid=(B, nc),
        in_specs=specs,
        out_specs=tok(MIX_W),
        out_shape=jax.ShapeDtypeStruct((B, L, MIX_W), F32),
        scratch_shapes=[pltpu.VMEM((GLA_HEADS, GLA_DK, GLA_DV), F32)],
        compiler_params=_params("parallel", "arbitrary"),
        name="gla_bwd" if reverse else "gla_fwd",
    )(*ins)


def _swa_kernel(nzb, q_ref, kvm_ref, kv0_ref, kvp_ref, kvz_ref, csq_ref, csm_ref, cs0_ref, csp_ref, sink_ref, o_ref):
    n = pl.program_id(1)
    nblk = pl.num_programs(1)
    BL = SWA_BLOCK
    kvw = SWA_KV_HEADS * SWA_HEAD_DIM
    qw = SWA_Q_HEADS * SWA_HEAD_DIM

    def rope(x, xs, cs, reps):
        cos = jnp.concatenate([cs[:, :LANE]] * reps, axis=1)
        sin = jnp.concatenate([cs[:, LANE:]] * reps, axis=1)
        return x * cos + xs * sin

    qq = q_ref[...]
    q = rope(qq[:, :qw], qq[:, qw:], csq_ref[...], qw // LANE) * SWA_HEAD_DIM ** -0.5
    ks, vs = [], []
    for kv_ref, cs_ref in ((kvm_ref, csm_ref), (kv0_ref, cs0_ref), (kvp_ref, csp_ref)):
        kv = kv_ref[...]
        ks.append(rope(kv[:, :kvw], kv[:, kvw:2 * kvw], cs_ref[...], kvw // LANE))
        vs.append(kv[:, 2 * kvw:])
    kvz = kvz_ref[...]
    ks.append(kvz[:, :kvw])
    vs.append(kvz[:, 2 * kvw:])
    keys = jnp.concatenate(ks, axis=0)
    vals = jnp.concatenate(vs, axis=0)
    nloc = 3 * BL
    nkeys = keys.shape[0]
    qpos = n * BL + (lax.broadcasted_iota(jnp.int32, (SWA_GROUP * BL, nkeys), 0) & (BL - 1))
    col = lax.broadcasted_iota(jnp.int32, (SWA_GROUP * BL, nkeys), 1)
    kpos = (n - 1) * BL + col
    lo = jnp.where(n >= nzb, nzb * BL, nblk * BL)
    band = (jnp.abs(kpos - qpos) <= SWA_WINDOW) & (kpos >= lo) & (kpos < nblk * BL)
    mask = band | (col >= nloc)
    rhead = lax.broadcasted_iota(jnp.int32, (SWA_GROUP * BL, 1), 0) // BL
    outs = [None] * SWA_Q_HEADS
    for g in range(SWA_KV_HEADS):
        hd = slice(g * SWA_HEAD_DIM, (g + 1) * SWA_HEAD_DIM)
        qs = jnp.concatenate([q[:, (g * SWA_GROUP + j) * SWA_HEAD_DIM:(g * SWA_GROUP + j + 1) * SWA_HEAD_DIM]
                              for j in range(SWA_GROUP)], axis=0)
        s = jnp.where(mask, _mm_nt(qs, keys[:, hd]), NEG_INF)
        sk = jnp.zeros((SWA_GROUP * BL, 1), F32)
        for j in range(SWA_GROUP):
            sk = jnp.where(rhead == j, sink_ref[g * SWA_GROUP + j], sk)
        m = jnp.maximum(jnp.max(s, axis=-1, keepdims=True), sk)
        e = jnp.exp(s - m)
        den = jnp.sum(e, axis=-1, keepdims=True) + jnp.exp(sk - m)
        og = _mm(e, vals[:, hd]) / den
        for j in range(SWA_GROUP):
            outs[g * SWA_GROUP + j] = og[j * BL:(j + 1) * BL]
    o_ref[...] = jnp.concatenate(outs, axis=1)


def _swa(pq, pkv, cs, sink, nzb):
    B, L, _ = pq.shape
    BL = SWA_BLOCK
    nblk = L // BL
    lz = nzb * BL
    prev = lambda n: jnp.maximum(n - 1, 0)
    nxt = lambda n: jnp.minimum(n + 1, nblk - 1)
    kv_spec = lambda f: pl.BlockSpec((None, BL, SWA_KV_W), lambda b, n: (b, f(n), 0))
    cs_spec = lambda f: pl.BlockSpec((BL, 2 * LANE), lambda b, n: (f(n), 0))
    same = lambda n: n
    return pl.pallas_call(
        functools.partial(_swa_kernel, nzb),
        grid=(B, nblk),
        in_specs=[pl.BlockSpec((None, BL, SWA_Q_W), lambda b, n: (b, n, 0)),
                  kv_spec(prev), kv_spec(same), kv_spec(nxt),
                  pl.BlockSpec((None, lz, SWA_KV_W), lambda b, n: (b, 0, 0)),
                  cs_spec(same), cs_spec(prev), cs_spec(same), cs_spec(nxt),
                  pl.BlockSpec(memory_space=pltpu.SMEM)],
        out_specs=pl.BlockSpec((None, BL, MIX_W), lambda b, n: (b, n, 0)),
        out_shape=jax.ShapeDtypeStruct((B, L, MIX_W), F32),
        compiler_params=_params("parallel", "parallel"),
        name="swa",
    )(pq, pkv, pkv, pkv, pkv, cs, cs, cs, cs, sink)


def _rwkv_feat_kernel(nzt, p_ref, pp_ref, pn_ref, vec_ref, ww_ref, wa_ref, wg_ref, e_ref,
                      o_r, o_v, o_kk, o_g, o_gb, o_lwf, o_kf, o_bf, o_lwb, o_kb, o_bb):
    i = pl.program_id(1)
    n = pl.num_programs(1)
    p = p_ref[...]
    TM = p.shape[0]
    first = (i == 0) | (i == nzt)
    last = (i == nzt - 1) | (i == n - 1)
    prow = jnp.where(first, 0.0, pp_ref[7:8, :])
    nrow = jnp.where(last, 0.0, pn_ref[0:1, :])
    rid = lax.broadcasted_iota(jnp.int32, (TM, 1), 0)
    up = jnp.where(rid == 0, prow, pltpu.roll(p, 1, 0))
    dn = jnp.where(rid == TM - 1, nrow, pltpu.roll(p, TM - 1, 0))
    vec = vec_ref[...]
    mu = jnp.concatenate([vec[8 + j:9 + j] for j in range(4)], axis=1)[:, :RWKV_W]
    pm = p + mu * (0.5 * (up + dn) - p)
    W = MIX_W
    r, k, v = pm[:, :W], pm[:, W:2 * W], pm[:, 2 * W:3 * W]
    wa = pm[:, 3 * W:3 * W + LANE]
    gl = pm[:, 3 * W + LANE:]
    e = e_ref[...]
    k_k, k_a, r_k = vec[4:5], vec[5:6], vec[6:7]
    g = _mm(_sigmoid(gl), wg_ref[...])
    kkn = k * k_k
    kk = kkn / jnp.maximum(jnp.sqrt(_segsum(kkn * kkn, e)), 1e-12)
    twa = jnp.tanh(wa)
    ksum = None
    for d, (o_lw, o_k, o_b) in enumerate(((o_lwf, o_kf, o_bf), (o_lwb, o_kb, o_bb))):
        w_raw = -_softplus(-(vec[d:d + 1] + _mm(twa, ww_ref[d]))) - 0.5
        a = _sigmoid(vec[2 + d:3 + d] + _mm(wa, wa_ref[d]))
        kd = k * (1.0 + (a - 1.0) * k_a)
        o_lw[...] = -jnp.exp(w_raw)
        o_k[...] = kd
        o_b[...] = kk * a
        ksum = kd if ksum is None else ksum + kd
    o_r[...] = r
    o_v[...] = v
    o_kk[...] = kk
    o_g[...] = g
    o_gb[...] = _segsum(r * ksum * r_k, e) * v * g


def _rwkv_feat(p, vec, ww, wa, wg, e, nzt):
    B, L, _ = p.shape
    TM = ROW_TILE
    r8 = TM // 8
    tok = pl.BlockSpec((None, TM, RWKV_W), lambda b, i: (b, i, 0))
    halo_p = pl.BlockSpec((None, 8, RWKV_W), lambda b, i: (b, jnp.maximum(i * r8 - 1, 0), 0))
    halo_n = pl.BlockSpec((None, 8, RWKV_W), lambda b, i: (b, jnp.minimum((i + 1) * r8, L // 8 - 1), 0))
    out = pl.BlockSpec((None, TM, MIX_W), lambda b, i: (b, i, 0))
    return pl.pallas_call(
        functools.partial(_rwkv_feat_kernel, nzt),
        grid=(B, L // TM),
        in_specs=[tok, halo_p, halo_n, _full(vec.shape), _full(ww.shape), _full(wa.shape), _full(wg.shape), _full(e.shape)],
        out_specs=[out] * 11,
        out_shape=[jax.ShapeDtypeStruct((B, L, MIX_W), F32)] * 11,
        compiler_params=_params("parallel", "parallel"),
        name="rwkv_feat",
    )(p, p, p, vec, ww, wa, wg, e)


def _rwkv_scan_kernel(reverse, final, *refs):
    if final:
        r_ref, v_ref, kk_ref, lw_ref, k_ref, b_ref, yprev_ref, g_ref, gb_ref, gn_ref, o_ref, h_ref = refs
    else:
        r_ref, v_ref, kk_ref, lw_ref, k_ref, b_ref, o_ref, h_ref = refs

    @pl.when(pl.program_id(1) == 0)
    def _():
        h_ref[...] = jnp.zeros_like(h_ref)

    lw = lw_ref[...]
    C = lw.shape[0]
    N = RWKV_HEAD
    strict, incl, eye = _order_masks(C, reverse)
    cum = _mm_hi(incl.astype(F32), lw)
    tot = jnp.sum(lw, axis=0, keepdims=True)
    e_inv = jnp.exp(-cum)
    e_end = jnp.exp(tot - cum)
    e_tot = jnp.exp(tot)
    kap_t = kk_ref[...] * jnp.exp(cum - lw)
    r_t = r_ref[...] * jnp.exp(cum)
    k_all, b_all, v_all = k_ref[...], b_ref[...], v_ref[...]
    k_t, b_t = k_all * e_inv, b_all * e_inv
    k_e, b_e = k_all * e_end, b_all * e_end
    eye_n = eye[:N, :N]
    heads = range(RWKV_HEADS)
    sls = [slice(h * N, (h + 1) * N) for h in heads]
    hs = [h_ref[h] for h in heads]
    vh = [v_all[:, sl] for sl in sls]
    G = [_mm_nt(jnp.concatenate([kap_t[:, sl], r_t[:, sl]], axis=0),
                jnp.concatenate([b_t[:, sl], k_t[:, sl]], axis=0)) for sl in sls]
    a_ab = [jnp.where(strict, g[:C, :C], 0.0) for g in G]
    a_ak = [jnp.where(strict, g[:C, C:], 0.0) for g in G]
    a_rb = [jnp.where(incl, g[C:, :C], 0.0) for g in G]
    a_rk = [jnp.where(incl, g[C:, C:], 0.0) for g in G]
    akv = [_mm(a_ak[h], vh[h]) for h in heads]
    pw = [_mm(a_ab[h], a_ab[h]) for h in heads]
    x = [jnp.concatenate([kap_t[:, sls[h]], akv[h]], axis=1) for h in heads]
    x = [x[h] - _mm(a_ab[h], x[h]) for h in heads]
    span = 2
    while span < C:
        x = [x[h] + _mm(pw[h], x[h]) for h in heads]
        span *= 2
        if span < C:
            pw = [_mm(pw[h], pw[h]) for h in heads]
    w_h = [xx[:, :N] for xx in x]
    vu = [jnp.concatenate([vh[h], x[h][:, N:]], axis=0) for h in heads]
    bw = [_mm_tn(b_e[:, sls[h]], w_h[h]) for h in heads]
    nn = [_mm_tn(jnp.concatenate([k_e[:, sls[h]], -b_e[:, sls[h]]], axis=0), vu[h]) for h in heads]
    rw = [r_t[:, sls[h]] - _mm(a_rb[h], w_h[h]) for h in heads]
    yloc = [_mm(jnp.concatenate([a_rk[h], -a_rb[h]], axis=1), vu[h]) for h in heads]
    dec = [jnp.sum(jnp.where(eye_n, e_tot[:, sl], 0.0), axis=1, keepdims=True) for sl in sls]
    ys = [_mm(rw[h], hs[h]) + yloc[h] for h in heads]
    h_new = [dec[h] * hs[h] - _mm(bw[h], hs[h]) + nn[h] for h in heads]
    for h in heads:
        h_ref[h] = h_new[h]
    if final:
        yprev = yprev_ref[...]
        outs = []
        for h in range(RWKV_HEADS):
            y = ys[h] + yprev[:, h * N:(h + 1) * N]
            mean = jnp.mean(y, axis=-1, keepdims=True)
            yc = y - mean
            var = jnp.mean(yc * yc, axis=-1, keepdims=True)
            outs.append(yc * lax.rsqrt(var + RWKV_GN_EPS))
        gn = gn_ref[...]
        o_ref[...] = (jnp.concatenate(outs, axis=1) * gn[0:1] + gn[1:2]) * g_ref[...] + gb_ref[...]
    else:
        o_ref[...] = jnp.concatenate(ys, axis=1)


def _rwkv_scan(r, v, kk, lw, k, b, nzc, reverse, yprev=None, g=None, gb=None, gn=None):
    B, L, _ = r.shape
    C = SCAN_CHUNK
    nc = L // C
    cidx = _chunk_index(reverse, nzc, nc)
    tok = pl.BlockSpec((None, C, MIX_W), lambda b_, c: (b_, cidx(c), 0))
    final = yprev is not None
    ins = [r, v, kk, lw, k, b]
    specs = [tok] * 6
    if final:
        ins += [yprev, g, gb, gn]
        specs += [tok] * 3 + [_full(gn.shape)]
    return pl.pallas_call(
        functools.partial(_rwkv_scan_kernel, reverse, final),
        grid=(B, nc),
        in_specs=specs,
        out_specs=tok,
        out_shape=jax.ShapeDtypeStruct((B, L, MIX_W), F32),
        scratch_shapes=[pltpu.VMEM((RWKV_HEADS, RWKV_HEAD, RWKV_HEAD), F32)],
        compiler_params=_params("parallel", "arbitrary"),
        name="rwkv_bwd" if reverse else "rwkv_fwd",
    )(*ins)


def _merge_kernel(s_ref, mod_ref, g_ref, ya_ref, yb_ref, yc_ref, wg_ref, wb_ref, wo_ref, o_ref):
    m = mod_ref[...]
    s = s_ref[...]
    h = _norm_mod(s, g_ref[...], m[0:1], m[1:2]).astype(BF16)
    gates = jnp.dot(h, wg_ref[...], preferred_element_type=F32)
    acc = None
    for i, y_ref in enumerate((ya_ref, yb_ref, yc_ref)):
        t = _sigmoid(gates[:, i * D_MODEL:(i + 1) * D_MODEL]) * _mm(y_ref[...], wb_ref[i])
        acc = t if acc is None else acc + t
    o_ref[...] = s + m[2:3] * _mm(acc, wo_ref[...])


def _merge(s, mod3, g, ya, yb, yc, wg, wb, wo, nzt, off):
    B, L, _ = s.shape
    TM = ROW_TILE
    nt = L // TM - off
    tok_in = lambda wd: pl.BlockSpec((None, TM, wd), lambda b, i: (b, i + off, 0))
    return pl.pallas_call(
        _merge_kernel,
        grid=(B, nt),
        in_specs=[tok_in(D_MODEL), _mod_spec(nzt, B, off), _full((1, D_MODEL)), tok_in(MIX_W), tok_in(MIX_W), tok_in(MIX_W),
                  _full(wg.shape), _full(wb.shape), _full(wo.shape)],
        out_specs=pl.BlockSpec((None, TM, D_MODEL), lambda b, i: (b, i, 0)),
        out_shape=jax.ShapeDtypeStruct((B, nt * TM, D_MODEL), F32),
        compiler_params=_params("parallel", "parallel"),
        name="merge",
    )(s, mod3, g, ya, yb, yc, wg, wb, wo)


def _ffn_kernel(final, s_ref, mod_ref, g_ref, w1_ref, w3_ref, w2_ref, gf_ref, o_ref):
    m = mod_ref[...]
    s = s_ref[...]
    h = _norm_mod(s, g_ref[...], m[3:4], m[4:5]).astype(BF16)
    a = jnp.dot(h, w1_ref[...], preferred_element_type=F32)
    b = jnp.dot(h, w3_ref[...], preferred_element_type=F32)
    o = s + m[5:6] * _mm(_silu(a) * b, w2_ref[...])
    if final:
        o = o * lax.rsqrt(jnp.mean(o * o, axis=-1, keepdims=True) + NORM_EPS) * gf_ref[...]
    o_ref[...] = o


def _ffn(s, mod3, g, w1, w3, w2, gf, nzt, off, final):
    B, L, _ = s.shape
    TM = ROW_TILE
    nt = L // TM
    const = lambda shape: pl.BlockSpec(shape, lambda b, i: (0,) * len(shape), pipeline_mode=pl.Buffered(1))
    return pl.pallas_call(
        functools.partial(_ffn_kernel, final),
        grid=(B, nt),
        in_specs=[pl.BlockSpec((None, TM, D_MODEL), lambda b, i: (b, i, 0)), _mod_spec(nzt, B, off), _full((1, D_MODEL)),
                  const(w1.shape), const(w3.shape), const(w2.shape), _full((1, D_MODEL))],
        out_specs=pl.BlockSpec((None, TM, D_MODEL), lambda b, i: (b, i, 0)),
        out_shape=jax.ShapeDtypeStruct((B, L, D_MODEL), F32),
        compiler_params=_params("parallel", "parallel"),
        name="ffn",
    )(s, mod3, g, w1, w3, w2, gf)


def _rope_tables(T, lz):
    rows = T // GRID_W
    row = jnp.repeat(jnp.arange(rows), GRID_W).astype(F32)
    col = jnp.tile(jnp.arange(GRID_W), rows).astype(F32)
    inv = ROPE_BASE ** (-jnp.arange(ROPE_FREQS, dtype=F32) / ROPE_FREQS)
    ar, ac = row[:, None] * inv, col[:, None] * inv
    cos = jnp.concatenate([jnp.cos(ar), jnp.cos(ar), jnp.cos(ac), jnp.cos(ac)], axis=1)
    sin = jnp.concatenate([-jnp.sin(ar), jnp.sin(ar), -jnp.sin(ac), jnp.sin(ac)], axis=1)
    reps = LANE // SWA_HEAD_DIM
    x_tab = jnp.concatenate([jnp.tile(cos, (1, reps)), jnp.tile(sin, (1, reps))], axis=1)
    z_tab = jnp.concatenate([jnp.ones((lz, LANE), F32), jnp.zeros((lz, LANE), F32)], axis=1)
    return jnp.concatenate([z_tab, x_tab], axis=0)


def _partner_cols(w):
    return w.reshape(w.shape[0], -1, 2, ROPE_FREQS)[:, :, ::-1, :].reshape(w.shape)


def _mixer_weight(w_in):
    g0, g1, g2 = np.cumsum(GROUP_SPLIT)[:3]
    gla, swa, rwkv = w_in[:, :g0], w_in[:, g0:g1], w_in[:, g1:g2]
    nk = GLA_HEADS * GLA_DK
    a = 2 * nk + MIX_W
    gla_p = jnp.concatenate([gla[:, :a], gla[:, a + 2 * GLA_GATE_RANK:], gla[:, a:a + 2 * GLA_GATE_RANK],
                             jnp.zeros((D_MODEL, LANE - 2 * GLA_GATE_RANK), F32)], axis=1)
    qw = SWA_Q_HEADS * SWA_HEAD_DIM
    kvw = SWA_KV_HEADS * SWA_HEAD_DIM
    q, k, v = swa[:, :qw], swa[:, qw:qw + kvw], swa[:, qw + kvw:]
    w = jnp.concatenate([gla_p, q, _partner_cols(q), k, _partner_cols(k), v, rwkv], axis=1)
    return w.astype(BF16), w_in[:, g2:].astype(BF16)


def _pad_rows(w, top, total):
    return jnp.concatenate([jnp.zeros((top, w.shape[1]), F32), w, jnp.zeros((total - top - w.shape[0], w.shape[1]), F32)], axis=0)


def kernel(x, c, ctx, c_ctx, w_ada, b_ada, g_mix, g_ffn, w_in, w_gk2, b_gk, g_gla, sink, mu_shift, w0, w_w2, a0, w_a2,
           w_g2, k_k, k_a, r_k, gn_w, gn_b, w_branch, w_out, w_ffn1, w_ffn3, w_ffn2, g_final):
    B, T, D = x.shape
    lz = ctx.shape[1]
    depth = w_in.shape[0]
    assert D == D_MODEL and lz % ROW_TILE == 0 and T % ROW_TILE == 0 and T % GRID_W == 0
    nzt, nzc, nzb = lz // ROW_TILE, lz // SCAN_CHUNK, lz // SWA_BLOCK
    s = jnp.concatenate([ctx, x], axis=1)
    cs = _rope_tables(T, lz)
    c8 = jnp.concatenate([c, c_ctx[None], jnp.zeros((8 - (B + 1) % 8 if (B + 1) % 8 else 0, D), F32)], axis=0)
    hidx = np.arange(MIX_W) // RWKV_HEAD
    seg = jnp.asarray(hidx[:, None] == hidx[None, :], BF16)
    out = None
    for l in range(depth):
        last = l == depth - 1
        mod3 = _ada(c8, w_ada[l], b_ada[l]).reshape(c8.shape[0], 6, D)
        w_mix, w_gate = _mixer_weight(w_in[l])
        p_gla, p_q, p_kv, p_rwkv = _inproj(s, mod3, g_mix[l][None], w_mix, nzt)

        wgk = [_pad_rows(w_gk2[l, d], d * GLA_GATE_RANK, LANE).astype(BF16) for d in range(2)]
        o_f = _gla_scan(p_gla, wgk[0], b_gk[l, 0][None], nzc, False)
        ya = _gla_scan(p_gla, wgk[1], b_gk[l, 1][None], nzc, True, o_f, g_gla[l][None])

        yb = _swa(p_q, p_kv, cs, sink[l], nzb)

        zero = jnp.zeros((MIX_W,), F32)
        mu4 = jnp.concatenate([mu_shift[l], jnp.zeros((4 * MIX_W - RWKV_W,), F32)]).reshape(4, MIX_W)
        vec = jnp.stack([w0[l, 0], w0[l, 1], a0[l, 0], a0[l, 1], k_k[l], k_a[l], r_k[l].reshape(-1), zero], axis=0)
        vec = jnp.concatenate([vec, mu4, jnp.zeros((4, MIX_W), F32)], axis=0)
        ww = jnp.stack([_pad_rows(w_w2[l, d], 0, LANE) for d in range(2)]).astype(BF16)
        wa = jnp.stack([_pad_rows(w_a2[l, d], RWKV_DECAY_RANK, LANE) for d in range(2)]).astype(BF16)
        f_r, f_v, f_kk, f_g, f_gb, lw_f, k_f, b_f, lw_b, k_b, b_b = _rwkv_feat(
            p_rwkv, vec, ww, wa, w_g2[l].astype(BF16), seg, nzt)
        y_f = _rwkv_scan(f_r, f_v, f_kk, lw_f, k_f, b_f, nzc, False)
        yc = _rwkv_scan(f_r, f_v, f_kk, lw_b, k_b, b_b, nzc, True, y_f, f_g, f_gb, jnp.stack([gn_w[l], gn_b[l]]))

        off = nzt if last else 0
        s_mix = _merge(s, mod3, g_mix[l][None], ya, yb, yc, w_gate, w_branch[l].astype(BF16), w_out[l].astype(BF16), nzt, off)
        s = _ffn(s_mix, mod3, g_ffn[l][None], w_ffn1[l].astype(BF16), w_ffn3[l].astype(BF16), w_ffn2[l].astype(BF16),
                 g_final[None], nzt, off, last)
        out = s
    return out
```

```python
import functools

import jax
import jax.numpy as jnp
import numpy as np
from jax import lax
from jax.experimental import pallas as pl
from jax.experimental.pallas import tpu as pltpu

F32 = jnp.float32
BF16 = jnp.bfloat16
HI = lax.Precision.HIGHEST

D_MODEL = 1024
GRID_W = 64
NORM_EPS = 1e-6
NEG_INF = -1e30
MIX_W = D_MODEL // 2
N_BRANCH = 3
GLA_HEADS = 4
GLA_DV = MIX_W // GLA_HEADS
GLA_DK = GLA_DV // 2
GLA_GATE_RANK = 16
GLA_GATE_NORM = 16.0
SWA_HEAD_DIM = 64
SWA_Q_HEADS = MIX_W // SWA_HEAD_DIM
SWA_KV_HEADS = SWA_Q_HEADS // 4
SWA_GROUP = SWA_Q_HEADS // SWA_KV_HEADS
SWA_WINDOW = 128
SWA_BLOCK = 128
ROPE_FREQS = SWA_HEAD_DIM // 4
ROPE_BASE = 10000.0
RWKV_HEAD = 64
RWKV_HEADS = MIX_W // RWKV_HEAD
RWKV_DECAY_RANK = 64
RWKV_A_RANK = 64
RWKV_GATE_RANK = 128
RWKV_GN_EPS = 64e-5
FFN_HIDDEN = -(-8 * D_MODEL // (3 * 256)) * 256
GLA_SPLIT = (GLA_HEADS * GLA_DK, GLA_HEADS * GLA_DK, MIX_W, GLA_GATE_RANK, GLA_GATE_RANK, MIX_W)
SWA_SPLIT = (SWA_Q_HEADS * SWA_HEAD_DIM, SWA_KV_HEADS * SWA_HEAD_DIM, SWA_KV_HEADS * SWA_HEAD_DIM)
RWKV_SPLIT = (MIX_W, MIX_W, MIX_W, RWKV_DECAY_RANK, RWKV_A_RANK, RWKV_GATE_RANK)
GROUP_SPLIT = (sum(GLA_SPLIT), sum(SWA_SPLIT), sum(RWKV_SPLIT), N_BRANCH * D_MODEL)

LANE = 128
ROW_TILE = 256
SCAN_CHUNK = 64
PAIR_HEAD = LANE // 2
assert GLA_DK == PAIR_HEAD and RWKV_HEAD == PAIR_HEAD and SCAN_CHUNK == PAIR_HEAD
GLA_W = 2 * GLA_HEADS * GLA_DK + 2 * MIX_W + LANE
SWA_Q_W = 1024
SWA_KV_W = 384
RWKV_W = sum(RWKV_SPLIT)
MIXER_W = GLA_W + SWA_Q_W + SWA_KV_W + RWKV_W
VMEM_LIMIT = 56 * 1024 * 1024


def _mm(a, b):
    return jnp.dot(a.astype(BF16), b.astype(BF16), preferred_element_type=F32)


def _mm_nt(a, b):
    return lax.dot_general(a.astype(BF16), b.astype(BF16), (((1,), (1,)), ((), ())), preferred_element_type=F32)


def _mm_tn(a, b):
    return lax.dot_general(a.astype(BF16), b.astype(BF16), (((0,), (0,)), ((), ())), preferred_element_type=F32)


def _mm_hi(a, b):
    return jnp.dot(a, b, precision=HI, preferred_element_type=F32)


def _sigmoid(x):
    return 1.0 / (1.0 + jnp.exp(-x))


def _silu(x):
    return x * _sigmoid(x)


def _softplus(x):
    return jnp.maximum(x, 0.0) + jnp.log(1.0 + jnp.exp(-jnp.abs(x)))


def _norm_mod(x, g, shift, scale):
    y = x * lax.rsqrt(jnp.mean(x * x, axis=-1, keepdims=True) + NORM_EPS)
    return (y * g) * (1.0 + scale) + shift


def _segsum(x, e):
    hi = x.astype(BF16)
    lo = (x - hi.astype(F32)).astype(BF16)
    return jnp.dot(hi, e, preferred_element_type=F32) + jnp.dot(lo, e, preferred_element_type=F32)


def _params(*sem):
    return pltpu.CompilerParams(dimension_semantics=sem, vmem_limit_bytes=VMEM_LIMIT)


def _full(shape):
    nd = len(shape)
    return pl.BlockSpec(shape, lambda *_: (0,) * nd)


def _ada_kernel(c_ref, w_ref, b_ref, o_ref):
    o_ref[...] = _mm_hi(_silu(c_ref[...]), w_ref[...]) + b_ref[...]


def _ada(c8, w, b):
    n = w.shape[1]
    tn = 1536
    return pl.pallas_call(
        _ada_kernel,
        grid=(n // tn,),
        in_specs=[_full(c8.shape), pl.BlockSpec((D_MODEL, tn), lambda j: (0, j)), pl.BlockSpec((1, tn), lambda j: (0, j))],
        out_specs=pl.BlockSpec((c8.shape[0], tn), lambda j: (0, j)),
        out_shape=jax.ShapeDtypeStruct((c8.shape[0], n), F32),
        compiler_params=_params("arbitrary"),
        name="ada",
    )(c8, w, b.reshape(1, n))


def _inproj_kernel(s_ref, mod_ref, g_ref, w_ref, o_gla, o_q, o_kv, o_rwkv):
    m = mod_ref[...]
    h = _norm_mod(s_ref[...], g_ref[...], m[0:1], m[1:2]).astype(BF16)
    p = jnp.dot(h, w_ref[...], preferred_element_type=F32)
    o0 = 0
    for o_ref in (o_gla, o_q, o_kv, o_rwkv):
        wdt = o_ref.shape[-1]
        o_ref[...] = p[:, o0:o0 + wdt]
        o0 += wdt


def _mod_spec(nzt, nb, off=0):
    return pl.BlockSpec((None, 6, D_MODEL), lambda b, i: (jnp.where(i + off < nzt, nb, b), 0, 0))


def _inproj(s, mod3, g, w, nzt):
    B, L, _ = s.shape
    widths = (GLA_W, SWA_Q_W, SWA_KV_W, RWKV_W)
    return pl.pallas_call(
        _inproj_kernel,
        grid=(B, L // ROW_TILE),
        in_specs=[pl.BlockSpec((None, ROW_TILE, D_MODEL), lambda b, i: (b, i, 0)), _mod_spec(nzt, B),
                  _full((1, D_MODEL)), _full(w.shape)],
        out_specs=[pl.BlockSpec((None, ROW_TILE, wd), lambda b, i: (b, i, 0)) for wd in widths],
        out_shape=[jax.ShapeDtypeStruct((B, L, wd), F32) for wd in widths],
        compiler_params=_params("parallel", "parallel"),
        name="inproj",
    )(s, mod3, g, w)


def _chunk_index(reverse, nzc, nc):
    if not reverse:
        return lambda c: c
    return lambda c: jnp.where(c < nzc, nzc - 1 - c, nzc + nc - 1 - c)


def _order_masks(n, reverse):
    ii = lax.broadcasted_iota(jnp.int32, (n, n), 0)
    jj = lax.broadcasted_iota(jnp.int32, (n, n), 1)
    strict = (jj > ii) if reverse else (jj < ii)
    return strict, strict | (ii == jj), ii == jj


def _pair_order_masks(n, reverse):
    ii = lax.broadcasted_iota(jnp.int32, (n, LANE), 0)
    jj = lax.broadcasted_iota(jnp.int32, (n, LANE), 1) & (PAIR_HEAD - 1)
    strict = (jj > ii) if reverse else (jj < ii)
    return strict, strict | (ii == jj)


def _first_head(width, head_w):
    return (lax.broadcasted_iota(jnp.int32, (1, width), 1) & head_w) == 0


def _block_diag(z, first):
    return jnp.concatenate([jnp.where(first, z, 0.0), jnp.where(first, 0.0, z)], axis=0)


def _diag_blocks(rows, head_w_cols, cols):
    ri = lax.broadcasted_iota(jnp.int32, (rows, cols), 0) // (rows // 2)
    ci = lax.broadcasted_iota(jnp.int32, (rows, cols), 1) // head_w_cols
    return ri == ci


def _row_vector_to_column(row):
    n = row.shape[1]
    eye = lax.broadcasted_iota(jnp.int32, (n, n), 0) == lax.broadcasted_iota(jnp.int32, (n, n), 1)
    return jnp.sum(jnp.where(eye, row, 0.0), axis=1, keepdims=True)


def _gla_kernel(reverse, final, *refs):
    if final:
        p_ref, wgk_ref, bgk_ref, oprev_ref, ggla_ref, o_ref, h_ref = refs
    else:
        p_ref, wgk_ref, bgk_ref, o_ref, h_ref = refs

    @pl.when(pl.program_id(1) == 0)
    def _():
        h_ref[...] = jnp.zeros_like(h_ref)

    NB, C = p_ref.shape[0], p_ref.shape[1]
    nk = GLA_HEADS * GLA_DK
    npair = GLA_HEADS // 2
    kw, vw = 2 * GLA_DK, 2 * GLA_DV
    _, incl = _order_masks(C, reverse)[:2]
    _, incl2 = _pair_order_masks(C, reverse)
    first_k = _first_head(kw, GLA_DK)
    first_v = _first_head(vw, GLA_DV)
    diag = _diag_blocks(kw, GLA_DV, vw)
    inclf = incl.astype(F32)
    units = [(nb, pr) for nb in range(NB) for pr in range(npair)]
    q_t, k_t, k_e, e_tot, v, og = {}, {}, {}, {}, {}, {}
    for nb in range(NB):
        p = p_ref[nb]
        q = p[:, 0:nk] * GLA_DK ** -0.5
        k = p[:, nk:2 * nk]
        v[nb] = p[:, 2 * nk:2 * nk + MIX_W]
        og[nb] = p[:, 2 * nk + MIX_W:2 * nk + 2 * MIX_W]
        gk = p[:, 2 * nk + 2 * MIX_W:]
        lg = -_softplus(-(_mm(gk, wgk_ref[...]) + bgk_ref[...])) / GLA_GATE_NORM
        cum = _mm_hi(inclf, lg)
        tot = jnp.sum(lg, axis=0, keepdims=True)
        q_t[nb] = q * jnp.exp(cum)
        k_t[nb] = k * jnp.exp(-cum)
        k_e[nb] = k * jnp.exp(tot - cum)
        e_tot[nb] = jnp.exp(tot)
    ksl = lambda pr: slice(pr * kw, (pr + 1) * kw)
    vsl = lambda pr: slice(pr * vw, (pr + 1) * vw)
    hs = {u: h_ref[u[0], u[1]] for u in units}
    vp = {u: v[u[0]][:, vsl(u[1])] for u in units}
    att = {u: jnp.where(incl2, _mm_nt(q_t[u[0]][:, ksl(u[1])], _block_diag(k_t[u[0]][:, ksl(u[1])], first_k)), 0.0)
           for u in units}
    inter = {u: _mm(q_t[u[0]][:, ksl(u[1])], hs[u]) for u in units}
    kv = {u: jnp.where(diag, _mm_tn(k_e[u[0]][:, ksl(u[1])], vp[u]), 0.0) for u in units}
    outs = {u: _mm(att[u], _block_diag(vp[u], first_v)) + inter[u] for u in units}
    for u in units:
        h_ref[u[0], u[1]] = _row_vector_to_column(e_tot[u[0]][:, ksl(u[1])]) * hs[u] + kv[u]
    for nb in range(NB):
        o = jnp.concatenate([outs[(nb, pr)] for pr in range(npair)], axis=1)
        if final:
            o = o + oprev_ref[nb]
            g = ggla_ref[...]
            ys = []
            for h in range(GLA_HEADS):
                oh = o[:, h * GLA_DV:(h + 1) * GLA_DV]
                ys.append(oh * lax.rsqrt(jnp.mean(oh * oh, axis=-1, keepdims=True) + NORM_EPS) * g)
            o = jnp.concatenate(ys, axis=1) * _silu(og[nb])
        o_ref[nb] = o


def _scan_batch(B):
    return 2 if B % 2 == 0 else 1


def _gla_scan(p, wgk, bgk, nzc, reverse, oprev=None, ggla=None):
    B, L, _ = p.shape
    C = SCAN_CHUNK
    NB = _scan_batch(B)
    nc = L // C
    cidx = _chunk_index(reverse, nzc, nc)
    tok = lambda wd: pl.BlockSpec((NB, C, wd), lambda b, c: (b, cidx(c), 0))
    final = oprev is not None
    ins = [p, wgk, bgk]
    specs = [tok(GLA_W), _full(wgk.shape), _full(bgk.shape)]
    if final:
        ins += [oprev, ggla]
        specs += [tok(MIX_W), _full(ggla.shape)]
    return pl.pallas_call(
        functools.partial(_gla_kernel, reverse, final),
        grid=(B // NB, nc),
        in_specs=specs,
        out_specs=tok(MIX_W),
        out_shape=jax.ShapeDtypeStruct((B, L, MIX_W), F32),
        scratch_shapes=[pltpu.VMEM((NB, GLA_HEADS // 2, 2 * GLA_DK, 2 * GLA_DV), F32)],
        compiler_params=_params("parallel", "arbitrary"),
        name="gla_bwd" if reverse else "gla_fwd",
    )(*ins)


def _swa_kernel(nzb, q_ref, kvm_ref, kv0_ref, kvp_ref, kvz_ref, csq_ref, csm_ref, cs0_ref, csp_ref, sink_ref, o_ref):
    n = pl.program_id(1)
    nblk = pl.num_programs(1)
    BL = SWA_BLOCK
    kvw = SWA_KV_HEADS * SWA_HEAD_DIM
    qw = SWA_Q_HEADS * SWA_HEAD_DIM

    def rope(x, xs, cs, reps):
        cos = jnp.concatenate([cs[:, :LANE]] * reps, axis=1)
        sin = jnp.concatenate([cs[:, LANE:]] * reps, axis=1)
        return x * cos + xs * sin

    qq = q_ref[...]
    q = rope(qq[:, :qw], qq[:, qw:], csq_ref[...], qw // LANE) * SWA_HEAD_DIM ** -0.5
    ks, vs = [], []
    for kv_ref, cs_ref in ((kvm_ref, csm_ref), (kv0_ref, cs0_ref), (kvp_ref, csp_ref)):
        kv = kv_ref[...]
        ks.append(rope(kv[:, :kvw], kv[:, kvw:2 * kvw], cs_ref[...], kvw // LANE))
        vs.append(kv[:, 2 * kvw:])
    kvz = kvz_ref[...]
    ks.append(kvz[:, :kvw])
    vs.append(kvz[:, 2 * kvw:])
    keys = jnp.concatenate(ks, axis=0)
    vals = jnp.concatenate(vs, axis=0)
    nloc = 3 * BL
    nkeys = keys.shape[0]
    qpos = n * BL + (lax.broadcasted_iota(jnp.int32, (SWA_GROUP * BL, nkeys), 0) & (BL - 1))
    col = lax.broadcasted_iota(jnp.int32, (SWA_GROUP * BL, nkeys), 1)
    kpos = (n - 1) * BL + col
    lo = jnp.where(n >= nzb, nzb * BL, nblk * BL)
    band = (jnp.abs(kpos - qpos) <= SWA_WINDOW) & (kpos >= lo) & (kpos < nblk * BL)
    mask = band | (col >= nloc)
    rhead = lax.broadcasted_iota(jnp.int32, (SWA_GROUP * BL, 1), 0) // BL
    outs = [None] * SWA_Q_HEADS
    for g in range(SWA_KV_HEADS):
        hd = slice(g * SWA_HEAD_DIM, (g + 1) * SWA_HEAD_DIM)
        qs = jnp.concatenate([q[:, (g * SWA_GROUP + j) * SWA_HEAD_DIM:(g * SWA_GROUP + j + 1) * SWA_HEAD_DIM]
                              for j in range(SWA_GROUP)], axis=0)
        s = jnp.where(mask, _mm_nt(qs, keys[:, hd]), NEG_INF)
        sk = jnp.zeros((SWA_GROUP * BL, 1), F32)
        for j in range(SWA_GROUP):
            sk = jnp.where(rhead == j, sink_ref[g * SWA_GROUP + j], sk)
        m = jnp.maximum(jnp.max(s, axis=-1, keepdims=True), sk)
        e = jnp.exp(s - m)
        den = jnp.sum(e, axis=-1, keepdims=True) + jnp.exp(sk - m)
        og = _mm(e, vals[:, hd]) / den
        for j in range(SWA_GROUP):
            outs[g * SWA_GROUP + j] = og[j * BL:(j + 1) * BL]
    o_ref[...] = jnp.concatenate(outs, axis=1)


def _swa(pq, pkv, cs, sink, nzb):
    B, L, _ = pq.shape
    BL = SWA_BLOCK
    nblk = L // BL
    lz = nzb * BL
    prev = lambda n: jnp.maximum(n - 1, 0)
    nxt = lambda n: jnp.minimum(n + 1, nblk - 1)
    kv_spec = lambda f: pl.BlockSpec((None, BL, SWA_KV_W), lambda b, n: (b, f(n), 0))
    cs_spec = lambda f: pl.BlockSpec((BL, 2 * LANE), lambda b, n: (f(n), 0))
    same = lambda n: n
    return pl.pallas_call(
        functools.partial(_swa_kernel, nzb),
        grid=(B, nblk),
        in_specs=[pl.BlockSpec((None, BL, SWA_Q_W), lambda b, n: (b, n, 0)),
                  kv_spec(prev), kv_spec(same), kv_spec(nxt),
                  pl.BlockSpec((None, lz, SWA_KV_W), lambda b, n: (b, 0, 0)),
                  cs_spec(same), cs_spec(prev), cs_spec(same), cs_spec(nxt),
                  pl.BlockSpec(memory_space=pltpu.SMEM)],
        out_specs=pl.BlockSpec((None, BL, MIX_W), lambda b, n: (b, n, 0)),
        out_shape=jax.ShapeDtypeStruct((B, L, MIX_W), F32),
        compiler_params=_params("parallel", "parallel"),
        name="swa",
    )(pq, pkv, pkv, pkv, pkv, cs, cs, cs, cs, sink)


def _rwkv_feat_kernel(nzt, p_ref, pp_ref, pn_ref, vec_ref, ww_ref, wa_ref, wg_ref, e_ref,
                      o_r, o_v, o_kk, o_g, o_gb, o_lwf, o_kf, o_bf, o_lwb, o_kb, o_bb):
    i = pl.program_id(1)
    n = pl.num_programs(1)
    p = p_ref[...]
    TM = p.shape[0]
    first = (i == 0) | (i == nzt)
    last = (i == nzt - 1) | (i == n - 1)
    prow = jnp.where(first, 0.0, pp_ref[7:8, :])
    nrow = jnp.where(last, 0.0, pn_ref[0:1, :])
    rid = lax.broadcasted_iota(jnp.int32, (TM, 1), 0)
    up = jnp.where(rid == 0, prow, pltpu.roll(p, 1, 0))
    dn = jnp.where(rid == TM - 1, nrow, pltpu.roll(p, TM - 1, 0))
    vec = vec_ref[...]
    mu = jnp.concatenate([vec[8 + j:9 + j] for j in range(4)], axis=1)[:, :RWKV_W]
    pm = p + mu * (0.5 * (up + dn) - p)
    W = MIX_W
    r, k, v = pm[:, :W], pm[:, W:2 * W], pm[:, 2 * W:3 * W]
    wa = pm[:, 3 * W:3 * W + LANE]
    gl = pm[:, 3 * W + LANE:]
    e = e_ref[...]
    k_k, k_a, r_k = vec[4:5], vec[5:6], vec[6:7]
    g = _mm(_sigmoid(gl), wg_ref[...])
    kkn = k * k_k
    kk = kkn / jnp.maximum(jnp.sqrt(_segsum(kkn * kkn, e)), 1e-12)
    twa = jnp.tanh(wa)
    ksum = None
    for d, (o_lw, o_k, o_b) in enumerate(((o_lwf, o_kf, o_bf), (o_lwb, o_kb, o_bb))):
        w_raw = -_softplus(-(vec[d:d + 1] + _mm(twa, ww_ref[d]))) - 0.5
        a = _sigmoid(vec[2 + d:3 + d] + _mm(wa, wa_ref[d]))
        kd = k * (1.0 + (a - 1.0) * k_a)
        o_lw[...] = -jnp.exp(w_raw)
        o_k[...] = kd
        o_b[...] = kk * a
        ksum = kd if ksum is None else ksum + kd
    o_r[...] = r
    o_v[...] = v
    o_kk[...] = kk
    o_g[...] = g
    o_gb[...] = _segsum(r * ksum * r_k, e) * v * g


def _rwkv_feat(p, vec, ww, wa, wg, e, nzt):
    B, L, _ = p.shape
    TM = ROW_TILE
    r8 = TM // 8
    tok = pl.BlockSpec((None, TM, RWKV_W), lambda b, i: (b, i, 0))
    halo_p = pl.BlockSpec((None, 8, RWKV_W), lambda b, i: (b, jnp.maximum(i * r8 - 1, 0), 0))
    halo_n = pl.BlockSpec((None, 8, RWKV_W), lambda b, i: (b, jnp.minimum((i + 1) * r8, L // 8 - 1), 0))
    out = pl.BlockSpec((None, TM, MIX_W), lambda b, i: (b, i, 0))
    return pl.pallas_call(
        functools.partial(_rwkv_feat_kernel, nzt),
        grid=(B, L // TM),
        in_specs=[tok, halo_p, halo_n, _full(vec.shape), _full(ww.shape), _full(wa.shape), _full(wg.shape), _full(e.shape)],
        out_specs=[out] * 11,
        out_shape=[jax.ShapeDtypeStruct((B, L, MIX_W), F32)] * 11,
        compiler_params=_params("parallel", "parallel"),
        name="rwkv_feat",
    )(p, p, p, vec, ww, wa, wg, e)


def _rwkv_scan_kernel(reverse, final, *refs):
    if final:
        r_ref, v_ref, kk_ref, lw_ref, k_ref, b_ref, yprev_ref, g_ref, gb_ref, gn_ref, o_ref, h_ref = refs
    else:
        r_ref, v_ref, kk_ref, lw_ref, k_ref, b_ref, o_ref, h_ref = refs

    @pl.when(pl.program_id(1) == 0)
    def _():
        h_ref[...] = jnp.zeros_like(h_ref)

    NB, C = lw_ref.shape[0], lw_ref.shape[1]
    PW = LANE
    npair = RWKV_HEADS // 2
    _, incl = _order_masks(C, reverse)[:2]
    strict2, incl2 = _pair_order_masks(C, reverse)
    first = _first_head(PW, RWKV_HEAD)
    first2 = _first_head(2 * PW, RWKV_HEAD)
    diag = _diag_blocks(PW, RWKV_HEAD, PW)
    inclf = incl.astype(F32)
    units = [(nb, pr) for nb in range(NB) for pr in range(npair)]
    kap_t, r_t, k_t, b_t, k_e, b_e, e_tot, v_all = {}, {}, {}, {}, {}, {}, {}, {}
    for nb in range(NB):
        lw = lw_ref[nb]
        cum = _mm_hi(inclf, lw)
        tot = jnp.sum(lw, axis=0, keepdims=True)
        e_inv = jnp.exp(-cum)
        e_end = jnp.exp(tot - cum)
        e_tot[nb] = jnp.exp(tot)
        kap_t[nb] = kk_ref[nb] * jnp.exp(cum - lw)
        r_t[nb] = r_ref[nb] * jnp.exp(cum)
        k_all, b_all = k_ref[nb], b_ref[nb]
        k_t[nb], b_t[nb] = k_all * e_inv, b_all * e_inv
        k_e[nb], b_e[nb] = k_all * e_end, b_all * e_end
        v_all[nb] = v_ref[nb]
    sl = lambda pr: slice(pr * PW, (pr + 1) * PW)
    pick = lambda d, u: d[u[0]][:, sl(u[1])]
    hs = {u: h_ref[u[0], u[1]] for u in units}
    vp = {u: pick(v_all, u) for u in units}
    G = {u: _mm_nt(jnp.concatenate([pick(kap_t, u), pick(r_t, u)], axis=0),
                   jnp.concatenate([_block_diag(pick(b_t, u), first), _block_diag(pick(k_t, u), first)], axis=0))
         for u in units}
    a_ab = {u: jnp.where(strict2, G[u][:C, :PW], 0.0) for u in units}
    a_ak = {u: jnp.where(strict2, G[u][:C, PW:], 0.0) for u in units}
    a_rb = {u: jnp.where(incl2, G[u][C:, :PW], 0.0) for u in units}
    a_rk = {u: jnp.where(incl2, G[u][C:, PW:], 0.0) for u in units}
    akv = {u: _mm(a_ak[u], _block_diag(vp[u], first)) for u in units}
    pw = {u: _mm(a_ab[u], _block_diag(a_ab[u], first)) for u in units}
    x = {u: jnp.concatenate([pick(kap_t, u), akv[u]], axis=1) for u in units}
    x = {u: x[u] - _mm(a_ab[u], _block_diag(x[u], first2)) for u in units}
    span = 2
    while span < C:
        x = {u: x[u] + _mm(pw[u], _block_diag(x[u], first2)) for u in units}
        span *= 2
        if span < C:
            pw = {u: _mm(pw[u], _block_diag(pw[u], first)) for u in units}
    w_p = {u: x[u][:, :PW] for u in units}
    u_p = {u: x[u][:, PW:] for u in units}
    vu = {u: jnp.concatenate([vp[u], u_p[u]], axis=0) for u in units}
    bw = {u: jnp.where(diag, _mm_tn(pick(b_e, u), w_p[u]), 0.0) for u in units}
    nn = {u: jnp.where(diag, _mm_tn(jnp.concatenate([pick(k_e, u), -pick(b_e, u)], axis=0), vu[u]), 0.0) for u in units}
    rw = {u: pick(r_t, u) - _mm(a_rb[u], _block_diag(w_p[u], first)) for u in units}
    yloc = {u: _mm(jnp.concatenate([a_rk[u], -a_rb[u]], axis=1),
                   jnp.concatenate([_block_diag(vp[u], first), _block_diag(u_p[u], first)], axis=0)) for u in units}
    ys = {u: _mm(rw[u], hs[u]) + yloc[u] for u in units}
    for u in units:
        h_ref[u[0], u[1]] = _row_vector_to_column(pick(e_tot, u)) * hs[u] - _mm(bw[u], hs[u]) + nn[u]
    for nb in range(NB):
        if final:
            yprev = yprev_ref[nb]
            outs = []
            for pr in range(npair):
                y = ys[(nb, pr)] + yprev[:, sl(pr)]
                s_a = jnp.sum(jnp.where(first, y, 0.0), axis=-1, keepdims=True)
                s_b = jnp.sum(jnp.where(first, 0.0, y), axis=-1, keepdims=True)
                yc = y - jnp.where(first, s_a, s_b) * (1.0 / RWKV_HEAD)
                q = yc * yc
                v_a = jnp.sum(jnp.where(first, q, 0.0), axis=-1, keepdims=True)
                v_b = jnp.sum(jnp.where(first, 0.0, q), axis=-1, keepdims=True)
                outs.append(yc * lax.rsqrt(jnp.where(first, v_a, v_b) * (1.0 / RWKV_HEAD) + RWKV_GN_EPS))
            gn = gn_ref[...]
            o_ref[nb] = (jnp.concatenate(outs, axis=1) * gn[0:1] + gn[1:2]) * g_ref[nb] + gb_ref[nb]
        else:
            o_ref[nb] = jnp.concatenate([ys[(nb, pr)] for pr in range(npair)], axis=1)


def _rwkv_scan(r, v, kk, lw, k, b, nzc, reverse, yprev=None, g=None, gb=None, gn=None):
    B, L, _ = r.shape
    C = SCAN_CHUNK
    NB = _scan_batch(B)
    nc = L // C
    cidx = _chunk_index(reverse, nzc, nc)
    tok = pl.BlockSpec((NB, C, MIX_W), lambda b_, c: (b_, cidx(c), 0))
    final = yprev is not None
    ins = [r, v, kk, lw, k, b]
    specs = [tok] * 6
    if final:
        ins += [yprev, g, gb, gn]
        specs += [tok] * 3 + [_full(gn.shape)]
    return pl.pallas_call(
        functools.partial(_rwkv_scan_kernel, reverse, final),
        grid=(B // NB, nc),
        in_specs=specs,
        out_specs=tok,
        out_shape=jax.ShapeDtypeStruct((B, L, MIX_W), F32),
        scratch_shapes=[pltpu.VMEM((NB, RWKV_HEADS // 2, LANE, LANE), F32)],
        compiler_params=_params("parallel", "arbitrary"),
        name="rwkv_bwd" if reverse else "rwkv_fwd",
    )(*ins)


def _merge_kernel(s_ref, mod_ref, g_ref, ya_ref, yb_ref, yc_ref, wg_ref, wb_ref, wo_ref, o_ref):
    m = mod_ref[...]
    s = s_ref[...]
    h = _norm_mod(s, g_ref[...], m[0:1], m[1:2]).astype(BF16)
    gates = jnp.dot(h, wg_ref[...], preferred_element_type=F32)
    acc = None
    for i, y_ref in enumerate((ya_ref, yb_ref, yc_ref)):
        t = _sigmoid(gates[:, i * D_MODEL:(i + 1) * D_MODEL]) * _mm(y_ref[...], wb_ref[i])
        acc = t if acc is None else acc + t
    o_ref[...] = s + m[2:3] * _mm(acc, wo_ref[...])


def _merge(s, mod3, g, ya, yb, yc, wg, wb, wo, nzt, off):
    B, L, _ = s.shape
    TM = ROW_TILE
    nt = L // TM - off
    tok_in = lambda wd: pl.BlockSpec((None, TM, wd), lambda b, i: (b, i + off, 0))
    return pl.pallas_call(
        _merge_kernel,
        grid=(B, nt),
        in_specs=[tok_in(D_MODEL), _mod_spec(nzt, B, off), _full((1, D_MODEL)), tok_in(MIX_W), tok_in(MIX_W), tok_in(MIX_W),
                  _full(wg.shape), _full(wb.shape), _full(wo.shape)],
        out_specs=pl.BlockSpec((None, TM, D_MODEL), lambda b, i: (b, i, 0)),
        out_shape=jax.ShapeDtypeStruct((B, nt * TM, D_MODEL), F32),
        compiler_params=_params("parallel", "parallel"),
        name="merge",
    )(s, mod3, g, ya, yb, yc, wg, wb, wo)


def _ffn_kernel(final, s_ref, mod_ref, g_ref, w1_ref, w3_ref, w2_ref, gf_ref, o_ref):
    m = mod_ref[...]
    s = s_ref[...]
    h = _norm_mod(s, g_ref[...], m[3:4], m[4:5]).astype(BF16)
    a = jnp.dot(h, w1_ref[...], preferred_element_type=F32)
    b = jnp.dot(h, w3_ref[...], preferred_element_type=F32)
    o = s + m[5:6] * _mm(_silu(a) * b, w2_ref[...])
    if final:
        o = o * lax.rsqrt(jnp.mean(o * o, axis=-1, keepdims=True) + NORM_EPS) * gf_ref[...]
    o_ref[...] = o


def _ffn(s, mod3, g, w1, w3, w2, gf, nzt, off, final):
    B, L, _ = s.shape
    TM = ROW_TILE
    nt = L // TM
    const = lambda shape: pl.BlockSpec(shape, lambda b, i: (0,) * len(shape), pipeline_mode=pl.Buffered(1))
    return pl.pallas_call(
        functools.partial(_ffn_kernel, final),
        grid=(B, nt),
        in_specs=[pl.BlockSpec((None, TM, D_MODEL), lambda b, i: (b, i, 0)), _mod_spec(nzt, B, off), _full((1, D_MODEL)),
                  const(w1.shape), const(w3.shape), const(w2.shape), _full((1, D_MODEL))],
        out_specs=pl.BlockSpec((None, TM, D_MODEL), lambda b, i: (b, i, 0)),
        out_shape=jax.ShapeDtypeStruct((B, L, D_MODEL), F32),
        compiler_params=_params("parallel", "parallel"),
        name="ffn",
    )(s, mod3, g, w1, w3, w2, gf)


def _rope_tables(T, lz):
    rows = T // GRID_W
    row = jnp.repeat(jnp.arange(rows), GRID_W).astype(F32)
    col = jnp.tile(jnp.arange(GRID_W), rows).astype(F32)
    inv = ROPE_BASE ** (-jnp.arange(ROPE_FREQS, dtype=F32) / ROPE_FREQS)
    ar, ac = row[:, None] * inv, col[:, None] * inv
    cos = jnp.concatenate([jnp.cos(ar), jnp.cos(ar), jnp.cos(ac), jnp.cos(ac)], axis=1)
    sin = jnp.concatenate([-jnp.sin(ar), jnp.sin(ar), -jnp.sin(ac), jnp.sin(ac)], axis=1)
    reps = LANE // SWA_HEAD_DIM
    x_tab = jnp.concatenate([jnp.tile(cos, (1, reps)), jnp.tile(sin, (1, reps))], axis=1)
    z_tab = jnp.concatenate([jnp.ones((lz, LANE), F32), jnp.zeros((lz, LANE), F32)], axis=1)
    return jnp.concatenate([z_tab, x_tab], axis=0)


def _partner_cols(w):
    return w.reshape(w.shape[0], -1, 2, ROPE_FREQS)[:, :, ::-1, :].reshape(w.shape)


def _mixer_weight(w_in):
    g0, g1, g2 = np.cumsum(GROUP_SPLIT)[:3]
    gla, swa, rwkv = w_in[:, :g0], w_in[:, g0:g1], w_in[:, g1:g2]
    nk = GLA_HEADS * GLA_DK
    a = 2 * nk + MIX_W
    gla_p = jnp.concatenate([gla[:, :a], gla[:, a + 2 * GLA_GATE_RANK:], gla[:, a:a + 2 * GLA_GATE_RANK],
                             jnp.zeros((D_MODEL, LANE - 2 * GLA_GATE_RANK), F32)], axis=1)
    qw = SWA_Q_HEADS * SWA_HEAD_DIM
    kvw = SWA_KV_HEADS * SWA_HEAD_DIM
    q, k, v = swa[:, :qw], swa[:, qw:qw + kvw], swa[:, qw + kvw:]
    w = jnp.concatenate([gla_p, q, _partner_cols(q), k, _partner_cols(k), v, rwkv], axis=1)
    return w.astype(BF16), w_in[:, g2:].astype(BF16)


def _pad_rows(w, top, total):
    return jnp.concatenate([jnp.zeros((top, w.shape[1]), F32), w, jnp.zeros((total - top - w.shape[0], w.shape[1]), F32)], axis=0)


def kernel(x, c, ctx, c_ctx, w_ada, b_ada, g_mix, g_ffn, w_in, w_gk2, b_gk, g_gla, sink, mu_shift, w0, w_w2, a0, w_a2,
           w_g2, k_k, k_a, r_k, gn_w, gn_b, w_branch, w_out, w_ffn1, w_ffn3, w_ffn2, g_final):
    B, T, D = x.shape
    lz = ctx.shape[1]
    depth = w_in.shape[0]
    assert D == D_MODEL and lz % ROW_TILE == 0 and T % ROW_TILE == 0 and T % GRID_W == 0
    nzt, nzc, nzb = lz // ROW_TILE, lz // SCAN_CHUNK, lz // SWA_BLOCK
    s = jnp.concatenate([ctx, x], axis=1)
    cs = _rope_tables(T, lz)
    c8 = jnp.concatenate([c, c_ctx[None], jnp.zeros((8 - (B + 1) % 8 if (B + 1) % 8 else 0, D), F32)], axis=0)
    hidx = np.arange(MIX_W) // RWKV_HEAD
    seg = jnp.asarray(hidx[:, None] == hidx[None, :], BF16)
    out = None
    for l in range(depth):
        last = l == depth - 1
        mod3 = _ada(c8, w_ada[l], b_ada[l]).reshape(c8.shape[0], 6, D)
        w_mix, w_gate = _mixer_weight(w_in[l])
        p_gla, p_q, p_kv, p_rwkv = _inproj(s, mod3, g_mix[l][None], w_mix, nzt)

        wgk = [_pad_rows(w_gk2[l, d], d * GLA_GATE_RANK, LANE).astype(BF16) for d in range(2)]
        o_f = _gla_scan(p_gla, wgk[0], b_gk[l, 0][None], nzc, False)
        ya = _gla_scan(p_gla, wgk[1], b_gk[l, 1][None], nzc, True, o_f, g_gla[l][None])

        yb = _swa(p_q, p_kv, cs, sink[l], nzb)

        zero = jnp.zeros((MIX_W,), F32)
        mu4 = jnp.concatenate([mu_shift[l], jnp.zeros((4 * MIX_W - RWKV_W,), F32)]).reshape(4, MIX_W)
        vec = jnp.stack([w0[l, 0], w0[l, 1], a0[l, 0], a0[l, 1], k_k[l], k_a[l], r_k[l].reshape(-1), zero], axis=0)
        vec = jnp.concatenate([vec, mu4, jnp.zeros((4, MIX_W), F32)], axis=0)
        ww = jnp.stack([_pad_rows(w_w2[l, d], 0, LANE) for d in range(2)]).astype(BF16)
        wa = jnp.stack([_pad_rows(w_a2[l, d], RWKV_DECAY_RANK, LANE) for d in range(2)]).astype(BF16)
        f_r, f_v, f_kk, f_g, f_gb, lw_f, k_f, b_f, lw_b, k_b, b_b = _rwkv_feat(
            p_rwkv, vec, ww, wa, w_g2[l].astype(BF16), seg, nzt)
        y_f = _rwkv_scan(f_r, f_v, f_kk, lw_f, k_f, b_f, nzc, False)
        yc = _rwkv_scan(f_r, f_v, f_kk, lw_b, k_b, b_b, nzc, True, y_f, f_g, f_gb, jnp.stack([gn_w[l], gn_b[l]]))

        off = nzt if last else 0
        s_mix = _merge(s, mod3, g_mix[l][None], ya, yb, yc, w_gate, w_branch[l].astype(BF16), w_out[l].astype(BF16), nzt, off)
        s = _ffn(s_mix, mod3, g_ffn[l][None], w_ffn1[l].astype(BF16), w_ffn3[l].astype(BF16), w_ffn2[l].astype(BF16),
                 g_final[None], nzt, off, last)
        out = s
    return out
```

```python
import functools

import jax
import jax.numpy as jnp
import numpy as np
from jax import lax
from jax.experimental import pallas as pl
from jax.experimental.pallas import tpu as pltpu

F32 = jnp.float32
BF16 = jnp.bfloat16
HI = lax.Precision.HIGHEST

D_MODEL = 1024
GRID_W = 64
NORM_EPS = 1e-6
NEG_INF = -1e30
MIX_W = D_MODEL // 2
N_BRANCH = 3
GLA_HEADS = 4
GLA_DV = MIX_W // GLA_HEADS
GLA_DK = GLA_DV // 2
GLA_GATE_RANK = 16
GLA_GATE_NORM = 16.0
SWA_HEAD_DIM = 64
SWA_Q_HEADS = MIX_W // SWA_HEAD_DIM
SWA_KV_HEADS = SWA_Q_HEADS // 4
SWA_GROUP = SWA_Q_HEADS // SWA_KV_HEADS
SWA_WINDOW = 128
SWA_BLOCK = 128
ROPE_FREQS = SWA_HEAD_DIM // 4
ROPE_BASE = 10000.0
RWKV_HEAD = 64
RWKV_HEADS = MIX_W // RWKV_HEAD
RWKV_DECAY_RANK = 64
RWKV_A_RANK = 64
RWKV_GATE_RANK = 128
RWKV_GN_EPS = 64e-5
FFN_HIDDEN = -(-8 * D_MODEL // (3 * 256)) * 256
GLA_SPLIT = (GLA_HEADS * GLA_DK, GLA_HEADS * GLA_DK, MIX_W, GLA_GATE_RANK, GLA_GATE_RANK, MIX_W)
SWA_SPLIT = (SWA_Q_HEADS * SWA_HEAD_DIM, SWA_KV_HEADS * SWA_HEAD_DIM, SWA_KV_HEADS * SWA_HEAD_DIM)
RWKV_SPLIT = (MIX_W, MIX_W, MIX_W, RWKV_DECAY_RANK, RWKV_A_RANK, RWKV_GATE_RANK)
GROUP_SPLIT = (sum(GLA_SPLIT), sum(SWA_SPLIT), sum(RWKV_SPLIT), N_BRANCH * D_MODEL)

LANE = 128
ROW_TILE = 256
SCAN_CHUNK = 64
SCAN_SUB = 2
PAIR_HEAD = LANE // 2
assert GLA_DK == PAIR_HEAD and RWKV_HEAD == PAIR_HEAD and SCAN_CHUNK == PAIR_HEAD
GLA_W = 2 * GLA_HEADS * GLA_DK + 2 * MIX_W + LANE
SWA_Q_W = 1024
SWA_KV_W = 384
RWKV_W = sum(RWKV_SPLIT)
MIXER_W = GLA_W + SWA_Q_W + SWA_KV_W + RWKV_W
VMEM_LIMIT = 56 * 1024 * 1024


def _mm(a, b):
    return jnp.dot(a.astype(BF16), b.astype(BF16), preferred_element_type=F32)


def _mm_nt(a, b):
    return lax.dot_general(a.astype(BF16), b.astype(BF16), (((1,), (1,)), ((), ())), preferred_element_type=F32)


def _mm_tn(a, b):
    return lax.dot_general(a.astype(BF16), b.astype(BF16), (((0,), (0,)), ((), ())), preferred_element_type=F32)


def _mm_hi(a, b):
    return jnp.dot(a, b, precision=HI, preferred_element_type=F32)


def _sigmoid(x):
    return 1.0 / (1.0 + jnp.exp(-x))


def _silu(x):
    return x * _sigmoid(x)


def _softplus(x):
    return jnp.maximum(x, 0.0) + jnp.log(1.0 + jnp.exp(-jnp.abs(x)))


def _norm_mod(x, g, shift, scale):
    y = x * lax.rsqrt(jnp.mean(x * x, axis=-1, keepdims=True) + NORM_EPS)
    return (y * g) * (1.0 + scale) + shift


def _segsum(x, e):
    hi = x.astype(BF16)
    lo = (x - hi.astype(F32)).astype(BF16)
    return jnp.dot(hi, e, preferred_element_type=F32) + jnp.dot(lo, e, preferred_element_type=F32)


def _params(*sem):
    return pltpu.CompilerParams(dimension_semantics=sem, vmem_limit_bytes=VMEM_LIMIT)


def _full(shape):
    nd = len(shape)
    return pl.BlockSpec(shape, lambda *_: (0,) * nd)


def _ada_kernel(c_ref, w_ref, b_ref, o_ref):
    o_ref[...] = _mm_hi(_silu(c_ref[...]), w_ref[...]) + b_ref[...]


def _ada(c8, w, b):
    n = w.shape[1]
    tn = 1536
    return pl.pallas_call(
        _ada_kernel,
        grid=(n // tn,),
        in_specs=[_full(c8.shape), pl.BlockSpec((D_MODEL, tn), lambda j: (0, j)), pl.BlockSpec((1, tn), lambda j: (0, j))],
        out_specs=pl.BlockSpec((c8.shape[0], tn), lambda j: (0, j)),
        out_shape=jax.ShapeDtypeStruct((c8.shape[0], n), F32),
        compiler_params=_params("arbitrary"),
        name="ada",
    )(c8, w, b.reshape(1, n))


def _inproj_kernel(s_ref, mod_ref, g_ref, w_ref, o_gla, o_q, o_kv, o_rwkv):
    m = mod_ref[...]
    h = _norm_mod(s_ref[...], g_ref[...], m[0:1], m[1:2]).astype(BF16)
    p = jnp.dot(h, w_ref[...], preferred_element_type=F32)
    o0 = 0
    for o_ref in (o_gla, o_q, o_kv, o_rwkv):
        wdt = o_ref.shape[-1]
        o_ref[...] = p[:, o0:o0 + wdt]
        o0 += wdt


def _mod_spec(nzt, nb, off=0):
    return pl.BlockSpec((None, 6, D_MODEL), lambda b, i: (jnp.where(i + off < nzt, nb, b), 0, 0))


def _inproj(s, mod3, g, w, nzt):
    B, L, _ = s.shape
    widths = (GLA_W, SWA_Q_W, SWA_KV_W, RWKV_W)
    return pl.pallas_call(
        _inproj_kernel,
        grid=(B, L // ROW_TILE),
        in_specs=[pl.BlockSpec((None, ROW_TILE, D_MODEL), lambda b, i: (b, i, 0)), _mod_spec(nzt, B),
                  _full((1, D_MODEL)), _full(w.shape)],
        out_specs=[pl.BlockSpec((None, ROW_TILE, wd), lambda b, i: (b, i, 0)) for wd in widths],
        out_shape=[jax.ShapeDtypeStruct((B, L, wd), F32) for wd in widths],
        compiler_params=_params("parallel", "parallel"),
        name="inproj",
    )(s, mod3, g, w)


def _chunk_index(reverse, nzc, nc):
    if not reverse:
        return lambda c: c
    return lambda c: jnp.where(c < nzc, nzc - 1 - c, nzc + nc - 1 - c)


def _order_masks(n, reverse):
    ii = lax.broadcasted_iota(jnp.int32, (n, n), 0)
    jj = lax.broadcasted_iota(jnp.int32, (n, n), 1)
    strict = (jj > ii) if reverse else (jj < ii)
    return strict, strict | (ii == jj), ii == jj


def _block_order_mask(n, chunk, reverse):
    ii = lax.broadcasted_iota(jnp.int32, (n, n), 0)
    jj = lax.broadcasted_iota(jnp.int32, (n, n), 1)
    return ((ii // chunk) == (jj // chunk)) & ((jj >= ii) if reverse else (jj <= ii))


def _mm_01(m01, x):
    hi = x.astype(BF16)
    r1 = x - hi.astype(F32)
    mid = r1.astype(BF16)
    lo = (r1 - mid.astype(F32)).astype(BF16)
    dot = lambda piece: jnp.dot(m01, piece, preferred_element_type=F32)
    return dot(hi) + dot(mid) + dot(lo)


def _chunk_totals(x, chunk):
    parts = [jnp.broadcast_to(jnp.sum(x[j:j + chunk], axis=0, keepdims=True), (chunk, x.shape[1]))
             for j in range(0, x.shape[0], chunk)]
    return parts[0] if len(parts) == 1 else jnp.concatenate(parts, axis=0)


def _pair_order_masks(n, reverse):
    ii = lax.broadcasted_iota(jnp.int32, (n, LANE), 0)
    jj = lax.broadcasted_iota(jnp.int32, (n, LANE), 1) & (PAIR_HEAD - 1)
    strict = (jj > ii) if reverse else (jj < ii)
    return strict, strict | (ii == jj)


def _first_head(width, head_w):
    return (lax.broadcasted_iota(jnp.int32, (1, width), 1) & head_w) == 0


def _block_diag(z, first):
    return jnp.concatenate([jnp.where(first, z, 0.0), jnp.where(first, 0.0, z)], axis=0)


def _diag_blocks(rows, head_w_cols, cols):
    ri = lax.broadcasted_iota(jnp.int32, (rows, cols), 0) // (rows // 2)
    ci = lax.broadcasted_iota(jnp.int32, (rows, cols), 1) // head_w_cols
    return ri == ci


def _row_vector_to_column(row):
    n = row.shape[1]
    eye = lax.broadcasted_iota(jnp.int32, (n, n), 0) == lax.broadcasted_iota(jnp.int32, (n, n), 1)
    return jnp.sum(jnp.where(eye, row, 0.0), axis=1, keepdims=True)


def _gla_kernel(reverse, final, *refs):
    if final:
        p_ref, wgk_ref, bgk_ref, oprev_ref, ggla_ref, o_ref, h_ref = refs
    else:
        p_ref, wgk_ref, bgk_ref, o_ref, h_ref = refs

    @pl.when(pl.program_id(1) == 0)
    def _():
        h_ref[...] = jnp.zeros_like(h_ref)

    NB, R = p_ref.shape[0], p_ref.shape[1]
    C = SCAN_CHUNK
    nsub = R // C
    order = tuple(reversed(range(nsub))) if reverse else tuple(range(nsub))
    nk = GLA_HEADS * GLA_DK
    npair = GLA_HEADS // 2
    kw, vw = 2 * GLA_DK, 2 * GLA_DV
    _, incl2 = _pair_order_masks(C, reverse)
    first_k = _first_head(kw, GLA_DK)
    first_v = _first_head(vw, GLA_DV)
    diag = _diag_blocks(kw, GLA_DV, vw)
    inclb = _block_order_mask(R, C, reverse).astype(BF16)
    q_t, k_t, k_e, e_tot, v, og = {}, {}, {}, {}, {}, {}
    for nb in range(NB):
        p = p_ref[nb]
        q = p[:, 0:nk] * GLA_DK ** -0.5
        k = p[:, nk:2 * nk]
        v[nb] = p[:, 2 * nk:2 * nk + MIX_W]
        og[nb] = p[:, 2 * nk + MIX_W:2 * nk + 2 * MIX_W]
        gk = p[:, 2 * nk + 2 * MIX_W:]
        lg = -_softplus(-(_mm(gk, wgk_ref[...]) + bgk_ref[...])) / GLA_GATE_NORM
        cum = _mm_01(inclb, lg)
        tot = _chunk_totals(lg, C)
        q_t[nb] = q * jnp.exp(cum)
        k_t[nb] = k * jnp.exp(-cum)
        k_e[nb] = k * jnp.exp(tot - cum)
        e_tot[nb] = jnp.exp(tot)
    ksl = lambda pr: slice(pr * kw, (pr + 1) * kw)
    vsl = lambda pr: slice(pr * vw, (pr + 1) * vw)
    rows = lambda j: slice(j * C, (j + 1) * C)
    chains = [(nb, pr) for nb in range(NB) for pr in range(npair)]
    units = [(j, nb, pr) for j in order for (nb, pr) in chains]
    qp = {(j, nb, pr): q_t[nb][rows(j), ksl(pr)] for (j, nb, pr) in units}
    vp = {(j, nb, pr): v[nb][rows(j), vsl(pr)] for (j, nb, pr) in units}
    att = {(j, nb, pr): jnp.where(incl2, _mm_nt(qp[(j, nb, pr)], _block_diag(k_t[nb][rows(j), ksl(pr)], first_k)), 0.0)
           for (j, nb, pr) in units}
    kv = {(j, nb, pr): jnp.where(diag, _mm_tn(k_e[nb][rows(j), ksl(pr)], vp[(j, nb, pr)]), 0.0) for (j, nb, pr) in units}
    dec = {(j, nb, pr): _row_vector_to_column(e_tot[nb][j * C:j * C + 1, ksl(pr)]) for (j, nb, pr) in units}
    lhs = {u: jnp.concatenate([att[u], qp[u]], axis=1) for u in units}
    bdv = {u: _block_diag(vp[u], first_v) for u in units}
    hs = {ch: h_ref[ch[0], ch[1]] for ch in chains}
    outs = {}
    for j in order:
        for ch in chains:
            u = (j,) + ch
            outs[u] = _mm(lhs[u], jnp.concatenate([bdv[u], hs[ch]], axis=0))
        for ch in chains:
            u = (j,) + ch
            hs[ch] = dec[u] * hs[ch] + kv[u]
    for ch in chains:
        h_ref[ch[0], ch[1]] = hs[ch]
    for nb in range(NB):
        o = jnp.concatenate([jnp.concatenate([outs[(j, nb, pr)] for pr in range(npair)], axis=1) for j in range(nsub)], axis=0)
        if final:
            o = o + oprev_ref[nb]
            g = ggla_ref[...]
            ys = []
            for h in range(GLA_HEADS):
                oh = o[:, h * GLA_DV:(h + 1) * GLA_DV]
                ys.append(oh * lax.rsqrt(jnp.mean(oh * oh, axis=-1, keepdims=True) + NORM_EPS) * g)
            o = jnp.concatenate(ys, axis=1) * _silu(og[nb])
        o_ref[nb] = o


def _scan_batch(B):
    return 2 if B % 2 == 0 else 1


def _gla_scan(p, wgk, bgk, nzc, reverse, oprev=None, ggla=None):
    B, L, _ = p.shape
    C = SCAN_CHUNK * SCAN_SUB
    NB = _scan_batch(B)
    nc = L // C
    nzc = nzc // SCAN_SUB
    cidx = _chunk_index(reverse, nzc, nc)
    tok = lambda wd: pl.BlockSpec((NB, C, wd), lambda b, c: (b, cidx(c), 0))
    final = oprev is not None
    ins = [p, wgk, bgk]
    specs = [tok(GLA_W), _full(wgk.shape), _full(bgk.shape)]
    if final:
        ins += [oprev, ggla]
        specs += [tok(MIX_W), _full(ggla.shape)]
    return pl.pallas_call(
        functools.partial(_gla_kernel, reverse, final),
        grid=(B // NB, nc),
        in_specs=specs,
        out_specs=tok(MIX_W),
        out_shape=jax.ShapeDtypeStruct((B, L, MIX_W), F32),
        scratch_shapes=[pltpu.VMEM((NB, GLA_HEADS // 2, 2 * GLA_DK, 2 * GLA_DV), F32)],
        compiler_params=_params("parallel", "arbitrary"),
        name="gla_bwd" if reverse else "gla_fwd",
    )(*ins)


def _swa_kernel(nzb, q_ref, kvm_ref, kv0_ref, kvp_ref, kvz_ref, csq_ref, csm_ref, cs0_ref, csp_ref, sink_ref, o_ref):
    n = pl.program_id(1)
    nblk = pl.num_programs(1)
    BL = SWA_BLOCK
    kvw = SWA_KV_HEADS * SWA_HEAD_DIM
    qw = SWA_Q_HEADS * SWA_HEAD_DIM

    def rope(x, xs, cs, reps):
        cos = jnp.concatenate([cs[:, :LANE]] * reps, axis=1)
        sin = jnp.concatenate([cs[:, LANE:]] * reps, axis=1)
        return x * cos + xs * sin

    qq = q_ref[...]
    q = rope(qq[:, :qw], qq[:, qw:], csq_ref[...], qw // LANE) * SWA_HEAD_DIM ** -0.5
    ks, vs = [], []
    for kv_ref, cs_ref in ((kvm_ref, csm_ref), (kv0_ref, cs0_ref), (kvp_ref, csp_ref)):
        kv = kv_ref[...]
        ks.append(rope(kv[:, :kvw], kv[:, kvw:2 * kvw], cs_ref[...], kvw // LANE))
        vs.append(kv[:, 2 * kvw:])
    kvz = kvz_ref[...]
    ks.append(kvz[:, :kvw])
    vs.append(kvz[:, 2 * kvw:])
    keys = jnp.concatenate(ks, axis=0)
    vals = jnp.concatenate(vs, axis=0)
    nloc = 3 * BL
    nkeys = keys.shape[0]
    qpos = n * BL + lax.broadcasted_iota(jnp.int32, (BL, nkeys), 0)
    col = lax.broadcasted_iota(jnp.int32, (BL, nkeys), 1)
    kpos = (n - 1) * BL + col
    lo = jnp.where(n >= nzb, nzb * BL, nblk * BL)
    band = (jnp.abs(kpos - qpos) <= SWA_WINDOW) & (kpos >= lo) & (kpos < nblk * BL)
    bias = jnp.where(band | (col >= nloc), 0.0, NEG_INF)
    bias = jnp.concatenate([bias] * SWA_GROUP, axis=0)
    rhead = lax.broadcasted_iota(jnp.int32, (SWA_GROUP * BL, 1), 0) // BL
    outs = [None] * SWA_Q_HEADS
    for g in range(SWA_KV_HEADS):
        hd = slice(g * SWA_HEAD_DIM, (g + 1) * SWA_HEAD_DIM)
        qs = jnp.concatenate([q[:, (g * SWA_GROUP + j) * SWA_HEAD_DIM:(g * SWA_GROUP + j + 1) * SWA_HEAD_DIM]
                              for j in range(SWA_GROUP)], axis=0)
        s = _mm_nt(qs, keys[:, hd]) + bias
        sk = jnp.zeros((SWA_GROUP * BL, 1), F32)
        for j in range(SWA_GROUP):
            sk = jnp.where(rhead == j, sink_ref[g * SWA_GROUP + j], sk)
        m = jnp.maximum(jnp.max(s, axis=-1, keepdims=True), sk)
        e = jnp.exp(s - m)
        den = jnp.sum(e, axis=-1, keepdims=True) + jnp.exp(sk - m)
        og = _mm(e, vals[:, hd]) / den
        for j in range(SWA_GROUP):
            outs[g * SWA_GROUP + j] = og[j * BL:(j + 1) * BL]
    o_ref[...] = jnp.concatenate(outs, axis=1)


def _swa(pq, pkv, cs, sink, nzb):
    B, L, _ = pq.shape
    BL = SWA_BLOCK
    nblk = L // BL
    lz = nzb * BL
    prev = lambda n: jnp.maximum(n - 1, 0)
    nxt = lambda n: jnp.minimum(n + 1, nblk - 1)
    kv_spec = lambda f: pl.BlockSpec((None, BL, SWA_KV_W), lambda b, n: (b, f(n), 0))
    cs_spec = lambda f: pl.BlockSpec((BL, 2 * LANE), lambda b, n: (f(n), 0))
    same = lambda n: n
    return pl.pallas_call(
        functools.partial(_swa_kernel, nzb),
        grid=(B, nblk),
        in_specs=[pl.BlockSpec((None, BL, SWA_Q_W), lambda b, n: (b, n, 0)),
                  kv_spec(prev), kv_spec(same), kv_spec(nxt),
                  pl.BlockSpec((None, lz, SWA_KV_W), lambda b, n: (b, 0, 0)),
                  cs_spec(same), cs_spec(prev), cs_spec(same), cs_spec(nxt),
                  pl.BlockSpec(memory_space=pltpu.SMEM)],
        out_specs=pl.BlockSpec((None, BL, MIX_W), lambda b, n: (b, n, 0)),
        out_shape=jax.ShapeDtypeStruct((B, L, MIX_W), F32),
        compiler_params=_params("parallel", "parallel"),
        name="swa",
    )(pq, pkv, pkv, pkv, pkv, cs, cs, cs, cs, sink)


def _rwkv_feat_kernel(nzt, p_ref, pp_ref, pn_ref, vec_ref, ww_ref, wa_ref, wg_ref, e_ref,
                      o_r, o_v, o_kk, o_g, o_gb, o_lwf, o_kf, o_bf, o_lwb, o_kb, o_bb):
    i = pl.program_id(1)
    n = pl.num_programs(1)
    p = p_ref[...]
    TM = p.shape[0]
    first = (i == 0) | (i == nzt)
    last = (i == nzt - 1) | (i == n - 1)
    prow = jnp.where(first, 0.0, pp_ref[7:8, :])
    nrow = jnp.where(last, 0.0, pn_ref[0:1, :])
    rid = lax.broadcasted_iota(jnp.int32, (TM, 1), 0)
    up = jnp.where(rid == 0, prow, pltpu.roll(p, 1, 0))
    dn = jnp.where(rid == TM - 1, nrow, pltpu.roll(p, TM - 1, 0))
    vec = vec_ref[...]
    mu = jnp.concatenate([vec[8 + j:9 + j] for j in range(4)], axis=1)[:, :RWKV_W]
    pm = p + mu * (0.5 * (up + dn) - p)
    W = MIX_W
    r, k, v = pm[:, :W], pm[:, W:2 * W], pm[:, 2 * W:3 * W]
    wa = pm[:, 3 * W:3 * W + LANE]
    gl = pm[:, 3 * W + LANE:]
    e = e_ref[...]
    k_k, k_a, r_k = vec[4:5], vec[5:6], vec[6:7]
    g = _mm(_sigmoid(gl), wg_ref[...])
    kkn = k * k_k
    kk = kkn / jnp.maximum(jnp.sqrt(_segsum(kkn * kkn, e)), 1e-12)
    twa = jnp.tanh(wa)
    ksum = None
    for d, (o_lw, o_k, o_b) in enumerate(((o_lwf, o_kf, o_bf), (o_lwb, o_kb, o_bb))):
        w_raw = -_softplus(-(vec[d:d + 1] + _mm(twa, ww_ref[d]))) - 0.5
        a = _sigmoid(vec[2 + d:3 + d] + _mm(wa, wa_ref[d]))
        kd = k * (1.0 + (a - 1.0) * k_a)
        o_lw[...] = -jnp.exp(w_raw)
        o_k[...] = kd
        o_b[...] = kk * a
        ksum = kd if ksum is None else ksum + kd
    o_r[...] = r
    o_v[...] = v
    o_kk[...] = kk
    o_g[...] = g
    o_gb[...] = _segsum(r * ksum * r_k, e) * v * g


def _rwkv_feat(p, vec, ww, wa, wg, e, nzt):
    B, L, _ = p.shape
    TM = ROW_TILE
    r8 = TM // 8
    tok = pl.BlockSpec((None, TM, RWKV_W), lambda b, i: (b, i, 0))
    halo_p = pl.BlockSpec((None, 8, RWKV_W), lambda b, i: (b, jnp.maximum(i * r8 - 1, 0), 0))
    halo_n = pl.BlockSpec((None, 8, RWKV_W), lambda b, i: (b, jnp.minimum((i + 1) * r8, L // 8 - 1), 0))
    out = pl.BlockSpec((None, TM, MIX_W), lambda b, i: (b, i, 0))
    return pl.pallas_call(
        functools.partial(_rwkv_feat_kernel, nzt),
        grid=(B, L // TM),
        in_specs=[tok, halo_p, halo_n, _full(vec.shape), _full(ww.shape), _full(wa.shape), _full(wg.shape), _full(e.shape)],
        out_specs=[out] * 11,
        out_shape=[jax.ShapeDtypeStruct((B, L, MIX_W), F32)] * 11,
        compiler_params=_params("parallel", "parallel"),
        name="rwkv_feat",
    )(p, p, p, vec, ww, wa, wg, e)


def _rwkv_scan_kernel(reverse, final, *refs):
    if final:
        r_ref, v_ref, kk_ref, lw_ref, k_ref, b_ref, yprev_ref, g_ref, gb_ref, gn_ref, o_ref, h_ref = refs
    else:
        r_ref, v_ref, kk_ref, lw_ref, k_ref, b_ref, o_ref, h_ref = refs

    @pl.when(pl.program_id(1) == 0)
    def _():
        h_ref[...] = jnp.zeros_like(h_ref)

    NB, R = lw_ref.shape[0], lw_ref.shape[1]
    C = SCAN_CHUNK
    nsub = R // C
    order = tuple(reversed(range(nsub))) if reverse else tuple(range(nsub))
    PW = LANE
    npair = RWKV_HEADS // 2
    strict2, incl2 = _pair_order_masks(C, reverse)
    first = _first_head(PW, RWKV_HEAD)
    first2 = _first_head(2 * PW, RWKV_HEAD)
    first3 = _first_head(3 * PW, RWKV_HEAD)
    diag = _diag_blocks(PW, RWKV_HEAD, PW)
    inclb = _block_order_mask(R, C, reverse).astype(BF16)
    kap_t, r_t, k_t, b_t, k_e, b_e, e_tot, v_all = {}, {}, {}, {}, {}, {}, {}, {}
    for nb in range(NB):
        lw = lw_ref[nb]
        cum = _mm_01(inclb, lw)
        tot = _chunk_totals(lw, C)
        e_inv = jnp.exp(-cum)
        e_end = jnp.exp(tot - cum)
        e_tot[nb] = jnp.exp(tot)
        kap_t[nb] = kk_ref[nb] * jnp.exp(cum - lw)
        r_t[nb] = r_ref[nb] * jnp.exp(cum)
        k_all, b_all = k_ref[nb], b_ref[nb]
        k_t[nb], b_t[nb] = k_all * e_inv, b_all * e_inv
        k_e[nb], b_e[nb] = k_all * e_end, b_all * e_end
        v_all[nb] = v_ref[nb]
    sl = lambda pr: slice(pr * PW, (pr + 1) * PW)
    rows = lambda j: slice(j * C, (j + 1) * C)
    pick = lambda d, u: d[u[1]][rows(u[0]), sl(u[2])]
    chains = [(nb, pr) for nb in range(NB) for pr in range(npair)]
    units = [(j,) + ch for j in order for ch in chains]
    vp = {u: pick(v_all, u) for u in units}
    G = {u: _mm_nt(jnp.concatenate([pick(kap_t, u), pick(r_t, u)], axis=0),
                   jnp.concatenate([_block_diag(pick(b_t, u), first), _block_diag(pick(k_t, u), first)], axis=0))
         for u in units}
    a_ab = {u: jnp.where(strict2, G[u][:C, :PW], 0.0) for u in units}
    a_ak = {u: jnp.where(strict2, G[u][:C, PW:], 0.0) for u in units}
    a_rb = {u: jnp.where(incl2, G[u][C:, :PW], 0.0) for u in units}
    a_rk = {u: jnp.where(incl2, G[u][C:, PW:], 0.0) for u in units}
    akv = {u: _mm(a_ak[u], _block_diag(vp[u], first)) for u in units}
    xp = {u: jnp.concatenate([pick(kap_t, u), akv[u], -a_ab[u]], axis=1) for u in units}
    prod = {u: _mm(a_ab[u], _block_diag(xp[u], first3)) for u in units}
    x = {u: xp[u][:, :2 * PW] - prod[u][:, :2 * PW] for u in units}
    pw = {u: -prod[u][:, 2 * PW:] for u in units}
    span = 2
    while span < C:
        span *= 2
        if span < C:
            xp = {u: jnp.concatenate([x[u], pw[u]], axis=1) for u in units}
            prod = {u: _mm(pw[u], _block_diag(xp[u], first3)) for u in units}
            x = {u: x[u] + prod[u][:, :2 * PW] for u in units}
            pw = {u: prod[u][:, 2 * PW:] for u in units}
        else:
            x = {u: x[u] + _mm(pw[u], _block_diag(x[u], first2)) for u in units}
    w_p = {u: x[u][:, :PW] for u in units}
    u_p = {u: x[u][:, PW:] for u in units}
    vu = {u: jnp.concatenate([vp[u], u_p[u]], axis=0) for u in units}
    bw = {u: jnp.where(diag, _mm_tn(pick(b_e, u), w_p[u]), 0.0) for u in units}
    nn = {u: jnp.where(diag, _mm_tn(jnp.concatenate([pick(k_e, u), -pick(b_e, u)], axis=0), vu[u]), 0.0) for u in units}
    rw = {u: pick(r_t, u) - _mm(a_rb[u], _block_diag(w_p[u], first)) for u in units}
    ylhs = {u: jnp.concatenate([a_rk[u], -a_rb[u], rw[u]], axis=1) for u in units}
    yrhs = {u: jnp.concatenate([_block_diag(vp[u], first), _block_diag(u_p[u], first)], axis=0) for u in units}
    dec = {u: _row_vector_to_column(e_tot[u[1]][u[0] * C:u[0] * C + 1, sl(u[2])]) for u in units}
    hs = {ch: h_ref[ch[0], ch[1]] for ch in chains}
    ys = {}
    for j in order:
        for ch in chains:
            u = (j,) + ch
            ys[u] = _mm(ylhs[u], jnp.concatenate([yrhs[u], hs[ch]], axis=0))
        for ch in chains:
            u = (j,) + ch
            hs[ch] = dec[u] * hs[ch] - _mm(bw[u], hs[ch]) + nn[u]
    for ch in chains:
        h_ref[ch[0], ch[1]] = hs[ch]
    for nb in range(NB):
        if final:
            yprev = yprev_ref[nb]
            outs = []
            for pr in range(npair):
                y = jnp.concatenate([ys[(j, nb, pr)] for j in range(nsub)], axis=0) + yprev[:, sl(pr)]
                s_a = jnp.sum(jnp.where(first, y, 0.0), axis=-1, keepdims=True)
                s_b = jnp.sum(jnp.where(first, 0.0, y), axis=-1, keepdims=True)
                yc = y - jnp.where(first, s_a, s_b) * (1.0 / RWKV_HEAD)
                q = yc * yc
                v_a = jnp.sum(jnp.where(first, q, 0.0), axis=-1, keepdims=True)
                v_b = jnp.sum(jnp.where(first, 0.0, q), axis=-1, keepdims=True)
                outs.append(yc * lax.rsqrt(jnp.where(first, v_a, v_b) * (1.0 / RWKV_HEAD) + RWKV_GN_EPS))
            gn = gn_ref[...]
            o_ref[nb] = (jnp.concatenate(outs, axis=1) * gn[0:1] + gn[1:2]) * g_ref[nb] + gb_ref[nb]
        else:
            o_ref[nb] = jnp.concatenate(
                [jnp.concatenate([ys[(j, nb, pr)] for pr in range(npair)], axis=1) for j in range(nsub)], axis=0)


def _rwkv_scan(r, v, kk, lw, k, b, nzc, reverse, yprev=None, g=None, gb=None, gn=None):
    B, L, _ = r.shape
    C = SCAN_CHUNK * SCAN_SUB
    NB = _scan_batch(B)
    nc = L // C
    nzc = nzc // SCAN_SUB
    cidx = _chunk_index(reverse, nzc, nc)
    tok = pl.BlockSpec((NB, C, MIX_W), lambda b_, c: (b_, cidx(c), 0))
    final = yprev is not None
    ins = [r, v, kk, lw, k, b]
    specs = [tok] * 6
    if final:
        ins += [yprev, g, gb, gn]
        specs += [tok] * 3 + [_full(gn.shape)]
    return pl.pallas_call(
        functools.partial(_rwkv_scan_kernel, reverse, final),
        grid=(B // NB, nc),
        in_specs=specs,
        out_specs=tok,
        out_shape=jax.ShapeDtypeStruct((B, L, MIX_W), F32),
        scratch_shapes=[pltpu.VMEM((NB, RWKV_HEADS // 2, LANE, LANE), F32)],
        compiler_params=_params("parallel", "arbitrary"),
        name="rwkv_bwd" if reverse else "rwkv_fwd",
    )(*ins)


def _merge_kernel(s_ref, mod_ref, g_ref, ya_ref, yb_ref, yc_ref, wg_ref, wb_ref, wo_ref, o_ref):
    m = mod_ref[...]
    s = s_ref[...]
    h = _norm_mod(s, g_ref[...], m[0:1], m[1:2]).astype(BF16)
    gates = jnp.dot(h, wg_ref[...], preferred_element_type=F32)
    acc = None
    for i, y_ref in enumerate((ya_ref, yb_ref, yc_ref)):
        t = _sigmoid(gates[:, i * D_MODEL:(i + 1) * D_MODEL]) * _mm(y_ref[...], wb_ref[i])
        acc = t if acc is None else acc + t
    o_ref[...] = s + m[2:3] * _mm(acc, wo_ref[...])


def _merge(s, mod3, g, ya, yb, yc, wg, wb, wo, nzt, off):
    B, L, _ = s.shape
    TM = ROW_TILE
    nt = L // TM - off
    tok_in = lambda wd: pl.BlockSpec((None, TM, wd), lambda b, i: (b, i + off, 0))
    return pl.pallas_call(
        _merge_kernel,
        grid=(B, nt),
        in_specs=[tok_in(D_MODEL), _mod_spec(nzt, B, off), _full((1, D_MODEL)), tok_in(MIX_W), tok_in(MIX_W), tok_in(MIX_W),
                  _full(wg.shape), _full(wb.shape), _full(wo.shape)],
        out_specs=pl.BlockSpec((None, TM, D_MODEL), lambda b, i: (b, i, 0)),
        out_shape=jax.ShapeDtypeStruct((B, nt * TM, D_MODEL), F32),
        compiler_params=_params("parallel", "parallel"),
        name="merge",
    )(s, mod3, g, ya, yb, yc, wg, wb, wo)


def _ffn_kernel(final, s_ref, mod_ref, g_ref, w1_ref, w3_ref, w2_ref, gf_ref, o_ref):
    m = mod_ref[...]
    s = s_ref[...]
    h = _norm_mod(s, g_ref[...], m[3:4], m[4:5]).astype(BF16)
    a = jnp.dot(h, w1_ref[...], preferred_element_type=F32)
    b = jnp.dot(h, w3_ref[...], preferred_element_type=F32)
    o = s + m[5:6] * _mm(_silu(a) * b, w2_ref[...])
    if final:
        o = o * lax.rsqrt(jnp.mean(o * o, axis=-1, keepdims=True) + NORM_EPS) * gf_ref[...]
    o_ref[...] = o


def _ffn(s, mod3, g, w1, w3, w2, gf, nzt, off, final):
    B, L, _ = s.shape
    TM = ROW_TILE
    nt = L // TM
    const = lambda shape: pl.BlockSpec(shape, lambda b, i: (0,) * len(shape), pipeline_mode=pl.Buffered(1))
    return pl.pallas_call(
        functools.partial(_ffn_kernel, final),
        grid=(B, nt),
        in_specs=[pl.BlockSpec((None, TM, D_MODEL), lambda b, i: (b, i, 0)), _mod_spec(nzt, B, off), _full((1, D_MODEL)),
                  const(w1.shape), const(w3.shape), const(w2.shape), _full((1, D_MODEL))],
        out_specs=pl.BlockSpec((None, TM, D_MODEL), lambda b, i: (b, i, 0)),
        out_shape=jax.ShapeDtypeStruct((B, L, D_MODEL), F32),
        compiler_params=_params("parallel", "parallel"),
        name="ffn",
    )(s, mod3, g, w1, w3, w2, gf)


def _rope_tables(T, lz):
    rows = T // GRID_W
    row = jnp.repeat(jnp.arange(rows), GRID_W).astype(F32)
    col = jnp.tile(jnp.arange(GRID_W), rows).astype(F32)
    inv = ROPE_BASE ** (-jnp.arange(ROPE_FREQS, dtype=F32) / ROPE_FREQS)
    ar, ac = row[:, None] * inv, col[:, None] * inv
    cos = jnp.concatenate([jnp.cos(ar), jnp.cos(ar), jnp.cos(ac), jnp.cos(ac)], axis=1)
    sin = jnp.concatenate([-jnp.sin(ar), jnp.sin(ar), -jnp.sin(ac), jnp.sin(ac)], axis=1)
    reps = LANE // SWA_HEAD_DIM
    x_tab = jnp.concatenate([jnp.tile(cos, (1, reps)), jnp.tile(sin, (1, reps))], axis=1)
    z_tab = jnp.concatenate([jnp.ones((lz, LANE), F32), jnp.zeros((lz, LANE), F32)], axis=1)
    return jnp.concatenate([z_tab, x_tab], axis=0)


def _partner_cols(w):
    return w.reshape(w.shape[0], -1, 2, ROPE_FREQS)[:, :, ::-1, :].reshape(w.shape)


def _mixer_weight(w_in):
    g0, g1, g2 = np.cumsum(GROUP_SPLIT)[:3]
    gla, swa, rwkv = w_in[:, :g0], w_in[:, g0:g1], w_in[:, g1:g2]
    nk = GLA_HEADS * GLA_DK
    a = 2 * nk + MIX_W
    gla_p = jnp.concatenate([gla[:, :a], gla[:, a + 2 * GLA_GATE_RANK:], gla[:, a:a + 2 * GLA_GATE_RANK],
                             jnp.zeros((D_MODEL, LANE - 2 * GLA_GATE_RANK), F32)], axis=1)
    qw = SWA_Q_HEADS * SWA_HEAD_DIM
    kvw = SWA_KV_HEADS * SWA_HEAD_DIM
    q, k, v = swa[:, :qw], swa[:, qw:qw + kvw], swa[:, qw + kvw:]
    w = jnp.concatenate([gla_p, q, _partner_cols(q), k, _partner_cols(k), v, rwkv], axis=1)
    return w.astype(BF16), w_in[:, g2:].astype(BF16)


def _pad_rows(w, top, total):
    return jnp.concatenate([jnp.zeros((top, w.shape[1]), F32), w, jnp.zeros((total - top - w.shape[0], w.shape[1]), F32)], axis=0)


def kernel(x, c, ctx, c_ctx, w_ada, b_ada, g_mix, g_ffn, w_in, w_gk2, b_gk, g_gla, sink, mu_shift, w0, w_w2, a0, w_a2,
           w_g2, k_k, k_a, r_k, gn_w, gn_b, w_branch, w_out, w_ffn1, w_ffn3, w_ffn2, g_final):
    B, T, D = x.shape
    lz = ctx.shape[1]
    depth = w_in.shape[0]
    assert D == D_MODEL and lz % ROW_TILE == 0 and T % ROW_TILE == 0 and T % GRID_W == 0
    nzt, nzc, nzb = lz // ROW_TILE, lz // SCAN_CHUNK, lz // SWA_BLOCK
    s = jnp.concatenate([ctx, x], axis=1)
    cs = _rope_tables(T, lz)
    c8 = jnp.concatenate([c, c_ctx[None], jnp.zeros((8 - (B + 1) % 8 if (B + 1) % 8 else 0, D), F32)], axis=0)
    hidx = np.arange(MIX_W) // RWKV_HEAD
    seg = jnp.asarray(hidx[:, None] == hidx[None, :], BF16)
    out = None
    for l in range(depth):
        last = l == depth - 1
        mod3 = _ada(c8, w_ada[l], b_ada[l]).reshape(c8.shape[0], 6, D)
        w_mix, w_gate = _mixer_weight(w_in[l])
        p_gla, p_q, p_kv, p_rwkv = _inproj(s, mod3, g_mix[l][None], w_mix, nzt)

        wgk = [_pad_rows(w_gk2[l, d], d * GLA_GATE_RANK, LANE).astype(BF16) for d in range(2)]
        o_f = _gla_scan(p_gla, wgk[0], b_gk[l, 0][None], nzc, False)
        ya = _gla_scan(p_gla, wgk[1], b_gk[l, 1][None], nzc, True, o_f, g_gla[l][None])

        yb = _swa(p_q, p_kv, cs, sink[l], nzb)

        zero = jnp.zeros((MIX_W,), F32)
        mu4 = jnp.concatenate([mu_shift[l], jnp.zeros((4 * MIX_W - RWKV_W,), F32)]).reshape(4, MIX_W)
        vec = jnp.stack([w0[l, 0], w0[l, 1], a0[l, 0], a0[l, 1], k_k[l], k_a[l], r_k[l].reshape(-1), zero], axis=0)
        vec = jnp.concatenate([vec, mu4, jnp.zeros((4, MIX_W), F32)], axis=0)
        ww = jnp.stack([_pad_rows(w_w2[l, d], 0, LANE) for d in range(2)]).astype(BF16)
        wa = jnp.stack([_pad_rows(w_a2[l, d], RWKV_DECAY_RANK, LANE) for d in range(2)]).astype(BF16)
        f_r, f_v, f_kk, f_g, f_gb, lw_f, k_f, b_f, lw_b, k_b, b_b = _rwkv_feat(
            p_rwkv, vec, ww, wa, w_g2[l].astype(BF16), seg, nzt)
        y_f = _rwkv_scan(f_r, f_v, f_kk, lw_f, k_f, b_f, nzc, False)
        yc = _rwkv_scan(f_r, f_v, f_kk, lw_b, k_b, b_b, nzc, True, y_f, f_g, f_gb, jnp.stack([gn_w[l], gn_b[l]]))

        off = nzt if last else 0
        s_mix = _merge(s, mod3, g_mix[l][None], ya, yb, yc, w_gate, w_branch[l].astype(BF16), w_out[l].astype(BF16), nzt, off)
        s = _ffn(s_mix, mod3, g_ffn[l][None], w_ffn1[l].astype(BF16), w_ffn3[l].astype(BF16), w_ffn2[l].astype(BF16),
                 g_final[None], nzt, off, last)
        out = s
    return out
```

```python
import functools

import jax
import jax.numpy as jnp
import numpy as np
from jax import lax
from jax.experimental import pallas as pl
from jax.experimental.pallas import tpu as pltpu

F32 = jnp.float32
BF16 = jnp.bfloat16
HI = lax.Precision.HIGHEST
LOG2E = 1.4426950408889634
DECAY_SCALE = 0.6065306597126334

D_MODEL = 1024
GRID_W = 64
NORM_EPS = 1e-6
NEG_INF = -1e30
MIX_W = D_MODEL // 2
N_BRANCH = 3
GLA_HEADS = 4
GLA_DV = MIX_W // GLA_HEADS
GLA_DK = GLA_DV // 2
GLA_GATE_RANK = 16
GLA_GATE_NORM = 16.0
SWA_HEAD_DIM = 64
SWA_Q_HEADS = MIX_W // SWA_HEAD_DIM
SWA_KV_HEADS = SWA_Q_HEADS // 4
SWA_GROUP = SWA_Q_HEADS // SWA_KV_HEADS
SWA_WINDOW = 128
SWA_BLOCK = 128
ROPE_FREQS = SWA_HEAD_DIM // 4
ROPE_BASE = 10000.0
RWKV_HEAD = 64
RWKV_HEADS = MIX_W // RWKV_HEAD
RWKV_DECAY_RANK = 64
RWKV_A_RANK = 64
RWKV_GATE_RANK = 128
RWKV_GN_EPS = 64e-5
FFN_HIDDEN = -(-8 * D_MODEL // (3 * 256)) * 256
GLA_SPLIT = (GLA_HEADS * GLA_DK, GLA_HEADS * GLA_DK, MIX_W, GLA_GATE_RANK, GLA_GATE_RANK, MIX_W)
SWA_SPLIT = (SWA_Q_HEADS * SWA_HEAD_DIM, SWA_KV_HEADS * SWA_HEAD_DIM, SWA_KV_HEADS * SWA_HEAD_DIM)
RWKV_SPLIT = (MIX_W, MIX_W, MIX_W, RWKV_DECAY_RANK, RWKV_A_RANK, RWKV_GATE_RANK)
GROUP_SPLIT = (sum(GLA_SPLIT), sum(SWA_SPLIT), sum(RWKV_SPLIT), N_BRANCH * D_MODEL)

LANE = 128
ROW_TILE = 256
HALO = 16
SCAN_CHUNK = 64
SCAN_SUB = 2
PAIR_HEAD = LANE // 2
assert GLA_DK == PAIR_HEAD and RWKV_HEAD == PAIR_HEAD and SCAN_CHUNK == PAIR_HEAD
GLA_W = 2 * GLA_HEADS * GLA_DK + 2 * MIX_W + LANE
SWA_Q_W = 1024
SWA_KV_W = 384
RWKV_W = sum(RWKV_SPLIT)
MIXER_W = GLA_W + SWA_Q_W + SWA_KV_W + RWKV_W
VMEM_LIMIT = 56 * 1024 * 1024


def _mm(a, b):
    return jnp.dot(a.astype(BF16), b.astype(BF16), preferred_element_type=F32)


def _mm_nt(a, b):
    return lax.dot_general(a.astype(BF16), b.astype(BF16), (((1,), (1,)), ((), ())), preferred_element_type=F32)


def _mm_tn(a, b):
    return lax.dot_general(a.astype(BF16), b.astype(BF16), (((0,), (0,)), ((), ())), preferred_element_type=F32)


def _mm_hi(a, b):
    return jnp.dot(a, b, precision=HI, preferred_element_type=F32)


def _sigmoid(x):
    return 1.0 / (1.0 + jnp.exp(-x))


def _silu(x):
    return x * _sigmoid(x)


def _softplus(x):
    return jnp.maximum(x, 0.0) + jnp.log(1.0 + jnp.exp(-jnp.abs(x)))


def _norm_mod(x, g, shift, scale):
    y = x * lax.rsqrt(jnp.mean(x * x, axis=-1, keepdims=True) + NORM_EPS)
    return (y * g) * (1.0 + scale) + shift


def _segsum(x, e):
    hi = x.astype(BF16)
    lo = (x - hi.astype(F32)).astype(BF16)
    return jnp.dot(hi, e, preferred_element_type=F32) + jnp.dot(lo, e, preferred_element_type=F32)


def _params(*sem):
    return pltpu.CompilerParams(dimension_semantics=sem, vmem_limit_bytes=VMEM_LIMIT)


def _full(shape):
    nd = len(shape)
    return pl.BlockSpec(shape, lambda *_: (0,) * nd)


def _ada_kernel(c_ref, w_ref, b_ref, o_ref):
    o_ref[...] = _mm_hi(_silu(c_ref[...]), w_ref[...]) + b_ref[...]


def _ada(c8, w, b):
    n = w.shape[1]
    tn = 1536
    return pl.pallas_call(
        _ada_kernel,
        grid=(n // tn,),
        in_specs=[_full(c8.shape), pl.BlockSpec((D_MODEL, tn), lambda j: (0, j)), pl.BlockSpec((1, tn), lambda j: (0, j))],
        out_specs=pl.BlockSpec((c8.shape[0], tn), lambda j: (0, j)),
        out_shape=jax.ShapeDtypeStruct((c8.shape[0], n), F32),
        compiler_params=_params("arbitrary"),
        name="ada",
    )(c8, w, b.reshape(1, n))


def _inproj_kernel(s_ref, mod_ref, g_ref, w_ref, o_gla, o_q, o_kv, o_rwkv):
    m = mod_ref[...]
    h = _norm_mod(s_ref[...], g_ref[...], m[0:1], m[1:2]).astype(BF16)
    p = jnp.dot(h, w_ref[...], preferred_element_type=F32)
    o0 = 0
    for o_ref in (o_gla, o_q, o_kv, o_rwkv):
        wdt = o_ref.shape[-1]
        o_ref[...] = p[:, o0:o0 + wdt].astype(o_ref.dtype)
        o0 += wdt


def _mod_spec(nzt, nb, off=0):
    return pl.BlockSpec((None, 6, D_MODEL), lambda b, i: (jnp.where(i + off < nzt, nb, b), 0, 0))


def _inproj(s, mod3, g, w, nzt):
    B, L, _ = s.shape
    widths = (GLA_W, SWA_Q_W, SWA_KV_W, RWKV_W)
    return pl.pallas_call(
        _inproj_kernel,
        grid=(B, L // ROW_TILE),
        in_specs=[pl.BlockSpec((None, ROW_TILE, D_MODEL), lambda b, i: (b, i, 0)), _mod_spec(nzt, B),
                  _full((1, D_MODEL)), _full(w.shape)],
        out_specs=[pl.BlockSpec((None, ROW_TILE, wd), lambda b, i: (b, i, 0)) for wd in widths],
        out_shape=[jax.ShapeDtypeStruct((B, L, wd), dt) for wd, dt in zip(widths, (F32, F32, F32, BF16))],
        compiler_params=_params("parallel", "parallel"),
        name="inproj",
    )(s, mod3, g, w)


def _chunk_index(reverse, nzc, nc):
    if not reverse:
        return lambda c: c
    return lambda c: jnp.where(c < nzc, nzc - 1 - c, nzc + nc - 1 - c)


def _order_masks(n, reverse):
    ii = lax.broadcasted_iota(jnp.int32, (n, n), 0)
    jj = lax.broadcasted_iota(jnp.int32, (n, n), 1)
    strict = (jj > ii) if reverse else (jj < ii)
    return strict, strict | (ii == jj), ii == jj


def _block_order_mask(n, chunk, reverse):
    ii = lax.broadcasted_iota(jnp.int32, (n, n), 0)
    jj = lax.broadcasted_iota(jnp.int32, (n, n), 1)
    return ((ii // chunk) == (jj // chunk)) & ((jj >= ii) if reverse else (jj <= ii))


def _mm_01(m01, x):
    hi = x.astype(BF16)
    r1 = x - hi.astype(F32)
    mid = r1.astype(BF16)
    lo = (r1 - mid.astype(F32)).astype(BF16)
    dot = lambda piece: jnp.dot(m01, piece, preferred_element_type=F32)
    return dot(hi) + dot(mid) + dot(lo)


def _chunk_totals(x, chunk):
    parts = [jnp.broadcast_to(jnp.sum(x[j:j + chunk], axis=0, keepdims=True), (chunk, x.shape[1]))
             for j in range(0, x.shape[0], chunk)]
    return parts[0] if len(parts) == 1 else jnp.concatenate(parts, axis=0)


def _pair_order_masks(n, reverse):
    ii = lax.broadcasted_iota(jnp.int32, (n, LANE), 0)
    jj = lax.broadcasted_iota(jnp.int32, (n, LANE), 1) & (PAIR_HEAD - 1)
    strict = (jj > ii) if reverse else (jj < ii)
    return strict, strict | (ii == jj), ii == jj


def _first_head(width, head_w):
    return (lax.broadcasted_iota(jnp.int32, (1, width), 1) & head_w) == 0


def _block_diag(z, first):
    return jnp.concatenate([jnp.where(first, z, 0.0), jnp.where(first, 0.0, z)], axis=0)


def _diag_blocks(rows, head_w_cols, cols):
    ri = lax.broadcasted_iota(jnp.int32, (rows, cols), 0) // (rows // 2)
    ci = lax.broadcasted_iota(jnp.int32, (rows, cols), 1) // head_w_cols
    return ri == ci


def _row_vector_to_column(row):
    n = row.shape[1]
    eye = lax.broadcasted_iota(jnp.int32, (n, n), 0) == lax.broadcasted_iota(jnp.int32, (n, n), 1)
    return jnp.sum(jnp.where(eye, row, 0.0), axis=1, keepdims=True)


def _gla_kernel(reverse, final, *refs):
    if final:
        p_ref, wgk_ref, bgk_ref, oprev_ref, ggla_ref, o_ref, h_ref = refs
    else:
        p_ref, wgk_ref, bgk_ref, o_ref, h_ref = refs

    @pl.when(pl.program_id(1) == 0)
    def _():
        h_ref[...] = jnp.zeros_like(h_ref)

    NB, R = p_ref.shape[0], p_ref.shape[1]
    C = SCAN_CHUNK
    nsub = R // C
    order = tuple(reversed(range(nsub))) if reverse else tuple(range(nsub))
    nk = GLA_HEADS * GLA_DK
    npair = GLA_HEADS // 2
    kw, vw = 2 * GLA_DK, 2 * GLA_DV
    incl2 = _pair_order_masks(C, reverse)[1]
    first_k = _first_head(kw, GLA_DK)
    first_v = _first_head(vw, GLA_DV)
    diag = _diag_blocks(kw, GLA_DV, vw)
    inclb = _block_order_mask(R, C, reverse).astype(BF16)
    q_t, k_t, k_e, e_tot, v, og = {}, {}, {}, {}, {}, {}
    for nb in range(NB):
        p = p_ref[nb]
        q = p[:, 0:nk] * GLA_DK ** -0.5
        k = p[:, nk:2 * nk]
        v[nb] = p[:, 2 * nk:2 * nk + MIX_W]
        og[nb] = p[:, 2 * nk + MIX_W:2 * nk + 2 * MIX_W]
        gk = p[:, 2 * nk + 2 * MIX_W:]
        lg = -_softplus(-(_mm(gk, wgk_ref[...]) + bgk_ref[...])) / GLA_GATE_NORM
        cum = _mm_01(inclb, lg)
        tot = _chunk_totals(lg, C)
        q_t[nb] = q * jnp.exp(cum)
        k_t[nb] = k * jnp.exp(-cum)
        k_e[nb] = k * jnp.exp(tot - cum)
        e_tot[nb] = jnp.exp(tot)
    ksl = lambda pr: slice(pr * kw, (pr + 1) * kw)
    vsl = lambda pr: slice(pr * vw, (pr + 1) * vw)
    rows = lambda j: slice(j * C, (j + 1) * C)
    chains = [(nb, pr) for nb in range(NB) for pr in range(npair)]
    units = [(j, nb, pr) for j in order for (nb, pr) in chains]
    qp = {(j, nb, pr): q_t[nb][rows(j), ksl(pr)] for (j, nb, pr) in units}
    vp = {(j, nb, pr): v[nb][rows(j), vsl(pr)] for (j, nb, pr) in units}
    att = {(j, nb, pr): jnp.where(incl2, _mm_nt(qp[(j, nb, pr)], _block_diag(k_t[nb][rows(j), ksl(pr)], first_k)), 0.0)
           for (j, nb, pr) in units}
    kv = {(j, nb, pr): jnp.where(diag, _mm_tn(k_e[nb][rows(j), ksl(pr)], vp[(j, nb, pr)]), 0.0) for (j, nb, pr) in units}
    dec = {(j, nb, pr): _row_vector_to_column(e_tot[nb][j * C:j * C + 1, ksl(pr)]) for (j, nb, pr) in units}
    lhs = {u: jnp.concatenate([att[u], qp[u]], axis=1) for u in units}
    bdv = {u: _block_diag(vp[u], first_v) for u in units}
    hs = {ch: h_ref[ch[0], ch[1]] for ch in chains}
    outs = {}
    for j in order:
        for ch in chains:
            u = (j,) + ch
            outs[u] = _mm(lhs[u], jnp.concatenate([bdv[u], hs[ch]], axis=0))
        for ch in chains:
            u = (j,) + ch
            hs[ch] = dec[u] * hs[ch] + kv[u]
    for ch in chains:
        h_ref[ch[0], ch[1]] = hs[ch]
    for nb in range(NB):
        o = jnp.concatenate([jnp.concatenate([outs[(j, nb, pr)] for pr in range(npair)], axis=1) for j in range(nsub)], axis=0)
        if final:
            o = o + oprev_ref[nb]
            g = ggla_ref[...]
            ys = []
            for h in range(GLA_HEADS):
                oh = o[:, h * GLA_DV:(h + 1) * GLA_DV]
                ys.append(oh * lax.rsqrt(jnp.mean(oh * oh, axis=-1, keepdims=True) + NORM_EPS) * g)
            o = jnp.concatenate(ys, axis=1) * _silu(og[nb])
        o_ref[nb] = o


def _scan_batch(B):
    return 2 if B % 2 == 0 else 1


def _gla_scan(p, wgk, bgk, nzc, reverse, oprev=None, ggla=None):
    B, L, _ = p.shape
    C = SCAN_CHUNK * SCAN_SUB
    NB = _scan_batch(B)
    nc = L // C
    nzc = nzc // SCAN_SUB
    cidx = _chunk_index(reverse, nzc, nc)
    tok = lambda wd: pl.BlockSpec((NB, C, wd), lambda b, c: (b, cidx(c), 0))
    final = oprev is not None
    ins = [p, wgk, bgk]
    specs = [tok(GLA_W), _full(wgk.shape), _full(bgk.shape)]
    if final:
        ins += [oprev, ggla]
        specs += [tok(MIX_W), _full(ggla.shape)]
    return pl.pallas_call(
        functools.partial(_gla_kernel, reverse, final),
        grid=(B // NB, nc),
        in_specs=specs,
        out_specs=tok(MIX_W),
        out_shape=jax.ShapeDtypeStruct((B, L, MIX_W), F32),
        scratch_shapes=[pltpu.VMEM((NB, GLA_HEADS // 2, 2 * GLA_DK, 2 * GLA_DV), F32)],
        compiler_params=_params("parallel", "arbitrary"),
        name="gla_bwd" if reverse else "gla_fwd",
    )(*ins)


def _swa_kernel(nzb, q_ref, kvm_ref, kv0_ref, kvp_ref, kvz_ref, csq_ref, csm_ref, cs0_ref, csp_ref, sink_ref, o_ref):
    n = pl.program_id(1)
    nblk = pl.num_programs(1)
    BL = SWA_BLOCK
    kvw = SWA_KV_HEADS * SWA_HEAD_DIM
    qw = SWA_Q_HEADS * SWA_HEAD_DIM

    def rope(x, xs, cs, reps):
        cos = jnp.concatenate([cs[:, :LANE]] * reps, axis=1)
        sin = jnp.concatenate([cs[:, LANE:]] * reps, axis=1)
        return x * cos + xs * sin

    qq = q_ref[...]
    q = rope(qq[:, :qw], qq[:, qw:], csq_ref[...], qw // LANE) * (SWA_HEAD_DIM ** -0.5 * LOG2E)
    ks, vs = [], []
    for kv_ref, cs_ref in ((kvm_ref, csm_ref), (kv0_ref, cs0_ref), (kvp_ref, csp_ref)):
        kv = kv_ref[...]
        ks.append(rope(kv[:, :kvw], kv[:, kvw:2 * kvw], cs_ref[...], kvw // LANE))
        vs.append(kv[:, 2 * kvw:])
    kvz = kvz_ref[...]
    ks.append(kvz[:, :kvw])
    vs.append(kvz[:, 2 * kvw:])
    keys = jnp.concatenate(ks, axis=0)
    vals = jnp.concatenate(vs, axis=0)
    nloc = 3 * BL
    nkeys = keys.shape[0]
    qpos = n * BL + lax.broadcasted_iota(jnp.int32, (BL, nkeys), 0)
    col = lax.broadcasted_iota(jnp.int32, (BL, nkeys), 1)
    kpos = (n - 1) * BL + col
    lo = jnp.where(n >= nzb, nzb * BL, nblk * BL)
    band = (jnp.abs(kpos - qpos) <= SWA_WINDOW) & (kpos >= lo) & (kpos < nblk * BL)
    bias = jnp.where(band | (col >= nloc), 0.0, NEG_INF)
    bias = jnp.concatenate([bias] * SWA_GROUP, axis=0)
    rhead = lax.broadcasted_iota(jnp.int32, (SWA_GROUP * BL, 1), 0) // BL
    outs = [None] * SWA_Q_HEADS
    for g in range(SWA_KV_HEADS):
        hd = slice(g * SWA_HEAD_DIM, (g + 1) * SWA_HEAD_DIM)
        qs = jnp.concatenate([q[:, (g * SWA_GROUP + j) * SWA_HEAD_DIM:(g * SWA_GROUP + j + 1) * SWA_HEAD_DIM]
                              for j in range(SWA_GROUP)], axis=0)
        s = _mm_nt(qs, keys[:, hd]) + bias
        sk = jnp.zeros((SWA_GROUP * BL, 1), F32)
        for j in range(SWA_GROUP):
            sk = jnp.where(rhead == j, sink_ref[g * SWA_GROUP + j] * LOG2E, sk)
        m = jnp.maximum(jnp.max(s, axis=-1, keepdims=True), sk)
        e = jnp.exp2(s - m)
        den = jnp.sum(e, axis=-1, keepdims=True) + jnp.exp2(sk - m)
        og = _mm(e, vals[:, hd]) / den
        for j in range(SWA_GROUP):
            outs[g * SWA_GROUP + j] = og[j * BL:(j + 1) * BL]
    o_ref[...] = jnp.concatenate(outs, axis=1)


def _swa(pq, pkv, cs, sink, nzb):
    B, L, _ = pq.shape
    BL = SWA_BLOCK
    nblk = L // BL
    lz = nzb * BL
    prev = lambda n: jnp.maximum(n - 1, 0)
    nxt = lambda n: jnp.minimum(n + 1, nblk - 1)
    kv_spec = lambda f: pl.BlockSpec((None, BL, SWA_KV_W), lambda b, n: (b, f(n), 0))
    cs_spec = lambda f: pl.BlockSpec((BL, 2 * LANE), lambda b, n: (f(n), 0))
    same = lambda n: n
    return pl.pallas_call(
        functools.partial(_swa_kernel, nzb),
        grid=(B, nblk),
        in_specs=[pl.BlockSpec((None, BL, SWA_Q_W), lambda b, n: (b, n, 0)),
                  kv_spec(prev), kv_spec(same), kv_spec(nxt),
                  pl.BlockSpec((None, lz, SWA_KV_W), lambda b, n: (b, 0, 0)),
                  cs_spec(same), cs_spec(prev), cs_spec(same), cs_spec(nxt),
                  pl.BlockSpec(memory_space=pltpu.SMEM)],
        out_specs=pl.BlockSpec((None, BL, MIX_W), lambda b, n: (b, n, 0)),
        out_shape=jax.ShapeDtypeStruct((B, L, MIX_W), F32),
        compiler_params=_params("parallel", "parallel"),
        name="swa",
    )(pq, pkv, pkv, pkv, pkv, cs, cs, cs, cs, sink)


def _rwkv_feat_kernel(nzt, p_ref, pp_ref, pn_ref, vec_ref, ww_ref, wa_ref, wg_ref, e_ref,
                      o_r, o_v, o_kk, o_g, o_gb, o_lwf, o_kf, o_bf, o_lwb, o_kb, o_bb):
    i = pl.program_id(1)
    n = pl.num_programs(1)
    p = p_ref[...].astype(F32)
    TM = p.shape[0]
    first = (i == 0) | (i == nzt)
    last = (i == nzt - 1) | (i == n - 1)
    prow = jnp.where(first, 0.0, pp_ref[HALO - 1:HALO, :].astype(F32))
    nrow = jnp.where(last, 0.0, pn_ref[0:1, :].astype(F32))
    rid = lax.broadcasted_iota(jnp.int32, (TM, 1), 0)
    up = jnp.where(rid == 0, prow, pltpu.roll(p, 1, 0))
    dn = jnp.where(rid == TM - 1, nrow, pltpu.roll(p, TM - 1, 0))
    vec = vec_ref[...]
    mu = jnp.concatenate([vec[8 + j:9 + j] for j in range(4)], axis=1)[:, :RWKV_W]
    pm = p * (1.0 - mu) + (0.5 * mu) * (up + dn)
    W = MIX_W
    r, k, v = pm[:, :W], pm[:, W:2 * W], pm[:, 2 * W:3 * W]
    wa = pm[:, 3 * W:3 * W + LANE]
    gl = pm[:, 3 * W + LANE:]
    e = e_ref[...]
    k_k, k_a, r_k = vec[4:5], vec[5:6], vec[6:7]
    g = _mm(_sigmoid(gl), wg_ref[...])
    kkn = k * k_k
    kk = kkn / jnp.maximum(jnp.sqrt(_segsum(kkn * kkn, e)), 1e-12)
    twa = jnp.tanh(wa)
    ksum = None
    for d, (o_lw, o_k, o_b) in enumerate(((o_lwf, o_kf, o_bf), (o_lwb, o_kb, o_bb))):
        a = _sigmoid(vec[2 + d:3 + d] + _mm(wa, wa_ref[d]))
        kd = k * (1.0 + (a - 1.0) * k_a)
        o_lw[...] = -DECAY_SCALE * _sigmoid(vec[d:d + 1] + _mm(twa, ww_ref[d]))
        o_k[...] = kd.astype(o_k.dtype)
        o_b[...] = (kk * a).astype(o_b.dtype)
        ksum = kd if ksum is None else ksum + kd
    o_r[...] = r.astype(o_r.dtype)
    o_v[...] = v.astype(o_v.dtype)
    o_kk[...] = kk.astype(o_kk.dtype)
    o_g[...] = g.astype(o_g.dtype)
    o_gb[...] = (_segsum(r * ksum * r_k, e) * v * g).astype(o_gb.dtype)


def _rwkv_feat(p, vec, ww, wa, wg, e, nzt):
    B, L, _ = p.shape
    TM = ROW_TILE
    rh = TM // HALO
    tok = pl.BlockSpec((None, TM, RWKV_W), lambda b, i: (b, i, 0))
    halo_p = pl.BlockSpec((None, HALO, RWKV_W), lambda b, i: (b, jnp.maximum(i * rh - 1, 0), 0))
    halo_n = pl.BlockSpec((None, HALO, RWKV_W), lambda b, i: (b, jnp.minimum((i + 1) * rh, L // HALO - 1), 0))
    out = pl.BlockSpec((None, TM, MIX_W), lambda b, i: (b, i, 0))
    dts = [BF16] * 5 + [F32, BF16, BF16] * 2
    return pl.pallas_call(
        functools.partial(_rwkv_feat_kernel, nzt),
        grid=(B, L // TM),
        in_specs=[tok, halo_p, halo_n, _full(vec.shape), _full(ww.shape), _full(wa.shape), _full(wg.shape), _full(e.shape)],
        out_specs=[out] * 11,
        out_shape=[jax.ShapeDtypeStruct((B, L, MIX_W), dt) for dt in dts],
        compiler_params=_params("parallel", "parallel"),
        name="rwkv_feat",
    )(p, p, p, vec, ww, wa, wg, e)


def _rwkv_scan_kernel(reverse, final, *refs):
    if final:
        r_ref, v_ref, kk_ref, lw_ref, k_ref, b_ref, yprev_ref, g_ref, gb_ref, gn_ref, o_ref, h_ref = refs
    else:
        r_ref, v_ref, kk_ref, lw_ref, k_ref, b_ref, o_ref, h_ref = refs

    @pl.when(pl.program_id(1) == 0)
    def _():
        h_ref[...] = jnp.zeros_like(h_ref)

    NB, R = lw_ref.shape[0], lw_ref.shape[1]
    C = SCAN_CHUNK
    nsub = R // C
    order = tuple(reversed(range(nsub))) if reverse else tuple(range(nsub))
    PW = LANE
    npair = RWKV_HEADS // 2
    strict2, incl2, ident2 = _pair_order_masks(C, reverse)
    first = _first_head(PW, RWKV_HEAD)
    first2 = _first_head(2 * PW, RWKV_HEAD)
    diag = _diag_blocks(PW, RWKV_HEAD, PW)
    inclb = _block_order_mask(R, C, reverse).astype(BF16)
    kap_t, r_t, k_t, b_t, k_e, b_e, e_tot, v_all = {}, {}, {}, {}, {}, {}, {}, {}
    for nb in range(NB):
        lw = lw_ref[nb]
        cum = _mm_01(inclb, lw)
        tot = _chunk_totals(lw, C)
        e_inv = jnp.exp(-cum)
        e_end = jnp.exp(tot - cum)
        e_tot[nb] = jnp.exp(tot)
        kap_t[nb] = kk_ref[nb].astype(F32) * jnp.exp(cum - lw)
        r_t[nb] = r_ref[nb].astype(F32) * jnp.exp(cum)
        k_all, b_all = k_ref[nb].astype(F32), b_ref[nb].astype(F32)
        k_t[nb], b_t[nb] = k_all * e_inv, b_all * e_inv
        k_e[nb], b_e[nb] = k_all * e_end, b_all * e_end
        v_all[nb] = v_ref[nb].astype(F32)
    sl = lambda pr: slice(pr * PW, (pr + 1) * PW)
    rows = lambda j: slice(j * C, (j + 1) * C)
    pick = lambda d, u: d[u[1]][rows(u[0]), sl(u[2])]
    chains = [(nb, pr) for nb in range(NB) for pr in range(npair)]
    units = [(j,) + ch for j in order for ch in chains]
    vp = {u: pick(v_all, u) for u in units}
    G = {u: _mm_nt(jnp.concatenate([pick(kap_t, u), pick(r_t, u)], axis=0),
                   jnp.concatenate([_block_diag(pick(b_t, u), first), _block_diag(pick(k_t, u), first)], axis=0))
         for u in units}
    a_ab = {u: jnp.where(strict2, G[u][:C, :PW], 0.0) for u in units}
    a_ak = {u: jnp.where(strict2, G[u][:C, PW:], 0.0) for u in units}
    a_rb = {u: jnp.where(incl2, G[u][C:, :PW], 0.0) for u in units}
    a_rk = {u: jnp.where(incl2, G[u][C:, PW:], 0.0) for u in units}
    akv = {u: _mm(a_ak[u], _block_diag(vp[u], first)) for u in units}
    eye2 = jnp.where(ident2, 1.0, 0.0)
    t_inv = {u: eye2 - a_ab[u] for u in units}
    pw = {u: _mm(a_ab[u], _block_diag(a_ab[u], first)) for u in units}
    span = 2
    while span < C:
        span *= 2
        if span < C:
            prod = {u: _mm(jnp.concatenate([t_inv[u], pw[u]], axis=0), _block_diag(pw[u], first)) for u in units}
            t_inv = {u: t_inv[u] + prod[u][:C] for u in units}
            pw = {u: prod[u][C:] for u in units}
        else:
            t_inv = {u: t_inv[u] + _mm(t_inv[u], _block_diag(pw[u], first)) for u in units}
    x = {u: _mm(t_inv[u], _block_diag(jnp.concatenate([pick(kap_t, u), akv[u]], axis=1), first2)) for u in units}
    w_p = {u: x[u][:, :PW] for u in units}
    u_p = {u: x[u][:, PW:] for u in units}
    vu = {u: jnp.concatenate([vp[u], u_p[u]], axis=0) for u in units}
    bw = {u: jnp.where(diag, _mm_tn(pick(b_e, u), w_p[u]), 0.0) for u in units}
    nn = {u: jnp.where(diag, _mm_tn(jnp.concatenate([pick(k_e, u), -pick(b_e, u)], axis=0), vu[u]), 0.0) for u in units}
    rw = {u: pick(r_t, u) - _mm(a_rb[u], _block_diag(w_p[u], first)) for u in units}
    ylhs = {u: jnp.concatenate([a_rk[u], -a_rb[u], rw[u]], axis=1) for u in units}
    yrhs = {u: jnp.concatenate([_block_diag(vp[u], first), _block_diag(u_p[u], first)], axis=0) for u in units}
    dec = {u: _row_vector_to_column(e_tot[u[1]][u[0] * C:u[0] * C + 1, sl(u[2])]) for u in units}
    hs = {ch: h_ref[ch[0], ch[1]] for ch in chains}
    ys = {}
    for j in order:
        for ch in chains:
            u = (j,) + ch
            ys[u] = _mm(ylhs[u], jnp.concatenate([yrhs[u], hs[ch]], axis=0))
        for ch in chains:
            u = (j,) + ch
            hs[ch] = dec[u] * hs[ch] - _mm(bw[u], hs[ch]) + nn[u]
    for ch in chains:
        h_ref[ch[0], ch[1]] = hs[ch]
    for nb in range(NB):
        if final:
            yprev = yprev_ref[nb]
            outs = []
            for pr in range(npair):
                y = jnp.concatenate([ys[(j, nb, pr)] for j in range(nsub)], axis=0) + yprev[:, sl(pr)]
                s_a = jnp.sum(jnp.where(first, y, 0.0), axis=-1, keepdims=True)
                s_b = jnp.sum(jnp.where(first, 0.0, y), axis=-1, keepdims=True)
                yc = y - jnp.where(first, s_a, s_b) * (1.0 / RWKV_HEAD)
                q = yc * yc
                v_a = jnp.sum(jnp.where(first, q, 0.0), axis=-1, keepdims=True)
                v_b = jnp.sum(jnp.where(first, 0.0, q), axis=-1, keepdims=True)
                outs.append(yc * lax.rsqrt(jnp.where(first, v_a, v_b) * (1.0 / RWKV_HEAD) + RWKV_GN_EPS))
            gn = gn_ref[...]
            o_ref[nb] = (jnp.concatenate(outs, axis=1) * gn[0:1] + gn[1:2]) * g_ref[nb].astype(F32) + gb_ref[nb].astype(F32)
        else:
            o_ref[nb] = jnp.concatenate(
                [jnp.concatenate([ys[(j, nb, pr)] for pr in range(npair)], axis=1) for j in range(nsub)], axis=0)


def _rwkv_scan(r, v, kk, lw, k, b, nzc, reverse, yprev=None, g=None, gb=None, gn=None):
    B, L, _ = r.shape
    C = SCAN_CHUNK * SCAN_SUB
    NB = _scan_batch(B)
    nc = L // C
    nzc = nzc // SCAN_SUB
    cidx = _chunk_index(reverse, nzc, nc)
    tok = pl.BlockSpec((NB, C, MIX_W), lambda b_, c: (b_, cidx(c), 0))
    final = yprev is not None
    ins = [r, v, kk, lw, k, b]
    specs = [tok] * 6
    if final:
        ins += [yprev, g, gb, gn]
        specs += [tok] * 3 + [_full(gn.shape)]
    return pl.pallas_call(
        functools.partial(_rwkv_scan_kernel, reverse, final),
        grid=(B // NB, nc),
        in_specs=specs,
        out_specs=tok,
        out_shape=jax.ShapeDtypeStruct((B, L, MIX_W), F32),
        scratch_shapes=[pltpu.VMEM((NB, RWKV_HEADS // 2, LANE, LANE), F32)],
        compiler_params=_params("parallel", "arbitrary"),
        name="rwkv_bwd" if reverse else "rwkv_fwd",
    )(*ins)


def _merge_kernel(s_ref, mod_ref, g_ref, ya_ref, yb_ref, yc_ref, wg_ref, wb_ref, wo_ref, o_ref):
    m = mod_ref[...]
    s = s_ref[...]
    h = _norm_mod(s, g_ref[...], m[0:1], m[1:2]).astype(BF16)
    gates = jnp.dot(h, wg_ref[...], preferred_element_type=F32)
    acc = None
    for i, y_ref in enumerate((ya_ref, yb_ref, yc_ref)):
        t = _sigmoid(gates[:, i * D_MODEL:(i + 1) * D_MODEL]) * _mm(y_ref[...], wb_ref[i])
        acc = t if acc is None else acc + t
    o_ref[...] = s + m[2:3] * _mm(acc, wo_ref[...])


def _merge(s, mod3, g, ya, yb, yc, wg, wb, wo, nzt, off):
    B, L, _ = s.shape
    TM = ROW_TILE
    nt = L // TM - off
    tok_in = lambda wd: pl.BlockSpec((None, TM, wd), lambda b, i: (b, i + off, 0))
    return pl.pallas_call(
        _merge_kernel,
        grid=(B, nt),
        in_specs=[tok_in(D_MODEL), _mod_spec(nzt, B, off), _full((1, D_MODEL)), tok_in(MIX_W), tok_in(MIX_W), tok_in(MIX_W),
                  _full(wg.shape), _full(wb.shape), _full(wo.shape)],
        out_specs=pl.BlockSpec((None, TM, D_MODEL), lambda b, i: (b, i, 0)),
        out_shape=jax.ShapeDtypeStruct((B, nt * TM, D_MODEL), F32),
        compiler_params=_params("parallel", "parallel"),
        name="merge",
    )(s, mod3, g, ya, yb, yc, wg, wb, wo)


def _ffn_kernel(final, s_ref, mod_ref, g_ref, w1_ref, w3_ref, w2_ref, gf_ref, o_ref):
    m = mod_ref[...]
    s = s_ref[...]
    h = _norm_mod(s, g_ref[...], m[3:4], m[4:5]).astype(BF16)
    a = jnp.dot(h, w1_ref[...], preferred_element_type=F32)
    b = jnp.dot(h, w3_ref[...], preferred_element_type=F32)
    o = s + m[5:6] * _mm(_silu(a) * b, w2_ref[...])
    if final:
        o = o * lax.rsqrt(jnp.mean(o * o, axis=-1, keepdims=True) + NORM_EPS) * gf_ref[...]
    o_ref[...] = o


def _ffn(s, mod3, g, w1, w3, w2, gf, nzt, off, final):
    B, L, _ = s.shape
    TM = ROW_TILE
    nt = L // TM
    const = lambda shape: pl.BlockSpec(shape, lambda b, i: (0,) * len(shape), pipeline_mode=pl.Buffered(1))
    return pl.pallas_call(
        functools.partial(_ffn_kernel, final),
        grid=(B, nt),
        in_specs=[pl.BlockSpec((None, TM, D_MODEL), lambda b, i: (b, i, 0)), _mod_spec(nzt, B, off), _full((1, D_MODEL)),
                  const(w1.shape), const(w3.shape), const(w2.shape), _full((1, D_MODEL))],
        out_specs=pl.BlockSpec((None, TM, D_MODEL), lambda b, i: (b, i, 0)),
        out_shape=jax.ShapeDtypeStruct((B, L, D_MODEL), F32),
        compiler_params=_params("parallel", "parallel"),
        name="ffn",
    )(s, mod3, g, w1, w3, w2, gf)


def _rope_tables(T, lz):
    rows = T // GRID_W
    row = jnp.repeat(jnp.arange(rows), GRID_W).astype(F32)
    col = jnp.tile(jnp.arange(GRID_W), rows).astype(F32)
    inv = ROPE_BASE ** (-jnp.arange(ROPE_FREQS, dtype=F32) / ROPE_FREQS)
    ar, ac = row[:, None] * inv, col[:, None] * inv
    cos = jnp.concatenate([jnp.cos(ar), jnp.cos(ar), jnp.cos(ac), jnp.cos(ac)], axis=1)
    sin = jnp.concatenate([-jnp.sin(ar), jnp.sin(ar), -jnp.sin(ac), jnp.sin(ac)], axis=1)
    reps = LANE // SWA_HEAD_DIM
    x_tab = jnp.concatenate([jnp.tile(cos, (1, reps)), jnp.tile(sin, (1, reps))], axis=1)
    z_tab = jnp.concatenate([jnp.ones((lz, LANE), F32), jnp.zeros((lz, LANE), F32)], axis=1)
    return jnp.concatenate([z_tab, x_tab], axis=0)


def _partner_cols(w):
    return w.reshape(w.shape[0], -1, 2, ROPE_FREQS)[:, :, ::-1, :].reshape(w.shape)


def _mixer_weight(w_in):
    g0, g1, g2 = np.cumsum(GROUP_SPLIT)[:3]
    gla, swa, rwkv = w_in[:, :g0], w_in[:, g0:g1], w_in[:, g1:g2]
    nk = GLA_HEADS * GLA_DK
    a = 2 * nk + MIX_W
    gla_p = jnp.concatenate([gla[:, :a], gla[:, a + 2 * GLA_GATE_RANK:], gla[:, a:a + 2 * GLA_GATE_RANK],
                             jnp.zeros((D_MODEL, LANE - 2 * GLA_GATE_RANK), F32)], axis=1)
    qw = SWA_Q_HEADS * SWA_HEAD_DIM
    kvw = SWA_KV_HEADS * SWA_HEAD_DIM
    q, k, v = swa[:, :qw], swa[:, qw:qw + kvw], swa[:, qw + kvw:]
    w = jnp.concatenate([gla_p, q, _partner_cols(q), k, _partner_cols(k), v, rwkv], axis=1)
    return w.astype(BF16), w_in[:, g2:].astype(BF16)


def _pad_rows(w, top, total):
    return jnp.concatenate([jnp.zeros((top, w.shape[1]), F32), w, jnp.zeros((total - top - w.shape[0], w.shape[1]), F32)], axis=0)


def kernel(x, c, ctx, c_ctx, w_ada, b_ada, g_mix, g_ffn, w_in, w_gk2, b_gk, g_gla, sink, mu_shift, w0, w_w2, a0, w_a2,
           w_g2, k_k, k_a, r_k, gn_w, gn_b, w_branch, w_out, w_ffn1, w_ffn3, w_ffn2, g_final):
    B, T, D = x.shape
    lz = ctx.shape[1]
    depth = w_in.shape[0]
    assert D == D_MODEL and lz % ROW_TILE == 0 and T % ROW_TILE == 0 and T % GRID_W == 0
    nzt, nzc, nzb = lz // ROW_TILE, lz // SCAN_CHUNK, lz // SWA_BLOCK
    s = jnp.concatenate([ctx, x], axis=1)
    cs = _rope_tables(T, lz)
    c8 = jnp.concatenate([c, c_ctx[None], jnp.zeros((8 - (B + 1) % 8 if (B + 1) % 8 else 0, D), F32)], axis=0)
    hidx = np.arange(MIX_W) // RWKV_HEAD
    seg = jnp.asarray(hidx[:, None] == hidx[None, :], BF16)
    out = None
    for l in range(depth):
        last = l == depth - 1
        mod3 = _ada(c8, w_ada[l], b_ada[l]).reshape(c8.shape[0], 6, D)
        w_mix, w_gate = _mixer_weight(w_in[l])
        p_gla, p_q, p_kv, p_rwkv = _inproj(s, mod3, g_mix[l][None], w_mix, nzt)

        wgk = [_pad_rows(w_gk2[l, d], d * GLA_GATE_RANK, LANE).astype(BF16) for d in range(2)]
        o_f = _gla_scan(p_gla, wgk[0], b_gk[l, 0][None], nzc, False)
        ya = _gla_scan(p_gla, wgk[1], b_gk[l, 1][None], nzc, True, o_f, g_gla[l][None])

        yb = _swa(p_q, p_kv, cs, sink[l], nzb)

        zero = jnp.zeros((MIX_W,), F32)
        mu4 = jnp.concatenate([mu_shift[l], jnp.zeros((4 * MIX_W - RWKV_W,), F32)]).reshape(4, MIX_W)
        vec = jnp.stack([w0[l, 0], w0[l, 1], a0[l, 0], a0[l, 1], k_k[l], k_a[l], r_k[l].reshape(-1), zero], axis=0)
        vec = jnp.concatenate([vec, mu4, jnp.zeros((4, MIX_W), F32)], axis=0)
        ww = jnp.stack([_pad_rows(w_w2[l, d], 0, LANE) for d in range(2)]).astype(BF16)
        wa = jnp.stack([_pad_rows(w_a2[l, d], RWKV_DECAY_RANK, LANE) for d in range(2)]).astype(BF16)
        f_r, f_v, f_kk, f_g, f_gb, lw_f, k_f, b_f, lw_b, k_b, b_b = _rwkv_feat(
            p_rwkv, vec, ww, wa, w_g2[l].astype(BF16), seg, nzt)
        y_f = _rwkv_scan(f_r, f_v, f_kk, lw_f, k_f, b_f, nzc, False)
        yc = _rwkv_scan(f_r, f_v, f_kk, lw_b, k_b, b_b, nzc, True, y_f, f_g, f_gb, jnp.stack([gn_w[l], gn_b[l]]))

        off = nzt if last else 0
        s_mix = _merge(s, mod3, g_mix[l][None], ya, yb, yc, w_gate, w_branch[l].astype(BF16), w_out[l].astype(BF16), nzt, off)
        s = _ffn(s_mix, mod3, g_ffn[l][None], w_ffn1[l].astype(BF16), w_ffn3[l].astype(BF16), w_ffn2[l].astype(BF16),
                 g_final[None], nzt, off, last)
        out = s
    return out
```

```python
import functools

import jax
import jax.numpy as jnp
import numpy as np
from jax import lax
from jax.experimental import pallas as pl
from jax.experimental.pallas import tpu as pltpu

F32 = jnp.float32
BF16 = jnp.bfloat16
HI = lax.Precision.HIGHEST
LOG2E = 1.4426950408889634
DECAY_SCALE = 0.6065306597126334

D_MODEL = 1024
GRID_W = 64
NORM_EPS = 1e-6
NEG_INF = -1e30
MIX_W = D_MODEL // 2
N_BRANCH = 3
GLA_HEADS = 4
GLA_DV = MIX_W // GLA_HEADS
GLA_DK = GLA_DV // 2
GLA_GATE_RANK = 16
GLA_GATE_NORM = 16.0
SWA_HEAD_DIM = 64
SWA_Q_HEADS = MIX_W // SWA_HEAD_DIM
SWA_KV_HEADS = SWA_Q_HEADS // 4
SWA_GROUP = SWA_Q_HEADS // SWA_KV_HEADS
SWA_WINDOW = 128
SWA_BLOCK = 128
ROPE_FREQS = SWA_HEAD_DIM // 4
ROPE_BASE = 10000.0
RWKV_HEAD = 64
RWKV_HEADS = MIX_W // RWKV_HEAD
RWKV_DECAY_RANK = 64
RWKV_A_RANK = 64
RWKV_GATE_RANK = 128
RWKV_GN_EPS = 64e-5
FFN_HIDDEN = -(-8 * D_MODEL // (3 * 256)) * 256
GLA_SPLIT = (GLA_HEADS * GLA_DK, GLA_HEADS * GLA_DK, MIX_W, GLA_GATE_RANK, GLA_GATE_RANK, MIX_W)
SWA_SPLIT = (SWA_Q_HEADS * SWA_HEAD_DIM, SWA_KV_HEADS * SWA_HEAD_DIM, SWA_KV_HEADS * SWA_HEAD_DIM)
RWKV_SPLIT = (MIX_W, MIX_W, MIX_W, RWKV_DECAY_RANK, RWKV_A_RANK, RWKV_GATE_RANK)
GROUP_SPLIT = (sum(GLA_SPLIT), sum(SWA_SPLIT), sum(RWKV_SPLIT), N_BRANCH * D_MODEL)

LANE = 128
ROW_TILE = 256
HALO = 16
SCAN_CHUNK = 64
SCAN_SUB = 2
GLA_SUB = 4
PAIR_HEAD = LANE // 2
assert GLA_DK == PAIR_HEAD and RWKV_HEAD == PAIR_HEAD and SCAN_CHUNK == PAIR_HEAD
GLA_W = 2 * GLA_HEADS * GLA_DK + 2 * MIX_W + LANE
SWA_Q_W = 1024
SWA_KV_W = 384
RWKV_W = sum(RWKV_SPLIT)
MIXER_W = GLA_W + SWA_Q_W + SWA_KV_W + RWKV_W
VMEM_LIMIT = 56 * 1024 * 1024


def _mm(a, b):
    return jnp.dot(a.astype(BF16), b.astype(BF16), preferred_element_type=F32)


def _mm_nt(a, b):
    return lax.dot_general(a.astype(BF16), b.astype(BF16), (((1,), (1,)), ((), ())), preferred_element_type=F32)


def _mm_tn(a, b):
    return lax.dot_general(a.astype(BF16), b.astype(BF16), (((0,), (0,)), ((), ())), preferred_element_type=F32)


def _mm_hi(a, b):
    return jnp.dot(a, b, precision=HI, preferred_element_type=F32)


def _sigmoid(x):
    return 1.0 / (1.0 + jnp.exp(-x))


def _silu(x):
    return x * _sigmoid(x)


def _softplus(x):
    return jnp.maximum(x, 0.0) + jnp.log(1.0 + jnp.exp(-jnp.abs(x)))


def _norm_mod(x, g, shift, scale):
    y = x * lax.rsqrt(jnp.mean(x * x, axis=-1, keepdims=True) + NORM_EPS)
    return (y * g) * (1.0 + scale) + shift


def _segsum(x, e):
    hi = x.astype(BF16)
    lo = (x - hi.astype(F32)).astype(BF16)
    return jnp.dot(hi, e, preferred_element_type=F32) + jnp.dot(lo, e, preferred_element_type=F32)


def _params(*sem):
    return pltpu.CompilerParams(dimension_semantics=sem, vmem_limit_bytes=VMEM_LIMIT)


def _full(shape):
    nd = len(shape)
    return pl.BlockSpec(shape, lambda *_: (0,) * nd)


def _ada_kernel(c_ref, w_ref, b_ref, o_ref):
    o_ref[...] = _mm_hi(_silu(c_ref[...]), w_ref[...]) + b_ref[...]


def _ada(c8, w, b):
    depth, _, n = w.shape
    tn = 1536
    return pl.pallas_call(
        _ada_kernel,
        grid=(depth, n // tn),
        in_specs=[_full(c8.shape), pl.BlockSpec((None, D_MODEL, tn), lambda l, j: (l, 0, j)),
                  pl.BlockSpec((None, 1, tn), lambda l, j: (l, 0, j))],
        out_specs=pl.BlockSpec((None, c8.shape[0], tn), lambda l, j: (l, 0, j)),
        out_shape=jax.ShapeDtypeStruct((depth, c8.shape[0], n), F32),
        compiler_params=_params("arbitrary", "arbitrary"),
        name="ada",
    )(c8, w, b.reshape(depth, 1, n))


class _Stream:
    def __init__(self, arrays, nzt):
        self.arrays, self.nzt = tuple(arrays), nzt
        self.split = len(self.arrays) == 2
        self.batch = self.arrays[0].shape[0]
        self.rows = sum(a.shape[1] for a in self.arrays)

    def specs(self, off=0):
        TM, D, nzt = ROW_TILE, D_MODEL, self.nzt
        if not self.split:
            return [pl.BlockSpec((None, TM, D), lambda b, i: (b, i + off, 0))]
        return [pl.BlockSpec((None, TM, D), lambda b, i: (b, jnp.minimum(i + off, nzt - 1), 0)),
                pl.BlockSpec((None, TM, D), lambda b, i: (b, jnp.maximum(i + off - nzt, 0), 0))]

    def read(self, refs, off=0):
        if not self.split:
            return refs[0][...]
        return jnp.where(pl.program_id(1) + off < self.nzt, refs[0][...], refs[1][...])


def _layer_spec(w, l, **kw):
    nd = w.ndim - 1
    return pl.BlockSpec((None,) + w.shape[1:], lambda *_: (l,) + (0,) * nd, **kw)


def _mod_spec(l, nzt, nb, off=0):
    return pl.BlockSpec((None, None, 6, D_MODEL), lambda b, i: (l, jnp.where(i + off < nzt, nb, b), 0, 0))


def _inproj_kernel(stream, *refs):
    ns = len(stream.arrays)
    mod_ref, g_ref, w_ref, o_gla, o_q, o_kv, o_rwkv = refs[ns:]
    m = mod_ref[...]
    h = _norm_mod(stream.read(refs[:ns]), g_ref[...], m[0:1], m[1:2]).astype(BF16)
    p = jnp.dot(h, w_ref[...], preferred_element_type=F32)
    o0 = 0
    for o_ref in (o_gla, o_q, o_kv, o_rwkv):
        wdt = o_ref.shape[-1]
        o_ref[...] = p[:, o0:o0 + wdt].astype(o_ref.dtype)
        o0 += wdt


def _inproj(stream, l, mod, g, w):
    B, L = stream.batch, stream.rows
    widths = (GLA_W, SWA_Q_W, SWA_KV_W, RWKV_W)
    return pl.pallas_call(
        functools.partial(_inproj_kernel, stream),
        grid=(B, L // ROW_TILE),
        in_specs=stream.specs() + [_mod_spec(l, stream.nzt, B), _layer_spec(g, l), _layer_spec(w, l)],
        out_specs=[pl.BlockSpec((None, ROW_TILE, wd), lambda b, i: (b, i, 0)) for wd in widths],
        out_shape=[jax.ShapeDtypeStruct((B, L, wd), dt) for wd, dt in zip(widths, (F32, F32, F32, BF16))],
        compiler_params=_params("parallel", "parallel"),
        name="inproj",
    )(*stream.arrays, mod, g, w)


def _chunk_index(reverse, nzc, nc):
    if not reverse:
        return lambda c: c
    return lambda c: jnp.where(c < nzc, nzc - 1 - c, nzc + nc - 1 - c)


def _order_masks(n, reverse):
    ii = lax.broadcasted_iota(jnp.int32, (n, n), 0)
    jj = lax.broadcasted_iota(jnp.int32, (n, n), 1)
    strict = (jj > ii) if reverse else (jj < ii)
    return strict, strict | (ii == jj), ii == jj


def _block_order_mask(n, chunk, reverse):
    ii = lax.broadcasted_iota(jnp.int32, (n, n), 0)
    jj = lax.broadcasted_iota(jnp.int32, (n, n), 1)
    return ((ii // chunk) == (jj // chunk)) & ((jj >= ii) if reverse else (jj <= ii))


def _mm_01(m01, x):
    hi = x.astype(BF16)
    r1 = x - hi.astype(F32)
    mid = r1.astype(BF16)
    lo = (r1 - mid.astype(F32)).astype(BF16)
    dot = lambda piece: jnp.dot(m01, piece, preferred_element_type=F32)
    return dot(hi) + dot(mid) + dot(lo)


def _chunk_row(x, chunk, r):
    parts = [jnp.broadcast_to(x[j + r:j + r + 1], (chunk, x.shape[1])) for j in range(0, x.shape[0], chunk)]
    return parts[0] if len(parts) == 1 else jnp.concatenate(parts, axis=0)


def _chunk_totals(x, chunk):
    parts = [jnp.broadcast_to(jnp.sum(x[j:j + chunk], axis=0, keepdims=True), (chunk, x.shape[1]))
             for j in range(0, x.shape[0], chunk)]
    return parts[0] if len(parts) == 1 else jnp.concatenate(parts, axis=0)


def _pair_order_masks(n, reverse):
    ii = lax.broadcasted_iota(jnp.int32, (n, LANE), 0)
    jj = lax.broadcasted_iota(jnp.int32, (n, LANE), 1) & (PAIR_HEAD - 1)
    strict = (jj > ii) if reverse else (jj < ii)
    return strict, strict | (ii == jj), ii == jj


def _first_head(width, head_w):
    return (lax.broadcasted_iota(jnp.int32, (1, width), 1) & head_w) == 0


def _block_diag(z, first):
    return jnp.concatenate([jnp.where(first, z, 0.0), jnp.where(first, 0.0, z)], axis=0)


def _diag_blocks(rows, head_w_cols, cols):
    ri = lax.broadcasted_iota(jnp.int32, (rows, cols), 0) // (rows // 2)
    ci = lax.broadcasted_iota(jnp.int32, (rows, cols), 1) // head_w_cols
    return ri == ci


def _row_vector_to_column(row):
    n = row.shape[1]
    eye = lax.broadcasted_iota(jnp.int32, (n, n), 0) == lax.broadcasted_iota(jnp.int32, (n, n), 1)
    return jnp.sum(jnp.where(eye, row, 0.0), axis=1, keepdims=True)


def _gla_kernel(reverse, final, *refs):
    if final:
        p_ref, wgk_ref, bgk_ref, oprev_ref, ggla_ref, o_ref, h_ref = refs
    else:
        p_ref, wgk_ref, bgk_ref, o_ref, h_ref = refs

    @pl.when(pl.program_id(1) == 0)
    def _():
        h_ref[...] = jnp.zeros_like(h_ref)

    NB, R = p_ref.shape[0], p_ref.shape[1]
    C = SCAN_CHUNK
    nsub = R // C
    order = tuple(reversed(range(nsub))) if reverse else tuple(range(nsub))
    nk = GLA_HEADS * GLA_DK
    npair = GLA_HEADS // 2
    kw, vw = 2 * GLA_DK, 2 * GLA_DV
    incl2 = _pair_order_masks(C, reverse)[1]
    first_k = _first_head(kw, GLA_DK)
    first_v = _first_head(vw, GLA_DV)
    diag = _diag_blocks(kw, GLA_DV, vw)
    inclb = _block_order_mask(R, C, reverse).astype(BF16)
    q_t, q_m, k_t, k_e, e_tot, v, og = {}, {}, {}, {}, {}, {}, {}
    for nb in range(NB):
        p = p_ref[nb]
        q = p[:, 0:nk] * GLA_DK ** -0.5
        k = p[:, nk:2 * nk]
        v[nb] = p[:, 2 * nk:2 * nk + MIX_W]
        og[nb] = p[:, 2 * nk + MIX_W:2 * nk + 2 * MIX_W]
        gk = p[:, 2 * nk + 2 * MIX_W:]
        lg = -_softplus(-(_mm(gk, wgk_ref[...]) + bgk_ref[...])) / GLA_GATE_NORM
        cum = _mm_01(inclb, lg)
        tot = _chunk_totals(lg, C)
        mid = _chunk_row(cum, C, C // 2)
        q_t[nb] = q * jnp.exp(cum)
        q_m[nb] = q * jnp.exp(cum - mid)
        k_t[nb] = k * jnp.exp(mid - cum)
        k_e[nb] = k * jnp.exp(tot - cum)
        e_tot[nb] = jnp.exp(tot)
    ksl = lambda pr: slice(pr * kw, (pr + 1) * kw)
    vsl = lambda pr: slice(pr * vw, (pr + 1) * vw)
    rows = lambda j: slice(j * C, (j + 1) * C)
    chains = [(nb, pr) for nb in range(NB) for pr in range(npair)]
    units = [(j, nb, pr) for j in order for (nb, pr) in chains]
    qp = {(j, nb, pr): q_t[nb][rows(j), ksl(pr)] for (j, nb, pr) in units}
    vp = {(j, nb, pr): v[nb][rows(j), vsl(pr)] for (j, nb, pr) in units}
    att = {(j, nb, pr): jnp.where(incl2, _mm_nt(q_m[nb][rows(j), ksl(pr)],
                                                 _block_diag(k_t[nb][rows(j), ksl(pr)], first_k)), 0.0)
           for (j, nb, pr) in units}
    kv = {(j, nb, pr): jnp.where(diag, _mm_tn(k_e[nb][rows(j), ksl(pr)], vp[(j, nb, pr)]), 0.0) for (j, nb, pr) in units}
    dec = {(j, nb, pr): _row_vector_to_column(e_tot[nb][j * C:j * C + 1, ksl(pr)]) for (j, nb, pr) in units}
    lhs = {u: jnp.concatenate([att[u], qp[u]], axis=1) for u in units}
    bdv = {u: _block_diag(vp[u], first_v) for u in units}
    hs = {ch: h_ref[ch[0], ch[1]] for ch in chains}
    outs = {}
    for j in order:
        for ch in chains:
            u = (j,) + ch
            outs[u] = _mm(lhs[u], jnp.concatenate([bdv[u], hs[ch]], axis=0))
        for ch in chains:
            u = (j,) + ch
            hs[ch] = dec[u] * hs[ch] + kv[u]
    for ch in chains:
        h_ref[ch[0], ch[1]] = hs[ch]
    for nb in range(NB):
        o = jnp.concatenate([jnp.concatenate([outs[(j, nb, pr)] for pr in range(npair)], axis=1) for j in range(nsub)], axis=0)
        if final:
            o = o + oprev_ref[nb]
            g = ggla_ref[...]
            ys = []
            for h in range(GLA_HEADS):
                oh = o[:, h * GLA_DV:(h + 1) * GLA_DV]
                ys.append(oh * lax.rsqrt(jnp.mean(oh * oh, axis=-1, keepdims=True) + NORM_EPS) * g)
            o = jnp.concatenate(ys, axis=1) * _silu(og[nb])
        o_ref[nb] = o


def _scan_batch(B):
    return 2 if B % 2 == 0 else 1


def _gla_scan(p, wgk, bgk, nzc, reverse, oprev=None, ggla=None):
    B, L, _ = p.shape
    C = SCAN_CHUNK * GLA_SUB
    NB = _scan_batch(B)
    nc = L // C
    nzc = nzc // GLA_SUB
    cidx = _chunk_index(reverse, nzc, nc)
    tok = lambda wd: pl.BlockSpec((NB, C, wd), lambda b, c: (b, cidx(c), 0))
    final = oprev is not None
    ins = [p, wgk, bgk]
    specs = [tok(GLA_W), _full(wgk.shape), _full(bgk.shape)]
    if final:
        ins += [oprev, ggla]
        specs += [tok(MIX_W), _full(ggla.shape)]
    return pl.pallas_call(
        functools.partial(_gla_kernel, reverse, final),
        grid=(B // NB, nc),
        in_specs=specs,
        out_specs=tok(MIX_W),
        out_shape=jax.ShapeDtypeStruct((B, L, MIX_W), F32),
        scratch_shapes=[pltpu.VMEM((NB, GLA_HEADS // 2, 2 * GLA_DK, 2 * GLA_DV), F32)],
        compiler_params=_params("parallel", "arbitrary"),
        name="gla_bwd" if reverse else "gla_fwd",
    )(*ins)


def _swa_kernel(nzb, q_ref, kvm_ref, kv0_ref, kvp_ref, kvz_ref, csq_ref, csm_ref, cs0_ref, csp_ref, sink_ref, o_ref):
    n = pl.program_id(1)
    nblk = pl.num_programs(1)
    BL = SWA_BLOCK
    kvw = SWA_KV_HEADS * SWA_HEAD_DIM
    qw = SWA_Q_HEADS * SWA_HEAD_DIM

    def rope(x, xs, cs, reps):
        cos = jnp.concatenate([cs[:, :LANE]] * reps, axis=1)
        sin = jnp.concatenate([cs[:, LANE:]] * reps, axis=1)
        return x * cos + xs * sin

    qq = q_ref[...]
    q = rope(qq[:, :qw], qq[:, qw:], csq_ref[...], qw // LANE) * (SWA_HEAD_DIM ** -0.5 * LOG2E)
    ks, vs = [], []
    for kv_ref, cs_ref in ((kvm_ref, csm_ref), (kv0_ref, cs0_ref), (kvp_ref, csp_ref)):
        kv = kv_ref[...]
        ks.append(rope(kv[:, :kvw], kv[:, kvw:2 * kvw], cs_ref[...], kvw // LANE))
        vs.append(kv[:, 2 * kvw:])
    kvz = kvz_ref[...]
    ks.append(kvz[:, :kvw])
    vs.append(kvz[:, 2 * kvw:])
    keys = jnp.concatenate(ks, axis=0)
    vals = jnp.concatenate(vs, axis=0)
    nloc = 3 * BL
    nkeys = keys.shape[0]
    qpos = n * BL + lax.broadcasted_iota(jnp.int32, (BL, nkeys), 0)
    col = lax.broadcasted_iota(jnp.int32, (BL, nkeys), 1)
    kpos = (n - 1) * BL + col
    lo = jnp.where(n >= nzb, nzb * BL, nblk * BL)
    band = (jnp.abs(kpos - qpos) <= SWA_WINDOW) & (kpos >= lo) & (kpos < nblk * BL)
    bias = jnp.where(band | (col >= nloc), 0.0, NEG_INF)
    bias = jnp.concatenate([bias] * SWA_GROUP, axis=0)
    rhead = lax.broadcasted_iota(jnp.int32, (SWA_GROUP * BL, 1), 0) // BL
    outs = [None] * SWA_Q_HEADS
    for g in range(SWA_KV_HEADS):
        hd = slice(g * SWA_HEAD_DIM, (g + 1) * SWA_HEAD_DIM)
        qs = jnp.concatenate([q[:, (g * SWA_GROUP + j) * SWA_HEAD_DIM:(g * SWA_GROUP + j + 1) * SWA_HEAD_DIM]
                              for j in range(SWA_GROUP)], axis=0)
        s = _mm_nt(qs, keys[:, hd]) + bias
        sk = jnp.zeros((SWA_GROUP * BL, 1), F32)
        for j in range(SWA_GROUP):
            sk = jnp.where(rhead == j, sink_ref[g * SWA_GROUP + j] * LOG2E, sk)
        m = jnp.maximum(jnp.max(s, axis=-1, keepdims=True), sk)
        e = jnp.exp2(s - m)
        den = jnp.sum(e, axis=-1, keepdims=True) + jnp.exp2(sk - m)
        og = _mm(e, vals[:, hd]) / den
        for j in range(SWA_GROUP):
            outs[g * SWA_GROUP + j] = og[j * BL:(j + 1) * BL]
    o_ref[...] = jnp.concatenate(outs, axis=1)


def _swa(pq, pkv, cs, sink, nzb):
    B, L, _ = pq.shape
    BL = SWA_BLOCK
    nblk = L // BL
    lz = nzb * BL
    prev = lambda n: jnp.maximum(n - 1, 0)
    nxt = lambda n: jnp.minimum(n + 1, nblk - 1)
    kv_spec = lambda f: pl.BlockSpec((None, BL, SWA_KV_W), lambda b, n: (b, f(n), 0))
    cs_spec = lambda f: pl.BlockSpec((BL, 2 * LANE), lambda b, n: (f(n), 0))
    same = lambda n: n
    return pl.pallas_call(
        functools.partial(_swa_kernel, nzb),
        grid=(B, nblk),
        in_specs=[pl.BlockSpec((None, BL, SWA_Q_W), lambda b, n: (b, n, 0)),
                  kv_spec(prev), kv_spec(same), kv_spec(nxt),
                  pl.BlockSpec((None, lz, SWA_KV_W), lambda b, n: (b, 0, 0)),
                  cs_spec(same), cs_spec(prev), cs_spec(same), cs_spec(nxt),
                  pl.BlockSpec(memory_space=pltpu.SMEM)],
        out_specs=pl.BlockSpec((None, BL, MIX_W), lambda b, n: (b, n, 0)),
        out_shape=jax.ShapeDtypeStruct((B, L, MIX_W), F32),
        compiler_params=_params("parallel", "parallel"),
        name="swa",
    )(pq, pkv, pkv, pkv, pkv, cs, cs, cs, cs, sink)


def _rwkv_feat_kernel(nzt, p_ref, pp_ref, pn_ref, vec_ref, ww_ref, wa_ref, wg_ref, e_ref,
                      o_r, o_v, o_kk, o_g, o_gb, o_lwf, o_kf, o_bf, o_lwb, o_kb, o_bb):
    i = pl.program_id(1)
    n = pl.num_programs(1)
    p = p_ref[...].astype(F32)
    TM = p.shape[0]
    first = (i == 0) | (i == nzt)
    last = (i == nzt - 1) | (i == n - 1)
    prow = jnp.where(first, 0.0, pp_ref[HALO - 1:HALO, :].astype(F32))
    nrow = jnp.where(last, 0.0, pn_ref[0:1, :].astype(F32))
    rid = lax.broadcasted_iota(jnp.int32, (TM, 1), 0)
    up = jnp.where(rid == 0, prow, pltpu.roll(p, 1, 0))
    dn = jnp.where(rid == TM - 1, nrow, pltpu.roll(p, TM - 1, 0))
    vec = vec_ref[...]
    mu = jnp.concatenate([vec[8 + j:9 + j] for j in range(4)], axis=1)[:, :RWKV_W]
    pm = p * (1.0 - mu) + (0.5 * mu) * (up + dn)
    W = MIX_W
    r, k, v = pm[:, :W], pm[:, W:2 * W], pm[:, 2 * W:3 * W]
    wa = pm[:, 3 * W:3 * W + LANE]
    gl = pm[:, 3 * W + LANE:]
    e = e_ref[...]
    k_k, k_a, r_k = vec[4:5], vec[5:6], vec[6:7]
    g = _mm(_sigmoid(gl), wg_ref[...])
    kkn = k * k_k
    kk = kkn / jnp.maximum(jnp.sqrt(_segsum(kkn * kkn, e)), 1e-12)
    twa = jnp.tanh(wa)
    ksum = None
    for d, (o_lw, o_k, o_b) in enumerate(((o_lwf, o_kf, o_bf), (o_lwb, o_kb, o_bb))):
        a = _sigmoid(vec[2 + d:3 + d] + _mm(wa, wa_ref[d]))
        kd = k * (1.0 + (a - 1.0) * k_a)
        o_lw[...] = -DECAY_SCALE * _sigmoid(vec[d:d + 1] + _mm(twa, ww_ref[d]))
        o_k[...] = kd.astype(o_k.dtype)
        o_b[...] = (kk * a).astype(o_b.dtype)
        ksum = kd if ksum is None else ksum + kd
    o_r[...] = r.astype(o_r.dtype)
    o_v[...] = v.astype(o_v.dtype)
    o_kk[...] = kk.astype(o_kk.dtype)
    o_g[...] = g.astype(o_g.dtype)
    o_gb[...] = (_segsum(r * ksum * r_k, e) * v * g).astype(o_gb.dtype)


def _rwkv_feat(p, vec, ww, wa, wg, e, nzt):
    B, L, _ = p.shape
    TM = ROW_TILE
    rh = TM // HALO
    tok = pl.BlockSpec((None, TM, RWKV_W), lambda b, i: (b, i, 0))
    halo_p = pl.BlockSpec((None, HALO, RWKV_W), lambda b, i: (b, jnp.maximum(i * rh - 1, 0), 0))
    halo_n = pl.BlockSpec((None, HALO, RWKV_W), lambda b, i: (b, jnp.minimum((i + 1) * rh, L // HALO - 1), 0))
    out = pl.BlockSpec((None, TM, MIX_W), lambda b, i: (b, i, 0))
    dts = [BF16] * 5 + [F32, BF16, BF16] * 2
    return pl.pallas_call(
        functools.partial(_rwkv_feat_kernel, nzt),
        grid=(B, L // TM),
        in_specs=[tok, halo_p, halo_n, _full(vec.shape), _full(ww.shape), _full(wa.shape), _full(wg.shape), _full(e.shape)],
        out_specs=[out] * 11,
        out_shape=[jax.ShapeDtypeStruct((B, L, MIX_W), dt) for dt in dts],
        compiler_params=_params("parallel", "parallel"),
        name="rwkv_feat",
    )(p, p, p, vec, ww, wa, wg, e)


def _rwkv_scan_kernel(reverse, final, *refs):
    if final:
        r_ref, v_ref, kk_ref, lw_ref, k_ref, b_ref, yprev_ref, g_ref, gb_ref, gn_ref, o_ref, h_ref = refs
    else:
        r_ref, v_ref, kk_ref, lw_ref, k_ref, b_ref, o_ref, h_ref = refs

    @pl.when(pl.program_id(1) == 0)
    def _():
        h_ref[...] = jnp.zeros_like(h_ref)

    NB, R = lw_ref.shape[0], lw_ref.shape[1]
    C = SCAN_CHUNK
    nsub = R // C
    order = tuple(reversed(range(nsub))) if reverse else tuple(range(nsub))
    PW = LANE
    npair = RWKV_HEADS // 2
    strict2, incl2, ident2 = _pair_order_masks(C, reverse)
    first = _first_head(PW, RWKV_HEAD)
    first2 = _first_head(2 * PW, RWKV_HEAD)
    diag = _diag_blocks(PW, RWKV_HEAD, PW)
    inclb = _block_order_mask(R, C, reverse).astype(BF16)
    kap_t, r_t, k_t, b_t, k_e, b_e, e_tot, v_all = {}, {}, {}, {}, {}, {}, {}, {}
    for nb in range(NB):
        lw = lw_ref[nb]
        cum = _mm_01(inclb, lw)
        tot = _chunk_totals(lw, C)
        e_inv = jnp.exp(-cum)
        e_end = jnp.exp(tot - cum)
        e_tot[nb] = jnp.exp(tot)
        kap_t[nb] = kk_ref[nb].astype(F32) * jnp.exp(cum - lw)
        r_t[nb] = r_ref[nb].astype(F32) * jnp.exp(cum)
        k_all, b_all = k_ref[nb].astype(F32), b_ref[nb].astype(F32)
        k_t[nb], b_t[nb] = k_all * e_inv, b_all * e_inv
        k_e[nb], b_e[nb] = k_all * e_end, b_all * e_end
        v_all[nb] = v_ref[nb].astype(F32)
    sl = lambda pr: slice(pr * PW, (pr + 1) * PW)
    rows = lambda j: slice(j * C, (j + 1) * C)
    pick = lambda d, u: d[u[1]][rows(u[0]), sl(u[2])]
    chains = [(nb, pr) for nb in range(NB) for pr in range(npair)]
    units = [(j,) + ch for j in order for ch in chains]
    vp = {u: pick(v_all, u) for u in units}
    G = {u: _mm_nt(jnp.concatenate([pick(kap_t, u), pick(r_t, u)], axis=0),
                   jnp.concatenate([_block_diag(pick(b_t, u), first), _block_diag(pick(k_t, u), first)], axis=0))
         for u in units}
    a_ab = {u: jnp.where(strict2, G[u][:C, :PW], 0.0) for u in units}
    a_ak = {u: jnp.where(strict2, G[u][:C, PW:], 0.0) for u in units}
    a_rb = {u: jnp.where(incl2, G[u][C:, :PW], 0.0) for u in units}
    a_rk = {u: jnp.where(incl2, G[u][C:, PW:], 0.0) for u in units}
    akv = {u: _mm(a_ak[u], _block_diag(vp[u], first)) for u in units}
    eye2 = jnp.where(ident2, 1.0, 0.0)
    t_inv = {u: eye2 - a_ab[u] for u in units}
    pw = {u: _mm(a_ab[u], _block_diag(a_ab[u], first)) for u in units}
    span = 2
    while span < C:
        span *= 2
        if span < C:
            prod = {u: _mm(jnp.concatenate([t_inv[u], pw[u]], axis=0), _block_diag(pw[u], first)) for u in units}
            t_inv = {u: t_inv[u] + prod[u][:C] for u in units}
            pw = {u: prod[u][C:] for u in units}
        else:
            t_inv = {u: t_inv[u] + _mm(t_inv[u], _block_diag(pw[u], first)) for u in units}
    x = {u: _mm(t_inv[u], _block_diag(jnp.concatenate([pick(kap_t, u), akv[u]], axis=1), first2)) for u in units}
    w_p = {u: x[u][:, :PW] for u in units}
    u_p = {u: x[u][:, PW:] for u in units}
    vu = {u: jnp.concatenate([vp[u], u_p[u]], axis=0) for u in units}
    bw = {u: jnp.where(diag, _mm_tn(pick(b_e, u), w_p[u]), 0.0) for u in units}
    nn = {u: jnp.where(diag, _mm_tn(jnp.concatenate([pick(k_e, u), -pick(b_e, u)], axis=0), vu[u]), 0.0) for u in units}
    rw = {u: pick(r_t, u) - _mm(a_rb[u], _block_diag(w_p[u], first)) for u in units}
    ylhs = {u: jnp.concatenate([a_rk[u], -a_rb[u], rw[u]], axis=1) for u in units}
    yrhs = {u: jnp.concatenate([_block_diag(vp[u], first), _block_diag(u_p[u], first)], axis=0) for u in units}
    dec = {u: _row_vector_to_column(e_tot[u[1]][u[0] * C:u[0] * C + 1, sl(u[2])]) for u in units}
    hs = {ch: h_ref[ch[0], ch[1]] for ch in chains}
    ys = {}
    for j in order:
        for ch in chains:
            u = (j,) + ch
            ys[u] = _mm(ylhs[u], jnp.concatenate([yrhs[u], hs[ch]], axis=0))
        for ch in chains:
            u = (j,) + ch
            hs[ch] = dec[u] * hs[ch] - _mm(bw[u], hs[ch]) + nn[u]
    for ch in chains:
        h_ref[ch[0], ch[1]] = hs[ch]
    for nb in range(NB):
        if final:
            yprev = yprev_ref[nb]
            outs = []
            for pr in range(npair):
                y = jnp.concatenate([ys[(j, nb, pr)] for j in range(nsub)], axis=0) + yprev[:, sl(pr)]
                s_a = jnp.sum(jnp.where(first, y, 0.0), axis=-1, keepdims=True)
                s_b = jnp.sum(jnp.where(first, 0.0, y), axis=-1, keepdims=True)
                yc = y - jnp.where(first, s_a, s_b) * (1.0 / RWKV_HEAD)
                q = yc * yc
                v_a = jnp.sum(jnp.where(first, q, 0.0), axis=-1, keepdims=True)
                v_b = jnp.sum(jnp.where(first, 0.0, q), axis=-1, keepdims=True)
                outs.append(yc * lax.rsqrt(jnp.where(first, v_a, v_b) * (1.0 / RWKV_HEAD) + RWKV_GN_EPS))
            gn = gn_ref[...]
            o_ref[nb] = (jnp.concatenate(outs, axis=1) * gn[0:1] + gn[1:2]) * g_ref[nb].astype(F32) + gb_ref[nb].astype(F32)
        else:
            o_ref[nb] = jnp.concatenate(
                [jnp.concatenate([ys[(j, nb, pr)] for pr in range(npair)], axis=1) for j in range(nsub)], axis=0)


def _rwkv_scan(r, v, kk, lw, k, b, nzc, reverse, yprev=None, g=None, gb=None, gn=None):
    B, L, _ = r.shape
    C = SCAN_CHUNK * SCAN_SUB
    NB = _scan_batch(B)
    nc = L // C
    nzc = nzc // SCAN_SUB
    cidx = _chunk_index(reverse, nzc, nc)
    tok = pl.BlockSpec((NB, C, MIX_W), lambda b_, c: (b_, cidx(c), 0))
    final = yprev is not None
    ins = [r, v, kk, lw, k, b]
    specs = [tok] * 6
    if final:
        ins += [yprev, g, gb, gn]
        specs += [tok] * 3 + [_full(gn.shape)]
    return pl.pallas_call(
        functools.partial(_rwkv_scan_kernel, reverse, final),
        grid=(B // NB, nc),
        in_specs=specs,
        out_specs=tok,
        out_shape=jax.ShapeDtypeStruct((B, L, MIX_W), F32),
        scratch_shapes=[pltpu.VMEM((NB, RWKV_HEADS // 2, LANE, LANE), F32)],
        compiler_params=_params("parallel", "arbitrary"),
        name="rwkv_bwd" if reverse else "rwkv_fwd",
    )(*ins)


def _merge_kernel(stream, off, *refs):
    ns = len(stream.arrays)
    mod_ref, g_ref, ya_ref, yb_ref, yc_ref, wg_ref, wb_ref, wo_ref, o_ref = refs[ns:]
    m = mod_ref[...]
    s = stream.read(refs[:ns], off)
    h = _norm_mod(s, g_ref[...], m[0:1], m[1:2]).astype(BF16)
    gates = jnp.dot(h, wg_ref[...], preferred_element_type=F32)
    acc = None
    for i, y_ref in enumerate((ya_ref, yb_ref, yc_ref)):
        t = _sigmoid(gates[:, i * D_MODEL:(i + 1) * D_MODEL]) * _mm(y_ref[...], wb_ref[i])
        acc = t if acc is None else acc + t
    o_ref[...] = s + m[2:3] * _mm(acc, wo_ref[...])


def _merge(stream, l, mod, g, ya, yb, yc, wg, wb, wo, off):
    B, L = stream.batch, stream.rows
    TM = ROW_TILE
    nt = L // TM - off
    tok_in = lambda wd: pl.BlockSpec((None, TM, wd), lambda b, i: (b, i + off, 0))
    return pl.pallas_call(
        functools.partial(_merge_kernel, stream, off),
        grid=(B, nt),
        in_specs=stream.specs(off) + [_mod_spec(l, stream.nzt, B, off), _layer_spec(g, l),
                                      tok_in(MIX_W), tok_in(MIX_W), tok_in(MIX_W),
                                      _layer_spec(wg, l), _layer_spec(wb, l), _layer_spec(wo, l)],
        out_specs=pl.BlockSpec((None, TM, D_MODEL), lambda b, i: (b, i, 0)),
        out_shape=jax.ShapeDtypeStruct((B, nt * TM, D_MODEL), F32),
        compiler_params=_params("parallel", "parallel"),
        name="merge",
    )(*stream.arrays, mod, g, ya, yb, yc, wg, wb, wo)


def _ffn_kernel(final, s_ref, mod_ref, g_ref, w1_ref, w3_ref, w2_ref, gf_ref, o_ref):
    m = mod_ref[...]
    s = s_ref[...]
    h = _norm_mod(s, g_ref[...], m[3:4], m[4:5]).astype(BF16)
    a = jnp.dot(h, w1_ref[...], preferred_element_type=F32)
    b = jnp.dot(h, w3_ref[...], preferred_element_type=F32)
    o = s + m[5:6] * _mm(_silu(a) * b, w2_ref[...])
    if final:
        o = o * lax.rsqrt(jnp.mean(o * o, axis=-1, keepdims=True) + NORM_EPS) * gf_ref[...]
    o_ref[...] = o


def _ffn(s, l, mod, g, w1, w3, w2, gf, nzt, off, final):
    B, L, _ = s.shape
    TM = ROW_TILE
    nt = L // TM
    once = pl.Buffered(1)
    return pl.pallas_call(
        functools.partial(_ffn_kernel, final),
        grid=(B, nt),
        in_specs=[pl.BlockSpec((None, TM, D_MODEL), lambda b, i: (b, i, 0)), _mod_spec(l, nzt, B, off), _layer_spec(g, l),
                  _layer_spec(w1, l, pipeline_mode=once), _layer_spec(w3, l, pipeline_mode=once),
                  _layer_spec(w2, l, pipeline_mode=once), _full((1, D_MODEL))],
        out_specs=pl.BlockSpec((None, TM, D_MODEL), lambda b, i: (b, i, 0)),
        out_shape=jax.ShapeDtypeStruct((B, L, D_MODEL), F32),
        compiler_params=_params("parallel", "parallel"),
        name="ffn",
    )(s, mod, g, w1, w3, w2, gf)


def _rope_tables(T, lz):
    rows = T // GRID_W
    row = jnp.repeat(jnp.arange(rows), GRID_W).astype(F32)
    col = jnp.tile(jnp.arange(GRID_W), rows).astype(F32)
    inv = ROPE_BASE ** (-jnp.arange(ROPE_FREQS, dtype=F32) / ROPE_FREQS)
    ar, ac = row[:, None] * inv, col[:, None] * inv
    cos = jnp.concatenate([jnp.cos(ar), jnp.cos(ar), jnp.cos(ac), jnp.cos(ac)], axis=1)
    sin = jnp.concatenate([-jnp.sin(ar), jnp.sin(ar), -jnp.sin(ac), jnp.sin(ac)], axis=1)
    reps = LANE // SWA_HEAD_DIM
    x_tab = jnp.concatenate([jnp.tile(cos, (1, reps)), jnp.tile(sin, (1, reps))], axis=1)
    z_tab = jnp.concatenate([jnp.ones((lz, LANE), F32), jnp.zeros((lz, LANE), F32)], axis=1)
    return jnp.concatenate([z_tab, x_tab], axis=0)


def _partner_cols(w):
    return w.reshape(w.shape[:-1] + (-1, 2, ROPE_FREQS))[..., ::-1, :].reshape(w.shape)


def _mixer_weight(w_in):
    g0, g1, g2 = np.cumsum(GROUP_SPLIT)[:3]
    w_in = w_in.astype(BF16)
    gla, swa, rwkv = w_in[..., :g0], w_in[..., g0:g1], w_in[..., g1:g2]
    nk = GLA_HEADS * GLA_DK
    a = 2 * nk + MIX_W
    gla_p = jnp.concatenate([gla[..., :a], gla[..., a + 2 * GLA_GATE_RANK:], gla[..., a:a + 2 * GLA_GATE_RANK],
                             jnp.zeros(w_in.shape[:-1] + (LANE - 2 * GLA_GATE_RANK,), BF16)], axis=-1)
    qw = SWA_Q_HEADS * SWA_HEAD_DIM
    kvw = SWA_KV_HEADS * SWA_HEAD_DIM
    q, k, v = swa[..., :qw], swa[..., qw:qw + kvw], swa[..., qw + kvw:]
    w = jnp.concatenate([gla_p, q, _partner_cols(q), k, _partner_cols(k), v, rwkv], axis=-1)
    return w, w_in[..., g2:]


def _pad_rows(w, top, total):
    return jnp.concatenate([jnp.zeros((top, w.shape[1]), F32), w, jnp.zeros((total - top - w.shape[0], w.shape[1]), F32)], axis=0)


def kernel(x, c, ctx, c_ctx, w_ada, b_ada, g_mix, g_ffn, w_in, w_gk2, b_gk, g_gla, sink, mu_shift, w0, w_w2, a0, w_a2,
           w_g2, k_k, k_a, r_k, gn_w, gn_b, w_branch, w_out, w_ffn1, w_ffn3, w_ffn2, g_final):
    B, T, D = x.shape
    lz = ctx.shape[1]
    depth = w_in.shape[0]
    assert D == D_MODEL and lz % ROW_TILE == 0 and T % ROW_TILE == 0 and T % GRID_W == 0
    nzt, nzc, nzb = lz // ROW_TILE, lz // SCAN_CHUNK, lz // SWA_BLOCK
    cs = _rope_tables(T, lz)
    c8 = jnp.concatenate([c, c_ctx[None], jnp.zeros((8 - (B + 1) % 8 if (B + 1) % 8 else 0, D), F32)], axis=0)
    hidx = np.arange(MIX_W) // RWKV_HEAD
    seg = jnp.asarray(hidx[:, None] == hidx[None, :], BF16)
    mod = _ada(c8, w_ada, b_ada).reshape(depth, c8.shape[0], 6, D)
    w_mix, w_gate = _mixer_weight(w_in)
    w_branch, w_out = w_branch.astype(BF16), w_out.astype(BF16)
    w_ffn1, w_ffn3, w_ffn2 = w_ffn1.astype(BF16), w_ffn3.astype(BF16), w_ffn2.astype(BF16)
    g_mix, g_ffn = g_mix[:, None], g_ffn[:, None]
    stream = _Stream((ctx, x), nzt)
    out = None
    for l in range(depth):
        last = l == depth - 1
        p_gla, p_q, p_kv, p_rwkv = _inproj(stream, l, mod, g_mix, w_mix)

        wgk = [_pad_rows(w_gk2[l, d], d * GLA_GATE_RANK, LANE).astype(BF16) for d in range(2)]
        o_f = _gla_scan(p_gla, wgk[0], b_gk[l, 0][None], nzc, False)
        ya = _gla_scan(p_gla, wgk[1], b_gk[l, 1][None], nzc, True, o_f, g_gla[l][None])

        yb = _swa(p_q, p_kv, cs, sink[l], nzb)

        zero = jnp.zeros((MIX_W,), F32)
        mu4 = jnp.concatenate([mu_shift[l], jnp.zeros((4 * MIX_W - RWKV_W,), F32)]).reshape(4, MIX_W)
        vec = jnp.stack([w0[l, 0], w0[l, 1], a0[l, 0], a0[l, 1], k_k[l], k_a[l], r_k[l].reshape(-1), zero], axis=0)
        vec = jnp.concatenate([vec, mu4, jnp.zeros((4, MIX_W), F32)], axis=0)
        ww = jnp.stack([_pad_rows(w_w2[l, d], 0, LANE) for d in range(2)]).astype(BF16)
        wa = jnp.stack([_pad_rows(w_a2[l, d], RWKV_DECAY_RANK, LANE) for d in range(2)]).astype(BF16)
        f_r, f_v, f_kk, f_g, f_gb, lw_f, k_f, b_f, lw_b, k_b, b_b = _rwkv_feat(
            p_rwkv, vec, ww, wa, w_g2[l].astype(BF16), seg, nzt)
        y_f = _rwkv_scan(f_r, f_v, f_kk, lw_f, k_f, b_f, nzc, False)
        yc = _rwkv_scan(f_r, f_v, f_kk, lw_b, k_b, b_b, nzc, True, y_f, f_g, f_gb, jnp.stack([gn_w[l], gn_b[l]]))

        off = nzt if last else 0
        s_mix = _merge(stream, l, mod, g_mix, ya, yb, yc, w_gate, w_branch, w_out, off)
        out = _ffn(s_mix, l, mod, g_ffn, w_ffn1, w_ffn3, w_ffn2, g_final[None], nzt, off, last)
        stream = _Stream((out,), nzt)
    return out
```

```python
import functools

import jax
import jax.numpy as jnp
import numpy as np
from jax import lax
from jax.experimental import pallas as pl
from jax.experimental.pallas import tpu as pltpu

F32 = jnp.float32
BF16 = jnp.bfloat16
HI = lax.Precision.HIGHEST
LOG2E = 1.4426950408889634
DECAY_SCALE = 0.6065306597126334

D_MODEL = 1024
GRID_W = 64
NORM_EPS = 1e-6
NEG_INF = -1e30
MIX_W = D_MODEL // 2
N_BRANCH = 3
GLA_HEADS = 4
GLA_DV = MIX_W // GLA_HEADS
GLA_DK = GLA_DV // 2
GLA_GATE_RANK = 16
GLA_GATE_NORM = 16.0
SWA_HEAD_DIM = 64
SWA_Q_HEADS = MIX_W // SWA_HEAD_DIM
SWA_KV_HEADS = SWA_Q_HEADS // 4
SWA_GROUP = SWA_Q_HEADS // SWA_KV_HEADS
SWA_WINDOW = 128
SWA_BLOCK = 128
ROPE_FREQS = SWA_HEAD_DIM // 4
ROPE_BASE = 10000.0
RWKV_HEAD = 64
RWKV_HEADS = MIX_W // RWKV_HEAD
RWKV_DECAY_RANK = 64
RWKV_A_RANK = 64
RWKV_GATE_RANK = 128
RWKV_GN_EPS = 64e-5
FFN_HIDDEN = -(-8 * D_MODEL // (3 * 256)) * 256
GLA_SPLIT = (GLA_HEADS * GLA_DK, GLA_HEADS * GLA_DK, MIX_W, GLA_GATE_RANK, GLA_GATE_RANK, MIX_W)
SWA_SPLIT = (SWA_Q_HEADS * SWA_HEAD_DIM, SWA_KV_HEADS * SWA_HEAD_DIM, SWA_KV_HEADS * SWA_HEAD_DIM)
RWKV_SPLIT = (MIX_W, MIX_W, MIX_W, RWKV_DECAY_RANK, RWKV_A_RANK, RWKV_GATE_RANK)
GROUP_SPLIT = (sum(GLA_SPLIT), sum(SWA_SPLIT), sum(RWKV_SPLIT), N_BRANCH * D_MODEL)

LANE = 128
SUBLANE = 8
ROW_TILE = 256
HALO = 16
SCAN_CHUNK = 64
SCAN_SUB = 2
GLA_SUB = 4
GLA_SPLIT_EXP_LIMIT = 80.0
PAIR_HEAD = LANE // 2
assert GLA_DK == PAIR_HEAD and RWKV_HEAD == PAIR_HEAD and SCAN_CHUNK == PAIR_HEAD
GLA_W = 2 * GLA_HEADS * GLA_DK + 2 * MIX_W + LANE
SWA_Q_W = 1024
SWA_KV_W = 384
RWKV_W = sum(RWKV_SPLIT)
MIXER_W = GLA_W + SWA_Q_W + SWA_KV_W + RWKV_W
VMEM_LIMIT = 56 * 1024 * 1024


def _mm(a, b):
    return jnp.dot(a.astype(BF16), b.astype(BF16), preferred_element_type=F32)


def _mm_nt(a, b):
    return lax.dot_general(a.astype(BF16), b.astype(BF16), (((1,), (1,)), ((), ())), preferred_element_type=F32)


def _mm_tn(a, b):
    return lax.dot_general(a.astype(BF16), b.astype(BF16), (((0,), (0,)), ((), ())), preferred_element_type=F32)


def _mm_hi(a, b):
    return jnp.dot(a, b, precision=HI, preferred_element_type=F32)


def _sigmoid(x):
    return 1.0 / (1.0 + jnp.exp(-x))


def _silu(x):
    return x * _sigmoid(x)


def _softplus(x):
    return jnp.maximum(x, 0.0) + jnp.log(1.0 + jnp.exp(-jnp.abs(x)))


def _norm_mod(x, g, shift, scale):
    y = x * lax.rsqrt(jnp.mean(x * x, axis=-1, keepdims=True) + NORM_EPS)
    return (y * g) * (1.0 + scale) + shift


def _segsum(x, e):
    hi = x.astype(BF16)
    lo = (x - hi.astype(F32)).astype(BF16)
    return jnp.dot(hi, e, preferred_element_type=F32) + jnp.dot(lo, e, preferred_element_type=F32)


def _params(*sem):
    return pltpu.CompilerParams(dimension_semantics=sem, vmem_limit_bytes=VMEM_LIMIT)


def _full(shape):
    nd = len(shape)
    return pl.BlockSpec(shape, lambda *_: (0,) * nd)


def _ada_kernel(c_ref, w_ref, b_ref, o_ref):
    o_ref[...] = _mm_hi(_silu(c_ref[...]), w_ref[...]) + b_ref[...]


def _ada(c8, w, b):
    depth, _, n = w.shape
    tn = 1536
    return pl.pallas_call(
        _ada_kernel,
        grid=(depth, n // tn),
        in_specs=[_full(c8.shape), pl.BlockSpec((None, D_MODEL, tn), lambda l, j: (l, 0, j)),
                  pl.BlockSpec((None, 1, tn), lambda l, j: (l, 0, j))],
        out_specs=pl.BlockSpec((None, c8.shape[0], tn), lambda l, j: (l, 0, j)),
        out_shape=jax.ShapeDtypeStruct((depth, c8.shape[0], n), F32),
        compiler_params=_params("arbitrary", "arbitrary"),
        name="ada",
    )(c8, w, b.reshape(depth, 1, n))


class _Stream:
    def __init__(self, arrays, nzt):
        self.arrays, self.nzt = tuple(arrays), nzt
        self.split = len(self.arrays) == 2
        self.batch = self.arrays[0].shape[0]
        self.rows = sum(a.shape[1] for a in self.arrays)

    def specs(self, off=0):
        TM, D, nzt = ROW_TILE, D_MODEL, self.nzt
        if not self.split:
            return [pl.BlockSpec((None, TM, D), lambda b, i: (b, i + off, 0))]
        return [pl.BlockSpec((None, TM, D), lambda b, i: (b, jnp.minimum(i + off, nzt - 1), 0)),
                pl.BlockSpec((None, TM, D), lambda b, i: (b, jnp.maximum(i + off - nzt, 0), 0))]

    def read(self, refs, off=0):
        if not self.split:
            return refs[0][...]
        return jnp.where(pl.program_id(1) + off < self.nzt, refs[0][...], refs[1][...])


def _layer_spec(w, l, **kw):
    nd = w.ndim - 1
    return pl.BlockSpec((None,) + w.shape[1:], lambda *_: (l,) + (0,) * nd, **kw)


def _mod_spec(l, nzt, nb, off=0):
    return pl.BlockSpec((None, None, 6, D_MODEL), lambda b, i: (l, jnp.where(i + off < nzt, nb, b), 0, 0))


def _inproj_kernel(stream, *refs):
    ns = len(stream.arrays)
    mod_ref, g_ref, w_ref, o_gla, o_q, o_kv, o_rwkv = refs[ns:]
    m = mod_ref[...]
    h = _norm_mod(stream.read(refs[:ns]), g_ref[...], m[0:1], m[1:2]).astype(BF16)
    p = jnp.dot(h, w_ref[...], preferred_element_type=F32)
    o0 = 0
    for o_ref in (o_gla, o_q, o_kv, o_rwkv):
        wdt = o_ref.shape[-1]
        o_ref[...] = p[:, o0:o0 + wdt].astype(o_ref.dtype)
        o0 += wdt


def _inproj(stream, l, mod, g, w):
    B, L = stream.batch, stream.rows
    widths = (GLA_W, SWA_Q_W, SWA_KV_W, RWKV_W)
    return pl.pallas_call(
        functools.partial(_inproj_kernel, stream),
        grid=(B, L // ROW_TILE),
        in_specs=stream.specs() + [_mod_spec(l, stream.nzt, B), _layer_spec(g, l), _layer_spec(w, l)],
        out_specs=[pl.BlockSpec((None, ROW_TILE, wd), lambda b, i: (b, i, 0)) for wd in widths],
        out_shape=[jax.ShapeDtypeStruct((B, L, wd), dt) for wd, dt in zip(widths, (F32, F32, F32, BF16))],
        compiler_params=_params("parallel", "parallel"),
        name="inproj",
    )(*stream.arrays, mod, g, w)


def _chunk_index(reverse, nzc, nc):
    if not reverse:
        return lambda c: c
    return lambda c: jnp.where(c < nzc, nzc - 1 - c, nzc + nc - 1 - c)


def _order_masks(n, reverse):
    ii = lax.broadcasted_iota(jnp.int32, (n, n), 0)
    jj = lax.broadcasted_iota(jnp.int32, (n, n), 1)
    strict = (jj > ii) if reverse else (jj < ii)
    return strict, strict | (ii == jj), ii == jj


def _block_order_mask(n, chunk, reverse):
    ii = lax.broadcasted_iota(jnp.int32, (n, n), 0)
    jj = lax.broadcasted_iota(jnp.int32, (n, n), 1)
    return ((ii // chunk) == (jj // chunk)) & ((jj >= ii) if reverse else (jj <= ii))


def _mm_01(m01, x):
    hi = x.astype(BF16)
    r1 = x - hi.astype(F32)
    mid = r1.astype(BF16)
    lo = (r1 - mid.astype(F32)).astype(BF16)
    dot = lambda piece: jnp.dot(m01, piece, preferred_element_type=F32)
    return dot(hi) + dot(mid) + dot(lo)


def _chunk_row(x, chunk, r):
    parts = [jnp.broadcast_to(x[j + r:j + r + 1], (chunk, x.shape[1])) for j in range(0, x.shape[0], chunk)]
    return parts[0] if len(parts) == 1 else jnp.concatenate(parts, axis=0)


def _chunk_totals(x, chunk):
    parts = [jnp.broadcast_to(jnp.sum(x[j:j + chunk], axis=0, keepdims=True), (chunk, x.shape[1]))
             for j in range(0, x.shape[0], chunk)]
    return parts[0] if len(parts) == 1 else jnp.concatenate(parts, axis=0)


def _pair_order_masks(n, reverse):
    ii = lax.broadcasted_iota(jnp.int32, (n, LANE), 0)
    jj = lax.broadcasted_iota(jnp.int32, (n, LANE), 1) & (PAIR_HEAD - 1)
    strict = (jj > ii) if reverse else (jj < ii)
    return strict, strict | (ii == jj), ii == jj


def _first_head(width, head_w):
    return (lax.broadcasted_iota(jnp.int32, (1, width), 1) & head_w) == 0


def _block_diag(z, first):
    return jnp.concatenate([jnp.where(first, z, 0.0), jnp.where(first, 0.0, z)], axis=0)


def _diag_blocks(rows, head_w_cols, cols):
    ri = lax.broadcasted_iota(jnp.int32, (rows, cols), 0) // (rows // 2)
    ci = lax.broadcasted_iota(jnp.int32, (rows, cols), 1) // head_w_cols
    return ri == ci


def _row_vector_to_column(row):
    n = row.shape[1]
    eye = lax.broadcasted_iota(jnp.int32, (n, n), 0) == lax.broadcasted_iota(jnp.int32, (n, n), 1)
    return jnp.sum(jnp.where(eye, row, 0.0), axis=1, keepdims=True)


def _gla_kernel(reverse, final, *refs):
    if final:
        p_ref, wgk_ref, bgk_ref, oprev_ref, ggla_ref, o_ref, h_ref, att_ref, cum_ref = refs
    else:
        p_ref, wgk_ref, bgk_ref, o_ref, h_ref, att_ref, cum_ref = refs

    @pl.when(pl.program_id(1) == 0)
    def _():
        h_ref[...] = jnp.zeros_like(h_ref)

    NB, R = p_ref.shape[0], p_ref.shape[1]
    C = SCAN_CHUNK
    nsub = R // C
    order = tuple(reversed(range(nsub))) if reverse else tuple(range(nsub))
    nk = GLA_HEADS * GLA_DK
    npair = GLA_HEADS // 2
    kw, vw = 2 * GLA_DK, 2 * GLA_DV
    incl2 = _pair_order_masks(C, reverse)[1]
    first_k = _first_head(kw, GLA_DK)
    first_v = _first_head(vw, GLA_DV)
    diag = _diag_blocks(kw, GLA_DV, vw)
    inclb = _block_order_mask(R, C, reverse).astype(BF16)
    q_s, q_t, q_m, k_t, k_e, e_tot, v, og = {}, {}, {}, {}, {}, {}, {}, {}
    span = None
    for nb in range(NB):
        p = p_ref[nb]
        q = p[:, 0:nk] * GLA_DK ** -0.5
        k = p[:, nk:2 * nk]
        v[nb] = p[:, 2 * nk:2 * nk + MIX_W]
        og[nb] = p[:, 2 * nk + MIX_W:2 * nk + 2 * MIX_W]
        gk = p[:, 2 * nk + 2 * MIX_W:]
        lg = -_softplus(-(_mm(gk, wgk_ref[...]) + bgk_ref[...])) / GLA_GATE_NORM
        cum = _mm_01(inclb, lg)
        tot = _chunk_totals(lg, C)
        mid = _chunk_row(cum, C, C // 2)
        rel = cum - mid
        part = jnp.max(jnp.abs(rel))
        span = part if span is None else jnp.maximum(span, part)
        cum_ref[nb] = cum
        q_s[nb] = q
        q_t[nb] = q * jnp.exp(cum)
        q_m[nb] = q * jnp.exp(rel)
        k_t[nb] = k * jnp.exp(-rel)
        k_e[nb] = k * jnp.exp(tot - cum)
        e_tot[nb] = jnp.exp(tot)
    ksl = lambda pr: slice(pr * kw, (pr + 1) * kw)
    vsl = lambda pr: slice(pr * vw, (pr + 1) * vw)
    rows = lambda j: slice(j * C, (j + 1) * C)
    chains = [(nb, pr) for nb in range(NB) for pr in range(npair)]
    units = [(j, nb, pr) for j in order for (nb, pr) in chains]
    qp = {(j, nb, pr): q_t[nb][rows(j), ksl(pr)] for (j, nb, pr) in units}
    vp = {(j, nb, pr): v[nb][rows(j), vsl(pr)] for (j, nb, pr) in units}
    safe = span <= GLA_SPLIT_EXP_LIMIT

    @pl.when(safe)
    def _():
        for idx, (j, nb, pr) in enumerate(units):
            att_ref[idx] = jnp.where(incl2, _mm_nt(q_m[nb][rows(j), ksl(pr)],
                                                   _block_diag(k_t[nb][rows(j), ksl(pr)], first_k)), 0.0)

    @pl.when(jnp.logical_not(safe))
    def _():
        tok = lax.broadcasted_iota(jnp.int32, (C, 1), 0)
        lane = lax.broadcasted_iota(jnp.int32, (1, kw), 1)
        for idx, (j, nb, pr) in enumerate(units):
            q_i = q_s[nb][rows(j), ksl(pr)]
            c_i = cum_ref[nb, rows(j), ksl(pr)]

            def columns(t8, acc, j=j, nb=nb, pr=pr, q_i=q_i, c_i=c_i):
                r0 = pl.multiple_of(j * C + t8 * SUBLANE, SUBLANE)
                k_rows = p_ref[nb, pl.ds(r0, SUBLANE), nk + pr * kw:nk + (pr + 1) * kw]
                c_rows = cum_ref[nb, pl.ds(r0, SUBLANE), ksl(pr)]
                for r in range(SUBLANE):
                    t = t8 * SUBLANE + r
                    x = q_i * k_rows[r:r + 1] * jnp.exp(jnp.minimum(c_i - c_rows[r:r + 1], 0.0))
                    s_a = jnp.sum(jnp.where(first_k, x, 0.0), axis=1, keepdims=True)
                    s_b = jnp.sum(jnp.where(first_k, 0.0, x), axis=1, keepdims=True)
                    seen = (tok <= t) if reverse else (tok >= t)
                    acc = jnp.where(lane == t, jnp.where(seen, s_a, 0.0), acc)
                    acc = jnp.where(lane == t + GLA_DK, jnp.where(seen, s_b, 0.0), acc)
                return acc

            att_ref[idx] = lax.fori_loop(0, C // SUBLANE, columns, jnp.zeros((C, kw), F32))

    att = {u: att_ref[idx] for idx, u in enumerate(units)}
    kv = {(j, nb, pr): jnp.where(diag, _mm_tn(k_e[nb][rows(j), ksl(pr)], vp[(j, nb, pr)]), 0.0) for (j, nb, pr) in units}
    dec = {(j, nb, pr): _row_vector_to_column(e_tot[nb][j * C:j * C + 1, ksl(pr)]) for (j, nb, pr) in units}
    lhs = {u: jnp.concatenate([att[u], qp[u]], axis=1) for u in units}
    bdv = {u: _block_diag(vp[u], first_v) for u in units}
    hs = {ch: h_ref[ch[0], ch[1]] for ch in chains}
    outs = {}
    for j in order:
        for ch in chains:
            u = (j,) + ch
            outs[u] = _mm(lhs[u], jnp.concatenate([bdv[u], hs[ch]], axis=0))
        for ch in chains:
            u = (j,) + ch
            hs[ch] = dec[u] * hs[ch] + kv[u]
    for ch in chains:
        h_ref[ch[0], ch[1]] = hs[ch]
    for nb in range(NB):
        o = jnp.concatenate([jnp.concatenate([outs[(j, nb, pr)] for pr in range(npair)], axis=1) for j in range(nsub)], axis=0)
        if final:
            o = o + oprev_ref[nb]
            g = ggla_ref[...]
            ys = []
            for h in range(GLA_HEADS):
                oh = o[:, h * GLA_DV:(h + 1) * GLA_DV]
                ys.append(oh * lax.rsqrt(jnp.mean(oh * oh, axis=-1, keepdims=True) + NORM_EPS) * g)
            o = jnp.concatenate(ys, axis=1) * _silu(og[nb])
        o_ref[nb] = o


def _scan_batch(B):
    return 2 if B % 2 == 0 else 1


def _gla_scan(p, wgk, bgk, nzc, reverse, oprev=None, ggla=None):
    B, L, _ = p.shape
    C = SCAN_CHUNK * GLA_SUB
    NB = _scan_batch(B)
    nc = L // C
    nzc = nzc // GLA_SUB
    cidx = _chunk_index(reverse, nzc, nc)
    tok = lambda wd: pl.BlockSpec((NB, C, wd), lambda b, c: (b, cidx(c), 0))
    final = oprev is not None
    ins = [p, wgk, bgk]
    specs = [tok(GLA_W), _full(wgk.shape), _full(bgk.shape)]
    if final:
        ins += [oprev, ggla]
        specs += [tok(MIX_W), _full(ggla.shape)]
    return pl.pallas_call(
        functools.partial(_gla_kernel, reverse, final),
        grid=(B // NB, nc),
        in_specs=specs,
        out_specs=tok(MIX_W),
        out_shape=jax.ShapeDtypeStruct((B, L, MIX_W), F32),
        scratch_shapes=[pltpu.VMEM((NB, GLA_HEADS // 2, 2 * GLA_DK, 2 * GLA_DV), F32),
                        pltpu.VMEM((GLA_SUB * NB * (GLA_HEADS // 2), SCAN_CHUNK, 2 * GLA_DK), F32),
                        pltpu.VMEM((NB, C, GLA_HEADS * GLA_DK), F32)],
        compiler_params=_params("parallel", "arbitrary"),
        name="gla_bwd" if reverse else "gla_fwd",
    )(*ins)


def _swa_kernel(nzb, q_ref, kv_ref, cs_ref, sink_ref, o_ref):
    n = pl.program_id(1)
    nblk = pl.num_programs(1)
    BL = SWA_BLOCK
    kvw = SWA_KV_HEADS * SWA_HEAD_DIM
    qw = SWA_Q_HEADS * SWA_HEAD_DIM
    nloc = 3 * BL
    lz = nzb * BL

    def rope(x, xs, cs, reps):
        cos = jnp.concatenate([cs[:, :LANE]] * reps, axis=1)
        sin = jnp.concatenate([cs[:, LANE:]] * reps, axis=1)
        return x * cos + xs * sin

    qq = q_ref[...]
    q0 = pl.multiple_of(n * BL, BL)
    q = rope(qq[:, :qw], qq[:, qw:], cs_ref[pl.ds(q0, BL), :], qw // LANE) * (SWA_HEAD_DIM ** -0.5 * LOG2E)
    k0 = pl.multiple_of(jnp.clip((n - 1) * BL, 0, (nblk - 3) * BL), BL)
    kv = kv_ref[pl.ds(k0, nloc), :]
    kvz = kv_ref[0:lz, :]
    keys = jnp.concatenate([rope(kv[:, :kvw], kv[:, kvw:2 * kvw], cs_ref[pl.ds(k0, nloc), :], kvw // LANE),
                            kvz[:, :kvw]], axis=0)
    vals = jnp.concatenate([kv[:, 2 * kvw:], kvz[:, 2 * kvw:]], axis=0)
    nkeys = nloc + lz
    qpos = n * BL + lax.broadcasted_iota(jnp.int32, (BL, nkeys), 0)
    col = lax.broadcasted_iota(jnp.int32, (BL, nkeys), 1)
    kpos = k0 + col
    lo = jnp.where(n >= nzb, lz, nblk * BL)
    band = (jnp.abs(kpos - qpos) <= SWA_WINDOW) & (kpos >= lo)
    bias = jnp.where(band | (col >= nloc), 0.0, NEG_INF)
    heads = range(SWA_Q_HEADS)
    kvh = lambda h: slice((h // SWA_GROUP) * SWA_HEAD_DIM, (h // SWA_GROUP + 1) * SWA_HEAD_DIM)
    s = [_mm_nt(q[:, h * SWA_HEAD_DIM:(h + 1) * SWA_HEAD_DIM], keys[:, kvh(h)]) + bias for h in heads]
    sk = [sink_ref[h] * LOG2E for h in heads]
    m = [jnp.maximum(jnp.max(s[h], axis=-1, keepdims=True), sk[h]) for h in heads]
    e = [jnp.exp2(s[h] - m[h]) for h in heads]
    den = [jnp.sum(e[h], axis=-1, keepdims=True) + jnp.exp2(sk[h] - m[h]) for h in heads]
    outs = [_mm(e[h], vals[:, kvh(h)]) / den[h] for h in heads]
    o_ref[...] = jnp.concatenate(outs, axis=1)


def _swa(pq, pkv, cs, sink, nzb):
    B, L, _ = pq.shape
    BL = SWA_BLOCK
    nblk = L // BL
    assert nblk >= 3
    return pl.pallas_call(
        functools.partial(_swa_kernel, nzb),
        grid=(B, nblk),
        in_specs=[pl.BlockSpec((None, BL, SWA_Q_W), lambda b, n: (b, n, 0)),
                  pl.BlockSpec((None, L, SWA_KV_W), lambda b, n: (b, 0, 0)),
                  _full(cs.shape),
                  pl.BlockSpec(memory_space=pltpu.SMEM)],
        out_specs=pl.BlockSpec((None, BL, MIX_W), lambda b, n: (b, n, 0)),
        out_shape=jax.ShapeDtypeStruct((B, L, MIX_W), F32),
        compiler_params=_params("parallel", "parallel"),
        name="swa",
    )(pq, pkv, cs, sink)


def _rwkv_feat_kernel(nzt, p_ref, pp_ref, pn_ref, vec_ref, ww_ref, wa_ref, wg_ref, e_ref,
                      o_r, o_v, o_kk, o_g, o_gb, o_lwf, o_kf, o_bf, o_lwb, o_kb, o_bb):
    i = pl.program_id(1)
    n = pl.num_programs(1)
    p = p_ref[...].astype(F32)
    TM = p.shape[0]
    first = (i == 0) | (i == nzt)
    last = (i == nzt - 1) | (i == n - 1)
    prow = jnp.where(first, 0.0, pp_ref[HALO - 1:HALO, :].astype(F32))
    nrow = jnp.where(last, 0.0, pn_ref[0:1, :].astype(F32))
    rid = lax.broadcasted_iota(jnp.int32, (TM, 1), 0)
    up = jnp.where(rid == 0, prow, pltpu.roll(p, 1, 0))
    dn = jnp.where(rid == TM - 1, nrow, pltpu.roll(p, TM - 1, 0))
    vec = vec_ref[...]
    mu = jnp.concatenate([vec[8 + j:9 + j] for j in range(4)], axis=1)[:, :RWKV_W]
    pm = p * (1.0 - mu) + (0.5 * mu) * (up + dn)
    W = MIX_W
    r, k, v = pm[:, :W], pm[:, W:2 * W], pm[:, 2 * W:3 * W]
    wa = pm[:, 3 * W:3 * W + LANE]
    gl = pm[:, 3 * W + LANE:]
    e = e_ref[...]
    k_k, k_a, r_k = vec[4:5], vec[5:6], vec[6:7]
    g = _mm(_sigmoid(gl), wg_ref[...])
    kkn = k * k_k
    kk = kkn * lax.rsqrt(jnp.maximum(_segsum(kkn * kkn, e), 1e-24))
    twa = jnp.tanh(wa)
    ksum = None
    for d, (o_lw, o_k, o_b) in enumerate(((o_lwf, o_kf, o_bf), (o_lwb, o_kb, o_bb))):
        a = _sigmoid(vec[2 + d:3 + d] + _mm(wa, wa_ref[d]))
        kd = k * (1.0 + (a - 1.0) * k_a)
        o_lw[...] = -DECAY_SCALE * _sigmoid(vec[d:d + 1] + _mm(twa, ww_ref[d]))
        o_k[...] = kd.astype(o_k.dtype)
        o_b[...] = (kk * a).astype(o_b.dtype)
        ksum = kd if ksum is None else ksum + kd
    o_r[...] = r.astype(o_r.dtype)
    o_v[...] = v.astype(o_v.dtype)
    o_kk[...] = kk.astype(o_kk.dtype)
    o_g[...] = g.astype(o_g.dtype)
    o_gb[...] = (_segsum(r * ksum * r_k, e) * v * g).astype(o_gb.dtype)


def _rwkv_feat(p, vec, ww, wa, wg, e, nzt):
    B, L, _ = p.shape
    TM = ROW_TILE
    rh = TM // HALO
    tok = pl.BlockSpec((None, TM, RWKV_W), lambda b, i: (b, i, 0))
    halo_p = pl.BlockSpec((None, HALO, RWKV_W), lambda b, i: (b, jnp.maximum(i * rh - 1, 0), 0))
    halo_n = pl.BlockSpec((None, HALO, RWKV_W), lambda b, i: (b, jnp.minimum((i + 1) * rh, L // HALO - 1), 0))
    out = pl.BlockSpec((None, TM, MIX_W), lambda b, i: (b, i, 0))
    dts = [BF16] * 5 + [F32, BF16, BF16] * 2
    return pl.pallas_call(
        functools.partial(_rwkv_feat_kernel, nzt),
        grid=(B, L // TM),
        in_specs=[tok, halo_p, halo_n, _full(vec.shape), _full(ww.shape), _full(wa.shape), _full(wg.shape), _full(e.shape)],
        out_specs=[out] * 11,
        out_shape=[jax.ShapeDtypeStruct((B, L, MIX_W), dt) for dt in dts],
        compiler_params=_params("parallel", "parallel"),
        name="rwkv_feat",
    )(p, p, p, vec, ww, wa, wg, e)


def _rwkv_scan_kernel(reverse, final, *refs):
    if final:
        r_ref, v_ref, kk_ref, lw_ref, k_ref, b_ref, yprev_ref, g_ref, gb_ref, gn_ref, o_ref, h_ref = refs
    else:
        r_ref, v_ref, kk_ref, lw_ref, k_ref, b_ref, o_ref, h_ref = refs

    @pl.when(pl.program_id(1) == 0)
    def _():
        h_ref[...] = jnp.zeros_like(h_ref)

    NB, R = lw_ref.shape[0], lw_ref.shape[1]
    C = SCAN_CHUNK
    nsub = R // C
    order = tuple(reversed(range(nsub))) if reverse else tuple(range(nsub))
    PW = LANE
    npair = RWKV_HEADS // 2
    strict2, incl2, ident2 = _pair_order_masks(C, reverse)
    first = _first_head(PW, RWKV_HEAD)
    first2 = _first_head(2 * PW, RWKV_HEAD)
    diag = _diag_blocks(PW, RWKV_HEAD, PW)
    inclb = _block_order_mask(R, C, reverse).astype(BF16)
    kap_t, r_t, k_t, b_t, k_e, b_e, e_tot, v_all = {}, {}, {}, {}, {}, {}, {}, {}
    for nb in range(NB):
        lw = lw_ref[nb]
        cum = _mm_01(inclb, lw)
        tot = _chunk_totals(lw, C)
        e_inv = jnp.exp(-cum)
        e_end = jnp.exp(tot - cum)
        e_tot[nb] = jnp.exp(tot)
        kap_t[nb] = kk_ref[nb].astype(F32) * jnp.exp(cum - lw)
        r_t[nb] = r_ref[nb].astype(F32) * jnp.exp(cum)
        k_all, b_all = k_ref[nb].astype(F32), b_ref[nb].astype(F32)
        k_t[nb], b_t[nb] = k_all * e_inv, b_all * e_inv
        k_e[nb], b_e[nb] = k_all * e_end, b_all * e_end
        v_all[nb] = v_ref[nb].astype(F32)
    sl = lambda pr: slice(pr * PW, (pr + 1) * PW)
    rows = lambda j: slice(j * C, (j + 1) * C)
    pick = lambda d, u: d[u[1]][rows(u[0]), sl(u[2])]
    chains = [(nb, pr) for nb in range(NB) for pr in range(npair)]
    units = [(j,) + ch for j in order for ch in chains]
    vp = {u: pick(v_all, u) for u in units}
    G = {u: _mm_nt(jnp.concatenate([pick(kap_t, u), pick(r_t, u)], axis=0),
                   jnp.concatenate([_block_diag(pick(b_t, u), first), _block_diag(pick(k_t, u), first)], axis=0))
         for u in units}
    a_ab = {u: jnp.where(strict2, G[u][:C, :PW], 0.0) for u in units}
    a_ak = {u: jnp.where(strict2, G[u][:C, PW:], 0.0) for u in units}
    a_rb = {u: jnp.where(incl2, G[u][C:, :PW], 0.0) for u in units}
    a_rk = {u: jnp.where(incl2, G[u][C:, PW:], 0.0) for u in units}
    akv = {u: _mm(a_ak[u], _block_diag(vp[u], first)) for u in units}
    eye2 = jnp.where(ident2, 1.0, 0.0)
    t_inv = {u: eye2 - a_ab[u] for u in units}
    pw = {u: _mm(a_ab[u], _block_diag(a_ab[u], first)) for u in units}
    span = 2
    while span < C:
        span *= 2
        if span < C:
            prod = {u: _mm(jnp.concatenate([t_inv[u], pw[u]], axis=0), _block_diag(pw[u], first)) for u in units}
            t_inv = {u: t_inv[u] + prod[u][:C] for u in units}
            pw = {u: prod[u][C:] for u in units}
        else:
            t_inv = {u: t_inv[u] + _mm(t_inv[u], _block_diag(pw[u], first)) for u in units}
    x = {u: _mm(t_inv[u], _block_diag(jnp.concatenate([pick(kap_t, u), akv[u]], axis=1), first2)) for u in units}
    w_p = {u: x[u][:, :PW] for u in units}
    u_p = {u: x[u][:, PW:] for u in units}
    vu = {u: jnp.concatenate([vp[u], u_p[u]], axis=0) for u in units}
    bw = {u: jnp.where(diag, _mm_tn(pick(b_e, u), w_p[u]), 0.0) for u in units}
    nn = {u: jnp.where(diag, _mm_tn(jnp.concatenate([pick(k_e, u), -pick(b_e, u)], axis=0), vu[u]), 0.0) for u in units}
    rw = {u: pick(r_t, u) - _mm(a_rb[u], _block_diag(w_p[u], first)) for u in units}
    ylhs = {u: jnp.concatenate([a_rk[u], -a_rb[u], rw[u]], axis=1) for u in units}
    yrhs = {u: jnp.concatenate([_block_diag(vp[u], first), _block_diag(u_p[u], first)], axis=0) for u in units}
    dec = {u: _row_vector_to_column(e_tot[u[1]][u[0] * C:u[0] * C + 1, sl(u[2])]) for u in units}
    hs = {ch: h_ref[ch[0], ch[1]] for ch in chains}
    ys = {}
    for j in order:
        for ch in chains:
            u = (j,) + ch
            ys[u] = _mm(ylhs[u], jnp.concatenate([yrhs[u], hs[ch]], axis=0))
        for ch in chains:
            u = (j,) + ch
            hs[ch] = dec[u] * hs[ch] - _mm(bw[u], hs[ch]) + nn[u]
    for ch in chains:
        h_ref[ch[0], ch[1]] = hs[ch]
    for nb in range(NB):
        if final:
            yprev = yprev_ref[nb]
            outs = []
            for pr in range(npair):
                y = jnp.concatenate([ys[(j, nb, pr)] for j in range(nsub)], axis=0) + yprev[:, sl(pr)]
                s_a = jnp.sum(jnp.where(first, y, 0.0), axis=-1, keepdims=True)
                s_b = jnp.sum(jnp.where(first, 0.0, y), axis=-1, keepdims=True)
                yc = y - jnp.where(first, s_a, s_b) * (1.0 / RWKV_HEAD)
                q = yc * yc
                v_a = jnp.sum(jnp.where(first, q, 0.0), axis=-1, keepdims=True)
                v_b = jnp.sum(jnp.where(first, 0.0, q), axis=-1, keepdims=True)
                outs.append(yc * lax.rsqrt(jnp.where(first, v_a, v_b) * (1.0 / RWKV_HEAD) + RWKV_GN_EPS))
            gn = gn_ref[...]
            o_ref[nb] = (jnp.concatenate(outs, axis=1) * gn[0:1] + gn[1:2]) * g_ref[nb].astype(F32) + gb_ref[nb].astype(F32)
        else:
            o_ref[nb] = jnp.concatenate(
                [jnp.concatenate([ys[(j, nb, pr)] for pr in range(npair)], axis=1) for j in range(nsub)], axis=0)


def _rwkv_scan(r, v, kk, lw, k, b, nzc, reverse, yprev=None, g=None, gb=None, gn=None):
    B, L, _ = r.shape
    C = SCAN_CHUNK * SCAN_SUB
    NB = _scan_batch(B)
    nc = L // C
    nzc = nzc // SCAN_SUB
    cidx = _chunk_index(reverse, nzc, nc)
    tok = pl.BlockSpec((NB, C, MIX_W), lambda b_, c: (b_, cidx(c), 0))
    final = yprev is not None
    ins = [r, v, kk, lw, k, b]
    specs = [tok] * 6
    if final:
        ins += [yprev, g, gb, gn]
        specs += [tok] * 3 + [_full(gn.shape)]
    return pl.pallas_call(
        functools.partial(_rwkv_scan_kernel, reverse, final),
        grid=(B // NB, nc),
        in_specs=specs,
        out_specs=tok,
        out_shape=jax.ShapeDtypeStruct((B, L, MIX_W), F32),
        scratch_shapes=[pltpu.VMEM((NB, RWKV_HEADS // 2, LANE, LANE), F32)],
        compiler_params=_params("parallel", "arbitrary"),
        name="rwkv_bwd" if reverse else "rwkv_fwd",
    )(*ins)


def _merge_kernel(stream, off, *refs):
    ns = len(stream.arrays)
    mod_ref, g_ref, ya_ref, yb_ref, yc_ref, wg_ref, wb_ref, wo_ref, o_ref = refs[ns:]
    m = mod_ref[...]
    s = stream.read(refs[:ns], off)
    h = _norm_mod(s, g_ref[...], m[0:1], m[1:2]).astype(BF16)
    gates = jnp.dot(h, wg_ref[...], preferred_element_type=F32)
    acc = None
    for i, y_ref in enumerate((ya_ref, yb_ref, yc_ref)):
        t = _sigmoid(gates[:, i * D_MODEL:(i + 1) * D_MODEL]) * _mm(y_ref[...], wb_ref[i])
        acc = t if acc is None else acc + t
    o_ref[...] = s + m[2:3] * _mm(acc, wo_ref[...])


def _merge(stream, l, mod, g, ya, yb, yc, wg, wb, wo, off):
    B, L = stream.batch, stream.rows
    TM = ROW_TILE
    nt = L // TM - off
    tok_in = lambda wd: pl.BlockSpec((None, TM, wd), lambda b, i: (b, i + off, 0))
    return pl.pallas_call(
        functools.partial(_merge_kernel, stream, off),
        grid=(B, nt),
        in_specs=stream.specs(off) + [_mod_spec(l, stream.nzt, B, off), _layer_spec(g, l),
                                      tok_in(MIX_W), tok_in(MIX_W), tok_in(MIX_W),
                                      _layer_spec(wg, l), _layer_spec(wb, l), _layer_spec(wo, l)],
        out_specs=pl.BlockSpec((None, TM, D_MODEL), lambda b, i: (b, i, 0)),
        out_shape=jax.ShapeDtypeStruct((B, nt * TM, D_MODEL), F32),
        compiler_params=_params("parallel", "parallel"),
        name="merge",
    )(*stream.arrays, mod, g, ya, yb, yc, wg, wb, wo)


def _ffn_kernel(final, s_ref, mod_ref, g_ref, w1_ref, w3_ref, w2_ref, gf_ref, o_ref):
    m = mod_ref[...]
    s = s_ref[...]
    h = _norm_mod(s, g_ref[...], m[3:4], m[4:5]).astype(BF16)
    a = jnp.dot(h, w1_ref[...], preferred_element_type=F32)
    b = jnp.dot(h, w3_ref[...], preferred_element_type=F32)
    o = s + m[5:6] * _mm(_silu(a) * b, w2_ref[...])
    if final:
        o = o * lax.rsqrt(jnp.mean(o * o, axis=-1, keepdims=True) + NORM_EPS) * gf_ref[...]
    o_ref[...] = o


def _ffn(s, l, mod, g, w1, w3, w2, gf, nzt, off, final):
    B, L, _ = s.shape
    TM = ROW_TILE
    nt = L // TM
    once = pl.Buffered(1)
    return pl.pallas_call(
        functools.partial(_ffn_kernel, final),
        grid=(B, nt),
        in_specs=[pl.BlockSpec((None, TM, D_MODEL), lambda b, i: (b, i, 0)), _mod_spec(l, nzt, B, off), _layer_spec(g, l),
                  _layer_spec(w1, l, pipeline_mode=once), _layer_spec(w3, l, pipeline_mode=once),
                  _layer_spec(w2, l, pipeline_mode=once), _full((1, D_MODEL))],
        out_specs=pl.BlockSpec((None, TM, D_MODEL), lambda b, i: (b, i, 0)),
        out_shape=jax.ShapeDtypeStruct((B, L, D_MODEL), F32),
        compiler_params=_params("parallel", "parallel"),
        name="ffn",
    )(s, mod, g, w1, w3, w2, gf)


def _rope_tables(T, lz):
    rows = T // GRID_W
    row = jnp.repeat(jnp.arange(rows), GRID_W).astype(F32)
    col = jnp.tile(jnp.arange(GRID_W), rows).astype(F32)
    inv = ROPE_BASE ** (-jnp.arange(ROPE_FREQS, dtype=F32) / ROPE_FREQS)
    ar, ac = row[:, None] * inv, col[:, None] * inv
    cos = jnp.concatenate([jnp.cos(ar), jnp.cos(ar), jnp.cos(ac), jnp.cos(ac)], axis=1)
    sin = jnp.concatenate([-jnp.sin(ar), jnp.sin(ar), -jnp.sin(ac), jnp.sin(ac)], axis=1)
    reps = LANE // SWA_HEAD_DIM
    x_tab = jnp.concatenate([jnp.tile(cos, (1, reps)), jnp.tile(sin, (1, reps))], axis=1)
    z_tab = jnp.concatenate([jnp.ones((lz, LANE), F32), jnp.zeros((lz, LANE), F32)], axis=1)
    return jnp.concatenate([z_tab, x_tab], axis=0)


def _partner_cols(w):
    return w.reshape(w.shape[:-1] + (-1, 2, ROPE_FREQS))[..., ::-1, :].reshape(w.shape)


def _mixer_weight(w_in):
    g0, g1, g2 = np.cumsum(GROUP_SPLIT)[:3]
    w_in = w_in.astype(BF16)
    gla, swa, rwkv = w_in[..., :g0], w_in[..., g0:g1], w_in[..., g1:g2]
    nk = GLA_HEADS * GLA_DK
    a = 2 * nk + MIX_W
    gla_p = jnp.concatenate([gla[..., :a], gla[..., a + 2 * GLA_GATE_RANK:], gla[..., a:a + 2 * GLA_GATE_RANK],
                             jnp.zeros(w_in.shape[:-1] + (LANE - 2 * GLA_GATE_RANK,), BF16)], axis=-1)
    qw = SWA_Q_HEADS * SWA_HEAD_DIM
    kvw = SWA_KV_HEADS * SWA_HEAD_DIM
    q, k, v = swa[..., :qw], swa[..., qw:qw + kvw], swa[..., qw + kvw:]
    w = jnp.concatenate([gla_p, q, _partner_cols(q), k, _partner_cols(k), v, rwkv], axis=-1)
    return w, w_in[..., g2:]


def _pad_rows(w, top, total):
    return jnp.concatenate([jnp.zeros((top, w.shape[1]), F32), w, jnp.zeros((total - top - w.shape[0], w.shape[1]), F32)], axis=0)


def kernel(x, c, ctx, c_ctx, w_ada, b_ada, g_mix, g_ffn, w_in, w_gk2, b_gk, g_gla, sink, mu_shift, w0, w_w2, a0, w_a2,
           w_g2, k_k, k_a, r_k, gn_w, gn_b, w_branch, w_out, w_ffn1, w_ffn3, w_ffn2, g_final):
    B, T, D = x.shape
    lz = ctx.shape[1]
    depth = w_in.shape[0]
    assert D == D_MODEL and lz % ROW_TILE == 0 and T % ROW_TILE == 0 and T % GRID_W == 0
    nzt, nzc, nzb = lz // ROW_TILE, lz // SCAN_CHUNK, lz // SWA_BLOCK
    cs = _rope_tables(T, lz)
    c8 = jnp.concatenate([c, c_ctx[None], jnp.zeros((8 - (B + 1) % 8 if (B + 1) % 8 else 0, D), F32)], axis=0)
    hidx = np.arange(MIX_W) // RWKV_HEAD
    seg = jnp.asarray(hidx[:, None] == hidx[None, :], BF16)
    mod = _ada(c8, w_ada, b_ada).reshape(depth, c8.shape[0], 6, D)
    w_mix, w_gate = _mixer_weight(w_in)
    w_branch, w_out = w_branch.astype(BF16), w_out.astype(BF16)
    w_ffn1, w_ffn3, w_ffn2 = w_ffn1.astype(BF16), w_ffn3.astype(BF16), w_ffn2.astype(BF16)
    g_mix, g_ffn = g_mix[:, None], g_ffn[:, None]
    stream = _Stream((ctx, x), nzt)
    out = None
    for l in range(depth):
        last = l == depth - 1
        p_gla, p_q, p_kv, p_rwkv = _inproj(stream, l, mod, g_mix, w_mix)

        wgk = [_pad_rows(w_gk2[l, d], d * GLA_GATE_RANK, LANE).astype(BF16) for d in range(2)]
        o_f = _gla_scan(p_gla, wgk[0], b_gk[l, 0][None], nzc, False)
        ya = _gla_scan(p_gla, wgk[1], b_gk[l, 1][None], nzc, True, o_f, g_gla[l][None])

        yb = _swa(p_q, p_kv, cs, sink[l], nzb)

        zero = jnp.zeros((MIX_W,), F32)
        mu4 = jnp.concatenate([mu_shift[l], jnp.zeros((4 * MIX_W - RWKV_W,), F32)]).reshape(4, MIX_W)
        vec = jnp.stack([w0[l, 0], w0[l, 1], a0[l, 0], a0[l, 1], k_k[l], k_a[l], r_k[l].reshape(-1), zero], axis=0)
        vec = jnp.concatenate([vec, mu4, jnp.zeros((4, MIX_W), F32)], axis=0)
        ww = jnp.stack([_pad_rows(w_w2[l, d], 0, LANE) for d in range(2)]).astype(BF16)
        wa = jnp.stack([_pad_rows(w_a2[l, d], RWKV_DECAY_RANK, LANE) for d in range(2)]).astype(BF16)
        f_r, f_v, f_kk, f_g, f_gb, lw_f, k_f, b_f, lw_b, k_b, b_b = _rwkv_feat(
            p_rwkv, vec, ww, wa, w_g2[l].astype(BF16), seg, nzt)
        y_f = _rwkv_scan(f_r, f_v, f_kk, lw_f, k_f, b_f, nzc, False)
        yc = _rwkv_scan(f_r, f_v, f_kk, lw_b, k_b, b_b, nzc, True, y_f, f_g, f_gb, jnp.stack([gn_w[l], gn_b[l]]))

        off = nzt if last else 0
        s_mix = _merge(stream, l, mod, g_mix, ya, yb, yc, w_gate, w_branch, w_out, off)
        out = _ffn(s_mix, l, mod, g_ffn, w_ffn1, w_ffn3, w_ffn2, g_final[None], nzt, off, last)
        stream = _Stream((out,), nzt)
    return out
```

```python
import functools

import jax
import jax.numpy as jnp
import numpy as np
from jax import lax
from jax.experimental import pallas as pl
from jax.experimental.pallas import tpu as pltpu

F32 = jnp.float32
BF16 = jnp.bfloat16
HI = lax.Precision.HIGHEST
LOG2E = 1.4426950408889634
DECAY_SCALE = 0.6065306597126334

D_MODEL = 1024
GRID_W = 64
NORM_EPS = 1e-6
NEG_INF = -1e30
MIX_W = D_MODEL // 2
N_BRANCH = 3
GLA_HEADS = 4
GLA_DV = MIX_W // GLA_HEADS
GLA_DK = GLA_DV // 2
GLA_GATE_RANK = 16
GLA_GATE_NORM = 16.0
SWA_HEAD_DIM = 64
SWA_Q_HEADS = MIX_W // SWA_HEAD_DIM
SWA_KV_HEADS = SWA_Q_HEADS // 4
SWA_GROUP = SWA_Q_HEADS // SWA_KV_HEADS
SWA_WINDOW = 128
SWA_BLOCK = 128
ROPE_FREQS = SWA_HEAD_DIM // 4
ROPE_BASE = 10000.0
RWKV_HEAD = 64
RWKV_HEADS = MIX_W // RWKV_HEAD
RWKV_DECAY_RANK = 64
RWKV_A_RANK = 64
RWKV_GATE_RANK = 128
RWKV_GN_EPS = 64e-5
FFN_HIDDEN = -(-8 * D_MODEL // (3 * 256)) * 256
GLA_SPLIT = (GLA_HEADS * GLA_DK, GLA_HEADS * GLA_DK, MIX_W, GLA_GATE_RANK, GLA_GATE_RANK, MIX_W)
SWA_SPLIT = (SWA_Q_HEADS * SWA_HEAD_DIM, SWA_KV_HEADS * SWA_HEAD_DIM, SWA_KV_HEADS * SWA_HEAD_DIM)
RWKV_SPLIT = (MIX_W, MIX_W, MIX_W, RWKV_DECAY_RANK, RWKV_A_RANK, RWKV_GATE_RANK)
GROUP_SPLIT = (sum(GLA_SPLIT), sum(SWA_SPLIT), sum(RWKV_SPLIT), N_BRANCH * D_MODEL)

LANE = 128
SUBLANE = 8
ROW_TILE = 256
HALO = 16
SCAN_CHUNK = 64
SCAN_SUB = 2
GLA_SUB = 4
RWKV_SCAN_BATCH = 4
GLA_SPLIT_EXP_LIMIT = 80.0
PAIR_HEAD = LANE // 2
assert GLA_DK == PAIR_HEAD and RWKV_HEAD == PAIR_HEAD and SCAN_CHUNK == PAIR_HEAD
GLA_W = 2 * GLA_HEADS * GLA_DK + 2 * MIX_W + LANE
SWA_Q_W = 1024
SWA_KV_W = 384
RWKV_W = sum(RWKV_SPLIT)
MIXER_W = GLA_W + SWA_Q_W + SWA_KV_W + RWKV_W
VMEM_LIMIT = 56 * 1024 * 1024


def _mm(a, b):
    return jnp.dot(a.astype(BF16), b.astype(BF16), preferred_element_type=F32)


def _mm_nt(a, b):
    return lax.dot_general(a.astype(BF16), b.astype(BF16), (((1,), (1,)), ((), ())), preferred_element_type=F32)


def _mm_tn(a, b):
    return lax.dot_general(a.astype(BF16), b.astype(BF16), (((0,), (0,)), ((), ())), preferred_element_type=F32)


def _mm_hi(a, b):
    return jnp.dot(a, b, precision=HI, preferred_element_type=F32)


def _sigmoid(x):
    return 1.0 / (1.0 + jnp.exp(-x))


def _silu(x):
    return x * _sigmoid(x)


def _softplus(x):
    return jnp.maximum(x, 0.0) + jnp.log(1.0 + jnp.exp(-jnp.abs(x)))


def _norm_mod(x, g, shift, scale):
    y = x * lax.rsqrt(jnp.mean(x * x, axis=-1, keepdims=True) + NORM_EPS)
    return (y * g) * (1.0 + scale) + shift


def _segsum(x, e):
    hi = x.astype(BF16)
    lo = (x - hi.astype(F32)).astype(BF16)
    return jnp.dot(hi, e, preferred_element_type=F32) + jnp.dot(lo, e, preferred_element_type=F32)


def _params(*sem):
    return pltpu.CompilerParams(dimension_semantics=sem, vmem_limit_bytes=VMEM_LIMIT)


def _full(shape):
    nd = len(shape)
    return pl.BlockSpec(shape, lambda *_: (0,) * nd)


def _ada_kernel(c_ref, w_ref, b_ref, o_ref):
    o_ref[...] = _mm_hi(_silu(c_ref[...]), w_ref[...]) + b_ref[...]


def _ada(c8, w, b):
    depth, _, n = w.shape
    tn = 1536
    return pl.pallas_call(
        _ada_kernel,
        grid=(depth, n // tn),
        in_specs=[_full(c8.shape), pl.BlockSpec((None, D_MODEL, tn), lambda l, j: (l, 0, j)),
                  pl.BlockSpec((None, 1, tn), lambda l, j: (l, 0, j))],
        out_specs=pl.BlockSpec((None, c8.shape[0], tn), lambda l, j: (l, 0, j)),
        out_shape=jax.ShapeDtypeStruct((depth, c8.shape[0], n), F32),
        compiler_params=_params("arbitrary", "arbitrary"),
        name="ada",
    )(c8, w, b.reshape(depth, 1, n))


class _Stream:
    def __init__(self, arrays, nzt):
        self.arrays, self.nzt = tuple(arrays), nzt
        self.split = len(self.arrays) == 2
        self.batch = self.arrays[0].shape[0]
        self.rows = sum(a.shape[1] for a in self.arrays)

    def specs(self, off=0):
        TM, D, nzt = ROW_TILE, D_MODEL, self.nzt
        if not self.split:
            return [pl.BlockSpec((None, TM, D), lambda b, i: (b, i + off, 0))]
        return [pl.BlockSpec((None, TM, D), lambda b, i: (b, jnp.minimum(i + off, nzt - 1), 0)),
                pl.BlockSpec((None, TM, D), lambda b, i: (b, jnp.maximum(i + off - nzt, 0), 0))]

    def read(self, refs, off=0):
        if not self.split:
            return refs[0][...]
        return jnp.where(pl.program_id(1) + off < self.nzt, refs[0][...], refs[1][...])


def _layer_spec(w, l, **kw):
    nd = w.ndim - 1
    return pl.BlockSpec((None,) + w.shape[1:], lambda *_: (l,) + (0,) * nd, **kw)


def _mod_spec(l, nzt, nb, off=0):
    return pl.BlockSpec((None, None, 6, D_MODEL), lambda b, i: (l, jnp.where(i + off < nzt, nb, b), 0, 0))


def _inproj_kernel(stream, *refs):
    ns = len(stream.arrays)
    mod_ref, g_ref, w_ref, o_gla, o_q, o_kv, o_rwkv = refs[ns:]
    m = mod_ref[...]
    h = _norm_mod(stream.read(refs[:ns]), g_ref[...], m[0:1], m[1:2]).astype(BF16)
    p = jnp.dot(h, w_ref[...], preferred_element_type=F32)
    o0 = 0
    for o_ref in (o_gla, o_q, o_kv, o_rwkv):
        wdt = o_ref.shape[-1]
        o_ref[...] = p[:, o0:o0 + wdt].astype(o_ref.dtype)
        o0 += wdt


def _inproj(stream, l, mod, g, w):
    B, L = stream.batch, stream.rows
    widths = (GLA_W, SWA_Q_W, SWA_KV_W, RWKV_W)
    return pl.pallas_call(
        functools.partial(_inproj_kernel, stream),
        grid=(B, L // ROW_TILE),
        in_specs=stream.specs() + [_mod_spec(l, stream.nzt, B), _layer_spec(g, l), _layer_spec(w, l)],
        out_specs=[pl.BlockSpec((None, ROW_TILE, wd), lambda b, i: (b, i, 0)) for wd in widths],
        out_shape=[jax.ShapeDtypeStruct((B, L, wd), dt) for wd, dt in zip(widths, (F32, F32, F32, BF16))],
        compiler_params=_params("parallel", "parallel"),
        name="inproj",
    )(*stream.arrays, mod, g, w)


def _chunk_index(reverse, nzc, nc):
    if not reverse:
        return lambda c: c
    return lambda c: jnp.where(c < nzc, nzc - 1 - c, nzc + nc - 1 - c)


def _order_masks(n, reverse):
    ii = lax.broadcasted_iota(jnp.int32, (n, n), 0)
    jj = lax.broadcasted_iota(jnp.int32, (n, n), 1)
    strict = (jj > ii) if reverse else (jj < ii)
    return strict, strict | (ii == jj), ii == jj


def _block_order_mask(n, chunk, reverse):
    ii = lax.broadcasted_iota(jnp.int32, (n, n), 0)
    jj = lax.broadcasted_iota(jnp.int32, (n, n), 1)
    return ((ii // chunk) == (jj // chunk)) & ((jj >= ii) if reverse else (jj <= ii))


def _mm_01(m01, x):
    hi = x.astype(BF16)
    r1 = x - hi.astype(F32)
    mid = r1.astype(BF16)
    lo = (r1 - mid.astype(F32)).astype(BF16)
    dot = lambda piece: jnp.dot(m01, piece, preferred_element_type=F32)
    return dot(hi) + dot(mid) + dot(lo)


def _chunk_row(x, chunk, r):
    parts = [jnp.broadcast_to(x[j + r:j + r + 1], (chunk, x.shape[1])) for j in range(0, x.shape[0], chunk)]
    return parts[0] if len(parts) == 1 else jnp.concatenate(parts, axis=0)


def _chunk_totals(x, chunk):
    parts = [jnp.broadcast_to(jnp.sum(x[j:j + chunk], axis=0, keepdims=True), (chunk, x.shape[1]))
             for j in range(0, x.shape[0], chunk)]
    return parts[0] if len(parts) == 1 else jnp.concatenate(parts, axis=0)


def _pair_order_masks(n, reverse):
    ii = lax.broadcasted_iota(jnp.int32, (n, LANE), 0)
    jj = lax.broadcasted_iota(jnp.int32, (n, LANE), 1) & (PAIR_HEAD - 1)
    strict = (jj > ii) if reverse else (jj < ii)
    return strict, strict | (ii == jj), ii == jj


def _first_head(width, head_w):
    return (lax.broadcasted_iota(jnp.int32, (1, width), 1) & head_w) == 0


def _block_diag(z, first):
    return jnp.concatenate([jnp.where(first, z, 0.0), jnp.where(first, 0.0, z)], axis=0)


def _diag_blocks(rows, head_w_cols, cols):
    ri = lax.broadcasted_iota(jnp.int32, (rows, cols), 0) // (rows // 2)
    ci = lax.broadcasted_iota(jnp.int32, (rows, cols), 1) // head_w_cols
    return ri == ci


def _row_vector_to_column(row):
    n = row.shape[1]
    eye = lax.broadcasted_iota(jnp.int32, (n, n), 0) == lax.broadcasted_iota(jnp.int32, (n, n), 1)
    return jnp.sum(jnp.where(eye, row, 0.0), axis=1, keepdims=True)


def _gla_kernel(reverse, final, *refs):
    if final:
        p_ref, wgk_ref, bgk_ref, oprev_ref, ggla_ref, o_ref, h_ref, att_ref, cum_ref = refs
    else:
        p_ref, wgk_ref, bgk_ref, o_ref, h_ref, att_ref, cum_ref = refs

    @pl.when(pl.program_id(1) == 0)
    def _():
        h_ref[...] = jnp.zeros_like(h_ref)

    NB, R = p_ref.shape[0], p_ref.shape[1]
    C = SCAN_CHUNK
    nsub = R // C
    order = tuple(reversed(range(nsub))) if reverse else tuple(range(nsub))
    nk = GLA_HEADS * GLA_DK
    npair = GLA_HEADS // 2
    kw, vw = 2 * GLA_DK, 2 * GLA_DV
    incl2 = _pair_order_masks(C, reverse)[1]
    first_k = _first_head(kw, GLA_DK)
    first_v = _first_head(vw, GLA_DV)
    diag = _diag_blocks(kw, GLA_DV, vw)
    inclb = _block_order_mask(R, C, reverse).astype(BF16)
    q_s, q_t, q_m, k_t, k_e, e_tot, v, og = {}, {}, {}, {}, {}, {}, {}, {}
    span = None
    for nb in range(NB):
        p = p_ref[nb]
        q = p[:, 0:nk] * GLA_DK ** -0.5
        k = p[:, nk:2 * nk]
        v[nb] = p[:, 2 * nk:2 * nk + MIX_W]
        og[nb] = p[:, 2 * nk + MIX_W:2 * nk + 2 * MIX_W]
        gk = p[:, 2 * nk + 2 * MIX_W:]
        lg = -_softplus(-(_mm(gk, wgk_ref[...]) + bgk_ref[...])) / GLA_GATE_NORM
        cum = _mm_01(inclb, lg)
        tot = _chunk_totals(lg, C)
        mid = _chunk_row(cum, C, C // 2)
        rel = cum - mid
        part = jnp.max(jnp.abs(rel))
        span = part if span is None else jnp.maximum(span, part)
        cum_ref[nb] = cum
        q_s[nb] = q
        q_t[nb] = q * jnp.exp(cum)
        q_m[nb] = q * jnp.exp(rel)
        k_t[nb] = k * jnp.exp(-rel)
        k_e[nb] = k * jnp.exp(tot - cum)
        e_tot[nb] = jnp.exp(tot)
    ksl = lambda pr: slice(pr * kw, (pr + 1) * kw)
    vsl = lambda pr: slice(pr * vw, (pr + 1) * vw)
    rows = lambda j: slice(j * C, (j + 1) * C)
    chains = [(nb, pr) for nb in range(NB) for pr in range(npair)]
    units = [(j, nb, pr) for j in order for (nb, pr) in chains]
    qp = {(j, nb, pr): q_t[nb][rows(j), ksl(pr)] for (j, nb, pr) in units}
    vp = {(j, nb, pr): v[nb][rows(j), vsl(pr)] for (j, nb, pr) in units}
    kv = {(j, nb, pr): jnp.where(diag, _mm_tn(k_e[nb][rows(j), ksl(pr)], vp[(j, nb, pr)]), 0.0) for (j, nb, pr) in units}
    dec = {(j, nb, pr): _row_vector_to_column(e_tot[nb][j * C:j * C + 1, ksl(pr)]) for (j, nb, pr) in units}
    bdv = {u: _block_diag(vp[u], first_v) for u in units}
    safe = span <= GLA_SPLIT_EXP_LIMIT

    @pl.when(safe)
    def _():
        for idx, (j, nb, pr) in enumerate(units):
            att_ref[idx] = jnp.where(incl2, _mm_nt(q_m[nb][rows(j), ksl(pr)],
                                                   _block_diag(k_t[nb][rows(j), ksl(pr)], first_k)), 0.0)

    @pl.when(jnp.logical_not(safe))
    def _():
        tok = lax.broadcasted_iota(jnp.int32, (C, 1), 0)
        lane = lax.broadcasted_iota(jnp.int32, (1, kw), 1)
        for idx, (j, nb, pr) in enumerate(units):
            q_i = q_s[nb][rows(j), ksl(pr)]
            c_i = cum_ref[nb, rows(j), ksl(pr)]

            def columns(t8, acc, j=j, nb=nb, pr=pr, q_i=q_i, c_i=c_i):
                r0 = pl.multiple_of(j * C + t8 * SUBLANE, SUBLANE)
                k_rows = p_ref[nb, pl.ds(r0, SUBLANE), nk + pr * kw:nk + (pr + 1) * kw]
                c_rows = cum_ref[nb, pl.ds(r0, SUBLANE), ksl(pr)]
                for r in range(SUBLANE):
                    t = t8 * SUBLANE + r
                    x = q_i * k_rows[r:r + 1] * jnp.exp(jnp.minimum(c_i - c_rows[r:r + 1], 0.0))
                    s_a = jnp.sum(jnp.where(first_k, x, 0.0), axis=1, keepdims=True)
                    s_b = jnp.sum(jnp.where(first_k, 0.0, x), axis=1, keepdims=True)
                    seen = (tok <= t) if reverse else (tok >= t)
                    acc = jnp.where(lane == t, jnp.where(seen, s_a, 0.0), acc)
                    acc = jnp.where(lane == t + GLA_DK, jnp.where(seen, s_b, 0.0), acc)
                return acc

            att_ref[idx] = lax.fori_loop(0, C // SUBLANE, columns, jnp.zeros((C, kw), F32))

    att = {u: att_ref[idx] for idx, u in enumerate(units)}
    lhs = {u: jnp.concatenate([att[u], qp[u]], axis=1) for u in units}
    hs = {ch: h_ref[ch[0], ch[1]] for ch in chains}
    outs = {}
    for j in order:
        for ch in chains:
            u = (j,) + ch
            outs[u] = _mm(lhs[u], jnp.concatenate([bdv[u], hs[ch]], axis=0))
        for ch in chains:
            u = (j,) + ch
            hs[ch] = dec[u] * hs[ch] + kv[u]
    for ch in chains:
        h_ref[ch[0], ch[1]] = hs[ch]
    for nb in range(NB):
        o = jnp.concatenate([jnp.concatenate([outs[(j, nb, pr)] for pr in range(npair)], axis=1) for j in range(nsub)], axis=0)
        if final:
            o = o + oprev_ref[nb]
            g = ggla_ref[...]
            ys = []
            for h in range(GLA_HEADS):
                oh = o[:, h * GLA_DV:(h + 1) * GLA_DV]
                ys.append(oh * lax.rsqrt(jnp.mean(oh * oh, axis=-1, keepdims=True) + NORM_EPS) * g)
            o = jnp.concatenate(ys, axis=1) * _silu(og[nb])
        o_ref[nb] = o


def _scan_batch(B, most=2):
    return max(n for n in (1, 2, 4) if n <= most and B % n == 0)


def _gla_scan(p, wgk, bgk, nzc, reverse, oprev=None, ggla=None):
    B, L, _ = p.shape
    C = SCAN_CHUNK * GLA_SUB
    NB = _scan_batch(B)
    nc = L // C
    nzc = nzc // GLA_SUB
    cidx = _chunk_index(reverse, nzc, nc)
    tok = lambda wd: pl.BlockSpec((NB, C, wd), lambda b, c: (b, cidx(c), 0))
    final = oprev is not None
    ins = [p, wgk, bgk]
    specs = [tok(GLA_W), _full(wgk.shape), _full(bgk.shape)]
    if final:
        ins += [oprev, ggla]
        specs += [tok(MIX_W), _full(ggla.shape)]
    return pl.pallas_call(
        functools.partial(_gla_kernel, reverse, final),
        grid=(B // NB, nc),
        in_specs=specs,
        out_specs=tok(MIX_W),
        out_shape=jax.ShapeDtypeStruct((B, L, MIX_W), F32),
        scratch_shapes=[pltpu.VMEM((NB, GLA_HEADS // 2, 2 * GLA_DK, 2 * GLA_DV), F32),
                        pltpu.VMEM((GLA_SUB * NB * (GLA_HEADS // 2), SCAN_CHUNK, 2 * GLA_DK), F32),
                        pltpu.VMEM((NB, C, GLA_HEADS * GLA_DK), F32)],
        compiler_params=_params("parallel", "arbitrary"),
        name="gla_bwd" if reverse else "gla_fwd",
    )(*ins)


def _swa_kernel(nzb, q_ref, kv_ref, cs_ref, sink_ref, o_ref):
    n = pl.program_id(1)
    nblk = pl.num_programs(1)
    BL = SWA_BLOCK
    kvw = SWA_KV_HEADS * SWA_HEAD_DIM
    qw = SWA_Q_HEADS * SWA_HEAD_DIM
    nloc = 3 * BL
    lz = nzb * BL

    def rope(x, xs, cs, reps):
        cos = jnp.concatenate([cs[:, :LANE]] * reps, axis=1)
        sin = jnp.concatenate([cs[:, LANE:]] * reps, axis=1)
        return x * cos + xs * sin

    qq = q_ref[...]
    q0 = pl.multiple_of(n * BL, BL)
    q = rope(qq[:, :qw], qq[:, qw:], cs_ref[pl.ds(q0, BL), :], qw // LANE) * (SWA_HEAD_DIM ** -0.5 * LOG2E)
    k0 = pl.multiple_of(jnp.clip((n - 1) * BL, 0, (nblk - 3) * BL), BL)
    kv = kv_ref[pl.ds(k0, nloc), :]
    kvz = kv_ref[0:lz, :]
    keys = jnp.concatenate([rope(kv[:, :kvw], kv[:, kvw:2 * kvw], cs_ref[pl.ds(k0, nloc), :], kvw // LANE),
                            kvz[:, :kvw]], axis=0)
    vals = jnp.concatenate([kv[:, 2 * kvw:], kvz[:, 2 * kvw:]], axis=0)
    nkeys = nloc + lz
    qpos = n * BL + lax.broadcasted_iota(jnp.int32, (BL, nkeys), 0)
    col = lax.broadcasted_iota(jnp.int32, (BL, nkeys), 1)
    kpos = k0 + col
    lo = jnp.where(n >= nzb, lz, nblk * BL)
    band = (jnp.abs(kpos - qpos) <= SWA_WINDOW) & (kpos >= lo)
    bias = jnp.where(band | (col >= nloc), 0.0, NEG_INF)
    heads = range(SWA_Q_HEADS)
    kvh = lambda h: slice((h // SWA_GROUP) * SWA_HEAD_DIM, (h // SWA_GROUP + 1) * SWA_HEAD_DIM)
    s = [_mm_nt(q[:, h * SWA_HEAD_DIM:(h + 1) * SWA_HEAD_DIM], keys[:, kvh(h)]) + bias for h in heads]
    sk = [sink_ref[h] * LOG2E for h in heads]
    m = [jnp.maximum(jnp.max(s[h], axis=-1, keepdims=True), sk[h]) for h in heads]
    e = [jnp.exp2(s[h] - m[h]) for h in heads]
    den = [jnp.sum(e[h], axis=-1, keepdims=True) + jnp.exp2(sk[h] - m[h]) for h in heads]
    outs = [_mm(e[h], vals[:, kvh(h)]) / den[h] for h in heads]
    o_ref[...] = jnp.concatenate(outs, axis=1)


def _swa(pq, pkv, cs, sink, nzb):
    B, L, _ = pq.shape
    BL = SWA_BLOCK
    nblk = L // BL
    assert nblk >= 3
    return pl.pallas_call(
        functools.partial(_swa_kernel, nzb),
        grid=(B, nblk),
        in_specs=[pl.BlockSpec((None, BL, SWA_Q_W), lambda b, n: (b, n, 0)),
                  pl.BlockSpec((None, L, SWA_KV_W), lambda b, n: (b, 0, 0)),
                  _full(cs.shape),
                  pl.BlockSpec(memory_space=pltpu.SMEM)],
        out_specs=pl.BlockSpec((None, BL, MIX_W), lambda b, n: (b, n, 0)),
        out_shape=jax.ShapeDtypeStruct((B, L, MIX_W), F32),
        compiler_params=_params("parallel", "parallel"),
        name="swa",
    )(pq, pkv, cs, sink)


def _rwkv_feat_kernel(nzt, p_ref, pp_ref, pn_ref, vec_ref, ww_ref, wa_ref, wg_ref, e_ref,
                      o_r, o_v, o_kk, o_g, o_gb, o_lwf, o_kf, o_bf, o_lwb, o_kb, o_bb):
    i = pl.program_id(1)
    n = pl.num_programs(1)
    p = p_ref[...].astype(F32)
    TM = p.shape[0]
    first = (i == 0) | (i == nzt)
    last = (i == nzt - 1) | (i == n - 1)
    prow = jnp.where(first, 0.0, pp_ref[HALO - 1:HALO, :].astype(F32))
    nrow = jnp.where(last, 0.0, pn_ref[0:1, :].astype(F32))
    rid = lax.broadcasted_iota(jnp.int32, (TM, 1), 0)
    up = jnp.where(rid == 0, prow, pltpu.roll(p, 1, 0))
    dn = jnp.where(rid == TM - 1, nrow, pltpu.roll(p, TM - 1, 0))
    vec = vec_ref[...]
    mu = jnp.concatenate([vec[8 + j:9 + j] for j in range(4)], axis=1)[:, :RWKV_W]
    pm = p * (1.0 - mu) + (0.5 * mu) * (up + dn)
    W = MIX_W
    r, k, v = pm[:, :W], pm[:, W:2 * W], pm[:, 2 * W:3 * W]
    wa = pm[:, 3 * W:3 * W + LANE]
    gl = pm[:, 3 * W + LANE:]
    e = e_ref[...]
    k_k, k_a, r_k = vec[4:5], vec[5:6], vec[6:7]
    g = _mm(_sigmoid(gl), wg_ref[...])
    kkn = k * k_k
    kk = kkn * lax.rsqrt(jnp.maximum(_segsum(kkn * kkn, e), 1e-24))
    twa = jnp.tanh(wa)
    ksum = None
    for d, (o_lw, o_k, o_b) in enumerate(((o_lwf, o_kf, o_bf), (o_lwb, o_kb, o_bb))):
        a = _sigmoid(vec[2 + d:3 + d] + _mm(wa, wa_ref[d]))
        kd = k * (1.0 + (a - 1.0) * k_a)
        o_lw[...] = -DECAY_SCALE * _sigmoid(vec[d:d + 1] + _mm(twa, ww_ref[d]))
        o_k[...] = kd.astype(o_k.dtype)
        o_b[...] = (kk * a).astype(o_b.dtype)
        ksum = kd if ksum is None else ksum + kd
    o_r[...] = r.astype(o_r.dtype)
    o_v[...] = v.astype(o_v.dtype)
    o_kk[...] = kk.astype(o_kk.dtype)
    o_g[...] = g.astype(o_g.dtype)
    o_gb[...] = (_segsum(r * ksum * r_k, e) * v * g).astype(o_gb.dtype)


def _rwkv_feat(p, vec, ww, wa, wg, e, nzt):
    B, L, _ = p.shape
    TM = ROW_TILE
    rh = TM // HALO
    tok = pl.BlockSpec((None, TM, RWKV_W), lambda b, i: (b, i, 0))
    halo_p = pl.BlockSpec((None, HALO, RWKV_W), lambda b, i: (b, jnp.maximum(i * rh - 1, 0), 0))
    halo_n = pl.BlockSpec((None, HALO, RWKV_W), lambda b, i: (b, jnp.minimum((i + 1) * rh, L // HALO - 1), 0))
    out = pl.BlockSpec((None, TM, MIX_W), lambda b, i: (b, i, 0))
    dts = [BF16] * 5 + [F32, BF16, BF16] * 2
    return pl.pallas_call(
        functools.partial(_rwkv_feat_kernel, nzt),
        grid=(B, L // TM),
        in_specs=[tok, halo_p, halo_n, _full(vec.shape), _full(ww.shape), _full(wa.shape), _full(wg.shape), _full(e.shape)],
        out_specs=[out] * 11,
        out_shape=[jax.ShapeDtypeStruct((B, L, MIX_W), dt) for dt in dts],
        compiler_params=_params("parallel", "parallel"),
        name="rwkv_feat",
    )(p, p, p, vec, ww, wa, wg, e)


def _rwkv_scan_kernel(reverse, final, *refs):
    if final:
        r_ref, v_ref, kk_ref, lw_ref, k_ref, b_ref, yprev_ref, g_ref, gb_ref, gn_ref, o_ref, h_ref = refs
    else:
        r_ref, v_ref, kk_ref, lw_ref, k_ref, b_ref, o_ref, h_ref = refs

    @pl.when(pl.program_id(1) == 0)
    def _():
        h_ref[...] = jnp.zeros_like(h_ref)

    NB, R = lw_ref.shape[0], lw_ref.shape[1]
    C = SCAN_CHUNK
    nsub = R // C
    order = tuple(reversed(range(nsub))) if reverse else tuple(range(nsub))
    PW = LANE
    npair = RWKV_HEADS // 2
    strict2, incl2, ident2 = _pair_order_masks(C, reverse)
    first = _first_head(PW, RWKV_HEAD)
    diag = _diag_blocks(PW, RWKV_HEAD, PW)
    inclb = _block_order_mask(R, C, reverse).astype(BF16)
    kap_t, r_t, k_t, b_t, k_e, b_e, e_tot, v_all = {}, {}, {}, {}, {}, {}, {}, {}
    for nb in range(NB):
        lw = lw_ref[nb]
        cum = _mm_01(inclb, lw)
        tot = _chunk_totals(lw, C)
        e_inv = jnp.exp(-cum)
        e_end = jnp.exp(tot - cum)
        e_tot[nb] = jnp.exp(tot)
        kap_t[nb] = kk_ref[nb].astype(F32) * jnp.exp(cum - lw)
        r_t[nb] = r_ref[nb].astype(F32) * jnp.exp(cum)
        k_all, b_all = k_ref[nb].astype(F32), b_ref[nb].astype(F32)
        k_t[nb], b_t[nb] = k_all * e_inv, b_all * e_inv
        k_e[nb], b_e[nb] = k_all * e_end, b_all * e_end
        v_all[nb] = v_ref[nb].astype(F32)
    sl = lambda pr: slice(pr * PW, (pr + 1) * PW)
    rows = lambda j: slice(j * C, (j + 1) * C)
    pick = lambda d, u: d[u[1]][rows(u[0]), sl(u[2])]
    chains = [(nb, pr) for nb in range(NB) for pr in range(npair)]
    units = [(j,) + ch for j in order for ch in chains]
    vp = {u: pick(v_all, u) for u in units}
    G = {u: _mm_nt(jnp.concatenate([pick(kap_t, u), pick(r_t, u)], axis=0),
                   jnp.concatenate([_block_diag(pick(b_t, u), first), _block_diag(pick(k_t, u), first)], axis=0))
         for u in units}
    a_ab = {u: jnp.where(strict2, G[u][:C, :PW], 0.0) for u in units}
    a_ak = {u: jnp.where(strict2, G[u][:C, PW:], 0.0) for u in units}
    a_rb = {u: jnp.where(incl2, G[u][C:, :PW], 0.0) for u in units}
    a_rk = {u: jnp.where(incl2, G[u][C:, PW:], 0.0) for u in units}
    eye2 = jnp.where(ident2, 1.0, 0.0)
    t_inv = {u: eye2 - a_ab[u] for u in units}
    pw = {u: _mm(a_ab[u], _block_diag(a_ab[u], first)) for u in units}
    span = 2
    while span < C:
        span *= 2
        if span < C:
            prod = {u: _mm(jnp.concatenate([t_inv[u], pw[u]], axis=0), _block_diag(pw[u], first)) for u in units}
            t_inv = {u: t_inv[u] + prod[u][:C] for u in units}
            pw = {u: prod[u][C:] for u in units}
        else:
            t_inv = {u: t_inv[u] + _mm(t_inv[u], _block_diag(pw[u], first)) for u in units}
    bdv = {u: _block_diag(vp[u], first) for u in units}
    ulhs = {u: jnp.concatenate([a_ak[u], pick(kap_t, u)], axis=1) for u in units}
    ylhs = {u: jnp.concatenate([a_rk[u], -a_rb[u], pick(r_t, u)], axis=1) for u in units}
    hlhs = {u: jnp.concatenate([pick(k_e, u), -pick(b_e, u)], axis=0) for u in units}
    dec = {u: _row_vector_to_column(e_tot[u[1]][u[0] * C:u[0] * C + 1, sl(u[2])]) for u in units}
    hs = {ch: h_ref[ch[0], ch[1]] for ch in chains}
    ys = {}
    for j in order:
        units_j = [(j,) + ch for ch in chains]
        rhs = {u: _mm(ulhs[u], jnp.concatenate([bdv[u], hs[u[1:]]], axis=0)) for u in units_j}
        u_p = {u: _mm(t_inv[u], _block_diag(rhs[u], first)) for u in units_j}
        for u in units_j:
            ys[u] = _mm(ylhs[u], jnp.concatenate([bdv[u], _block_diag(u_p[u], first), hs[u[1:]]], axis=0))
        for u in units_j:
            upd = _mm_tn(hlhs[u], jnp.concatenate([vp[u], u_p[u]], axis=0))
            hs[u[1:]] = dec[u] * hs[u[1:]] + jnp.where(diag, upd, 0.0)
    for ch in chains:
        h_ref[ch[0], ch[1]] = hs[ch]
    for nb in range(NB):
        if final:
            yprev = yprev_ref[nb]
            outs = []
            for pr in range(npair):
                y = jnp.concatenate([ys[(j, nb, pr)] for j in range(nsub)], axis=0) + yprev[:, sl(pr)]
                s_a = jnp.sum(jnp.where(first, y, 0.0), axis=-1, keepdims=True)
                s_b = jnp.sum(jnp.where(first, 0.0, y), axis=-1, keepdims=True)
                yc = y - jnp.where(first, s_a, s_b) * (1.0 / RWKV_HEAD)
                q = yc * yc
                v_a = jnp.sum(jnp.where(first, q, 0.0), axis=-1, keepdims=True)
                v_b = jnp.sum(jnp.where(first, 0.0, q), axis=-1, keepdims=True)
                outs.append(yc * lax.rsqrt(jnp.where(first, v_a, v_b) * (1.0 / RWKV_HEAD) + RWKV_GN_EPS))
            gn = gn_ref[...]
            o_ref[nb] = (jnp.concatenate(outs, axis=1) * gn[0:1] + gn[1:2]) * g_ref[nb].astype(F32) + gb_ref[nb].astype(F32)
        else:
            o_ref[nb] = jnp.concatenate(
                [jnp.concatenate([ys[(j, nb, pr)] for pr in range(npair)], axis=1) for j in range(nsub)], axis=0)


def _rwkv_scan(r, v, kk, lw, k, b, nzc, reverse, yprev=None, g=None, gb=None, gn=None):
    B, L, _ = r.shape
    C = SCAN_CHUNK * SCAN_SUB
    NB = _scan_batch(B, RWKV_SCAN_BATCH)
    nc = L // C
    nzc = nzc // SCAN_SUB
    cidx = _chunk_index(reverse, nzc, nc)
    tok = pl.BlockSpec((NB, C, MIX_W), lambda b_, c: (b_, cidx(c), 0))
    final = yprev is not None
    ins = [r, v, kk, lw, k, b]
    specs = [tok] * 6
    if final:
        ins += [yprev, g, gb, gn]
        specs += [tok] * 3 + [_full(gn.shape)]
    return pl.pallas_call(
        functools.partial(_rwkv_scan_kernel, reverse, final),
        grid=(B // NB, nc),
        in_specs=specs,
        out_specs=tok,
        out_shape=jax.ShapeDtypeStruct((B, L, MIX_W), F32),
        scratch_shapes=[pltpu.VMEM((NB, RWKV_HEADS // 2, LANE, LANE), F32)],
        compiler_params=_params("parallel", "arbitrary"),
        name="rwkv_bwd" if reverse else "rwkv_fwd",
    )(*ins)


def _merge_kernel(stream, off, *refs):
    ns = len(stream.arrays)
    mod_ref, g_ref, ya_ref, yb_ref, yc_ref, wg_ref, wb_ref, wo_ref, o_ref = refs[ns:]
    m = mod_ref[...]
    s = stream.read(refs[:ns], off)
    h = _norm_mod(s, g_ref[...], m[0:1], m[1:2]).astype(BF16)
    gates = jnp.dot(h, wg_ref[...], preferred_element_type=F32)
    acc = None
    for i, y_ref in enumerate((ya_ref, yb_ref, yc_ref)):
        t = _sigmoid(gates[:, i * D_MODEL:(i + 1) * D_MODEL]) * _mm(y_ref[...], wb_ref[i])
        acc = t if acc is None else acc + t
    o_ref[...] = s + m[2:3] * _mm(acc, wo_ref[...])


def _merge(stream, l, mod, g, ya, yb, yc, wg, wb, wo, off):
    B, L = stream.batch, stream.rows
    TM = ROW_TILE
    nt = L // TM - off
    tok_in = lambda wd: pl.BlockSpec((None, TM, wd), lambda b, i: (b, i + off, 0))
    return pl.pallas_call(
        functools.partial(_merge_kernel, stream, off),
        grid=(B, nt),
        in_specs=stream.specs(off) + [_mod_spec(l, stream.nzt, B, off), _layer_spec(g, l),
                                      tok_in(MIX_W), tok_in(MIX_W), tok_in(MIX_W),
                                      _layer_spec(wg, l), _layer_spec(wb, l), _layer_spec(wo, l)],
        out_specs=pl.BlockSpec((None, TM, D_MODEL), lambda b, i: (b, i, 0)),
        out_shape=jax.ShapeDtypeStruct((B, nt * TM, D_MODEL), F32),
        compiler_params=_params("parallel", "parallel"),
        name="merge",
    )(*stream.arrays, mod, g, ya, yb, yc, wg, wb, wo)


def _ffn_kernel(final, s_ref, mod_ref, g_ref, w1_ref, w3_ref, w2_ref, gf_ref, o_ref):
    m = mod_ref[...]
    s = s_ref[...]
    h = _norm_mod(s, g_ref[...], m[3:4], m[4:5]).astype(BF16)
    a = jnp.dot(h, w1_ref[...], preferred_element_type=F32)
    b = jnp.dot(h, w3_ref[...], preferred_element_type=F32)
    o = s + m[5:6] * _mm(_silu(a) * b, w2_ref[...])
    if final:
        o = o * lax.rsqrt(jnp.mean(o * o, axis=-1, keepdims=True) + NORM_EPS) * gf_ref[...]
    o_ref[...] = o


def _ffn(s, l, mod, g, w1, w3, w2, gf, nzt, off, final):
    B, L, _ = s.shape
    TM = ROW_TILE
    nt = L // TM
    once = pl.Buffered(1)
    return pl.pallas_call(
        functools.partial(_ffn_kernel, final),
        grid=(B, nt),
        in_specs=[pl.BlockSpec((None, TM, D_MODEL), lambda b, i: (b, i, 0)), _mod_spec(l, nzt, B, off), _layer_spec(g, l),
                  _layer_spec(w1, l, pipeline_mode=once), _layer_spec(w3, l, pipeline_mode=once),
                  _layer_spec(w2, l, pipeline_mode=once), _full((1, D_MODEL))],
        out_specs=pl.BlockSpec((None, TM, D_MODEL), lambda b, i: (b, i, 0)),
        out_shape=jax.ShapeDtypeStruct((B, L, D_MODEL), F32),
        compiler_params=_params("parallel", "parallel"),
        name="ffn",
    )(s, mod, g, w1, w3, w2, gf)


def _rope_tables(T, lz):
    rows = T // GRID_W
    row = jnp.repeat(jnp.arange(rows), GRID_W).astype(F32)
    col = jnp.tile(jnp.arange(GRID_W), rows).astype(F32)
    inv = ROPE_BASE ** (-jnp.arange(ROPE_FREQS, dtype=F32) / ROPE_FREQS)
    ar, ac = row[:, None] * inv, col[:, None] * inv
    cos = jnp.concatenate([jnp.cos(ar), jnp.cos(ar), jnp.cos(ac), jnp.cos(ac)], axis=1)
    sin = jnp.concatenate([-jnp.sin(ar), jnp.sin(ar), -jnp.sin(ac), jnp.sin(ac)], axis=1)
    reps = LANE // SWA_HEAD_DIM
    x_tab = jnp.concatenate([jnp.tile(cos, (1, reps)), jnp.tile(sin, (1, reps))], axis=1)
    z_tab = jnp.concatenate([jnp.ones((lz, LANE), F32), jnp.zeros((lz, LANE), F32)], axis=1)
    return jnp.concatenate([z_tab, x_tab], axis=0)


def _partner_cols(w):
    return w.reshape(w.shape[:-1] + (-1, 2, ROPE_FREQS))[..., ::-1, :].reshape(w.shape)


def _mixer_weight(w_in):
    g0, g1, g2 = np.cumsum(GROUP_SPLIT)[:3]
    w_in = w_in.astype(BF16)
    gla, swa, rwkv = w_in[..., :g0], w_in[..., g0:g1], w_in[..., g1:g2]
    nk = GLA_HEADS * GLA_DK
    a = 2 * nk + MIX_W
    gla_p = jnp.concatenate([gla[..., :a], gla[..., a + 2 * GLA_GATE_RANK:], gla[..., a:a + 2 * GLA_GATE_RANK],
                             jnp.zeros(w_in.shape[:-1] + (LANE - 2 * GLA_GATE_RANK,), BF16)], axis=-1)
    qw = SWA_Q_HEADS * SWA_HEAD_DIM
    kvw = SWA_KV_HEADS * SWA_HEAD_DIM
    q, k, v = swa[..., :qw], swa[..., qw:qw + kvw], swa[..., qw + kvw:]
    w = jnp.concatenate([gla_p, q, _partner_cols(q), k, _partner_cols(k), v, rwkv], axis=-1)
    return w, w_in[..., g2:]


def _pad_rows(w, top, total):
    return jnp.concatenate([jnp.zeros((top, w.shape[1]), F32), w, jnp.zeros((total - top - w.shape[0], w.shape[1]), F32)], axis=0)


def kernel(x, c, ctx, c_ctx, w_ada, b_ada, g_mix, g_ffn, w_in, w_gk2, b_gk, g_gla, sink, mu_shift, w0, w_w2, a0, w_a2,
           w_g2, k_k, k_a, r_k, gn_w, gn_b, w_branch, w_out, w_ffn1, w_ffn3, w_ffn2, g_final):
    B, T, D = x.shape
    lz = ctx.shape[1]
    depth = w_in.shape[0]
    assert D == D_MODEL and lz % ROW_TILE == 0 and T % ROW_TILE == 0 and T % GRID_W == 0
    nzt, nzc, nzb = lz // ROW_TILE, lz // SCAN_CHUNK, lz // SWA_BLOCK
    cs = _rope_tables(T, lz)
    c8 = jnp.concatenate([c, c_ctx[None], jnp.zeros((8 - (B + 1) % 8 if (B + 1) % 8 else 0, D), F32)], axis=0)
    hidx = np.arange(MIX_W) // RWKV_HEAD
    seg = jnp.asarray(hidx[:, None] == hidx[None, :], BF16)
    mod = _ada(c8, w_ada, b_ada).reshape(depth, c8.shape[0], 6, D)
    w_mix, w_gate = _mixer_weight(w_in)
    w_branch, w_out = w_branch.astype(BF16), w_out.astype(BF16)
    w_ffn1, w_ffn3, w_ffn2 = w_ffn1.astype(BF16), w_ffn3.astype(BF16), w_ffn2.astype(BF16)
    g_mix, g_ffn = g_mix[:, None], g_ffn[:, None]
    stream = _Stream((ctx, x), nzt)
    out = None
    for l in range(depth):
        last = l == depth - 1
        p_gla, p_q, p_kv, p_rwkv = _inproj(stream, l, mod, g_mix, w_mix)

        wgk = [_pad_rows(w_gk2[l, d], d * GLA_GATE_RANK, LANE).astype(BF16) for d in range(2)]
        o_f = _gla_scan(p_gla, wgk[0], b_gk[l, 0][None], nzc, False)
        ya = _gla_scan(p_gla, wgk[1], b_gk[l, 1][None], nzc, True, o_f, g_gla[l][None])

        yb = _swa(p_q, p_kv, cs, sink[l], nzb)

        zero = jnp.zeros((MIX_W,), F32)
        mu4 = jnp.concatenate([mu_shift[l], jnp.zeros((4 * MIX_W - RWKV_W,), F32)]).reshape(4, MIX_W)
        vec = jnp.stack([w0[l, 0], w0[l, 1], a0[l, 0], a0[l, 1], k_k[l], k_a[l], r_k[l].reshape(-1), zero], axis=0)
        vec = jnp.concatenate([vec, mu4, jnp.zeros((4, MIX_W), F32)], axis=0)
        ww = jnp.stack([_pad_rows(w_w2[l, d], 0, LANE) for d in range(2)]).astype(BF16)
        wa = jnp.stack([_pad_rows(w_a2[l, d], RWKV_DECAY_RANK, LANE) for d in range(2)]).astype(BF16)
        f_r, f_v, f_kk, f_g, f_gb, lw_f, k_f, b_f, lw_b, k_b, b_b = _rwkv_feat(
            p_rwkv, vec, ww, wa, w_g2[l].astype(BF16), seg, nzt)
        y_f = _rwkv_scan(f_r, f_v, f_kk, lw_f, k_f, b_f, nzc, False)
        yc = _rwkv_scan(f_r, f_v, f_kk, lw_b, k_b, b_b, nzc, True, y_f, f_g, f_gb, jnp.stack([gn_w[l], gn_b[l]]))

        off = nzt if last else 0
        s_mix = _merge(stream, l, mod, g_mix, ya, yb, yc, w_gate, w_branch, w_out, off)
        out = _ffn(s_mix, l, mod, g_ffn, w_ffn1, w_ffn3, w_ffn2, g_final[None], nzt, off, last)
        stream = _Stream((out,), nzt)
    return out
```

```python
import functools

import jax
import jax.numpy as jnp
import numpy as np
from jax import lax
from jax.experimental import pallas as pl
from jax.experimental.pallas import tpu as pltpu

F32 = jnp.float32
BF16 = jnp.bfloat16
HI = lax.Precision.HIGHEST
LOG2E = 1.4426950408889634
DECAY_SCALE = 0.6065306597126334

D_MODEL = 1024
GRID_W = 64
NORM_EPS = 1e-6
NEG_INF = -1e30
MIX_W = D_MODEL // 2
N_BRANCH = 3
GLA_HEADS = 4
GLA_DV = MIX_W // GLA_HEADS
GLA_DK = GLA_DV // 2
GLA_GATE_RANK = 16
GLA_GATE_NORM = 16.0
SWA_HEAD_DIM = 64
SWA_Q_HEADS = MIX_W // SWA_HEAD_DIM
SWA_KV_HEADS = SWA_Q_HEADS // 4
SWA_GROUP = SWA_Q_HEADS // SWA_KV_HEADS
SWA_WINDOW = 128
SWA_BLOCK = 128
ROPE_FREQS = SWA_HEAD_DIM // 4
ROPE_BASE = 10000.0
RWKV_HEAD = 64
RWKV_HEADS = MIX_W // RWKV_HEAD
RWKV_DECAY_RANK = 64
RWKV_A_RANK = 64
RWKV_GATE_RANK = 128
RWKV_GN_EPS = 64e-5
FFN_HIDDEN = -(-8 * D_MODEL // (3 * 256)) * 256
GLA_SPLIT = (GLA_HEADS * GLA_DK, GLA_HEADS * GLA_DK, MIX_W, GLA_GATE_RANK, GLA_GATE_RANK, MIX_W)
SWA_SPLIT = (SWA_Q_HEADS * SWA_HEAD_DIM, SWA_KV_HEADS * SWA_HEAD_DIM, SWA_KV_HEADS * SWA_HEAD_DIM)
RWKV_SPLIT = (MIX_W, MIX_W, MIX_W, RWKV_DECAY_RANK, RWKV_A_RANK, RWKV_GATE_RANK)
GROUP_SPLIT = (sum(GLA_SPLIT), sum(SWA_SPLIT), sum(RWKV_SPLIT), N_BRANCH * D_MODEL)

LANE = 128
BF16_ROWS = 16
ROW_TILE = 256
HALO = BF16_ROWS
SCAN_CHUNK = 64
SCAN_SUB = 2
GLA_SUB = 4
RWKV_SCAN_BATCH = 4
GLA_SPLIT_EXP_LIMIT = 80.0
PAIR_HEAD = LANE // 2
assert GLA_DK == PAIR_HEAD and RWKV_HEAD == PAIR_HEAD and SCAN_CHUNK == PAIR_HEAD
GLA_W = 2 * GLA_HEADS * GLA_DK + 2 * MIX_W + LANE
SWA_Q_W = 1024
SWA_KV_W = 384
RWKV_W = sum(RWKV_SPLIT)
MIXER_W = GLA_W + SWA_Q_W + SWA_KV_W + RWKV_W
VMEM_LIMIT = 56 * 1024 * 1024


def _mm(a, b):
    return jnp.dot(a.astype(BF16), b.astype(BF16), preferred_element_type=F32)


def _mm_nt(a, b):
    return lax.dot_general(a.astype(BF16), b.astype(BF16), (((1,), (1,)), ((), ())), preferred_element_type=F32)


def _mm_tn(a, b):
    return lax.dot_general(a.astype(BF16), b.astype(BF16), (((0,), (0,)), ((), ())), preferred_element_type=F32)


def _mm_hi(a, b):
    return jnp.dot(a, b, precision=HI, preferred_element_type=F32)


def _sigmoid(x):
    return 0.5 * jnp.tanh(0.5 * x) + 0.5


def _silu(x):
    return x * _sigmoid(x)


def _softplus(x):
    return jnp.maximum(x, 0.0) + jnp.log(1.0 + jnp.exp(-jnp.abs(x)))


def _norm_mod(x, g, shift, scale):
    y = x * lax.rsqrt(jnp.mean(x * x, axis=-1, keepdims=True) + NORM_EPS)
    return (y * g) * (1.0 + scale) + shift


def _segsum(x, e):
    return jnp.dot(x.astype(BF16), e, preferred_element_type=F32)


def _params(*sem):
    return pltpu.CompilerParams(dimension_semantics=sem, vmem_limit_bytes=VMEM_LIMIT)


def _full(shape):
    nd = len(shape)
    return pl.BlockSpec(shape, lambda *_: (0,) * nd)


def _ada_kernel(c_ref, w_ref, b_ref, o_ref):
    o_ref[...] = _mm_hi(_silu(c_ref[...]), w_ref[...]) + b_ref[...]


def _ada(c8, w, b):
    depth, _, n = w.shape
    tn = 1536
    return pl.pallas_call(
        _ada_kernel,
        grid=(depth, n // tn),
        in_specs=[_full(c8.shape), pl.BlockSpec((None, D_MODEL, tn), lambda l, j: (l, 0, j)),
                  pl.BlockSpec((None, 1, tn), lambda l, j: (l, 0, j))],
        out_specs=pl.BlockSpec((None, c8.shape[0], tn), lambda l, j: (l, 0, j)),
        out_shape=jax.ShapeDtypeStruct((depth, c8.shape[0], n), F32),
        compiler_params=_params("arbitrary", "arbitrary"),
        name="ada",
    )(c8, w, b.reshape(depth, 1, n))


class _Stream:
    def __init__(self, arrays, nzt):
        self.arrays, self.nzt = tuple(arrays), nzt
        self.split = len(self.arrays) == 2
        self.batch = self.arrays[0].shape[0]
        self.rows = sum(a.shape[1] for a in self.arrays)

    def specs(self, off=0):
        TM, D, nzt = ROW_TILE, D_MODEL, self.nzt
        if not self.split:
            return [pl.BlockSpec((None, TM, D), lambda b, i: (b, i + off, 0))]
        return [pl.BlockSpec((None, TM, D), lambda b, i: (b, jnp.minimum(i + off, nzt - 1), 0)),
                pl.BlockSpec((None, TM, D), lambda b, i: (b, jnp.maximum(i + off - nzt, 0), 0))]

    def read(self, refs, off=0):
        if not self.split:
            return refs[0][...]
        return jnp.where(pl.program_id(1) + off < self.nzt, refs[0][...], refs[1][...])


def _layer_spec(w, l, **kw):
    nd = w.ndim - 1
    return pl.BlockSpec((None,) + w.shape[1:], lambda *_: (l,) + (0,) * nd, **kw)


def _mod_spec(l, nzt, nb, off=0):
    return pl.BlockSpec((None, None, 6, D_MODEL), lambda b, i: (l, jnp.where(i + off < nzt, nb, b), 0, 0))


def _inproj_kernel(stream, *refs):
    ns = len(stream.arrays)
    mod_ref, g_ref, w_ref, o_gla, o_q, o_kv, o_rwkv = refs[ns:]
    m = mod_ref[...]
    h = _norm_mod(stream.read(refs[:ns]), g_ref[...], m[0:1], m[1:2]).astype(BF16)
    p = jnp.dot(h, w_ref[...], preferred_element_type=F32)
    o0 = 0
    for o_ref in (o_gla, o_q, o_kv, o_rwkv):
        wdt = o_ref.shape[-1]
        o_ref[...] = p[:, o0:o0 + wdt].astype(o_ref.dtype)
        o0 += wdt


def _inproj(stream, l, mod, g, w):
    B, L = stream.batch, stream.rows
    widths = (GLA_W, SWA_Q_W, SWA_KV_W, RWKV_W)
    return pl.pallas_call(
        functools.partial(_inproj_kernel, stream),
        grid=(B, L // ROW_TILE),
        in_specs=stream.specs() + [_mod_spec(l, stream.nzt, B), _layer_spec(g, l), _layer_spec(w, l)],
        out_specs=[pl.BlockSpec((None, ROW_TILE, wd), lambda b, i: (b, i, 0)) for wd in widths],
        out_shape=[jax.ShapeDtypeStruct((B, L, wd), dt) for wd, dt in zip(widths, (BF16, F32, F32, BF16))],
        compiler_params=_params("parallel", "parallel"),
        name="inproj",
    )(*stream.arrays, mod, g, w)


def _chunk_index(reverse, nzc, nc):
    if not reverse:
        return lambda c: c
    return lambda c: jnp.where(c < nzc, nzc - 1 - c, nzc + nc - 1 - c)


def _order_masks(n, reverse):
    ii = lax.broadcasted_iota(jnp.int32, (n, n), 0)
    jj = lax.broadcasted_iota(jnp.int32, (n, n), 1)
    strict = (jj > ii) if reverse else (jj < ii)
    return strict, strict | (ii == jj), ii == jj


def _block_order_mask(n, chunk, reverse):
    ii = lax.broadcasted_iota(jnp.int32, (n, n), 0)
    jj = lax.broadcasted_iota(jnp.int32, (n, n), 1)
    return ((ii // chunk) == (jj // chunk)) & ((jj >= ii) if reverse else (jj <= ii))


def _mm_01(m01, x):
    hi = x.astype(BF16)
    r1 = x - hi.astype(F32)
    mid = r1.astype(BF16)
    lo = (r1 - mid.astype(F32)).astype(BF16)
    dot = lambda piece: jnp.dot(m01, piece, preferred_element_type=F32)
    return dot(hi) + dot(mid) + dot(lo)


def _chunk_row(x, chunk, r):
    parts = [jnp.broadcast_to(x[j + r:j + r + 1], (chunk, x.shape[1])) for j in range(0, x.shape[0], chunk)]
    return parts[0] if len(parts) == 1 else jnp.concatenate(parts, axis=0)


def _chunk_totals(x, chunk):
    parts = [jnp.broadcast_to(jnp.sum(x[j:j + chunk], axis=0, keepdims=True), (chunk, x.shape[1]))
             for j in range(0, x.shape[0], chunk)]
    return parts[0] if len(parts) == 1 else jnp.concatenate(parts, axis=0)


def _pair_order_masks(n, reverse):
    ii = lax.broadcasted_iota(jnp.int32, (n, LANE), 0)
    jj = lax.broadcasted_iota(jnp.int32, (n, LANE), 1) & (PAIR_HEAD - 1)
    strict = (jj > ii) if reverse else (jj < ii)
    return strict, strict | (ii == jj), ii == jj


def _first_head(width, head_w):
    return (lax.broadcasted_iota(jnp.int32, (1, width), 1) & head_w) == 0


def _block_diag(z, first):
    return jnp.concatenate([jnp.where(first, z, 0.0), jnp.where(first, 0.0, z)], axis=0)


def _diag_blocks(rows, head_w_cols, cols):
    ri = lax.broadcasted_iota(jnp.int32, (rows, cols), 0) // (rows // 2)
    ci = lax.broadcasted_iota(jnp.int32, (rows, cols), 1) // head_w_cols
    return ri == ci


def _row_vector_to_column(row):
    n = row.shape[1]
    eye = lax.broadcasted_iota(jnp.int32, (n, n), 0) == lax.broadcasted_iota(jnp.int32, (n, n), 1)
    return jnp.sum(jnp.where(eye, row, 0.0), axis=1, keepdims=True)


def _gla_kernel(reverse, final, *refs):
    if final:
        p_ref, wgk_ref, bgk_ref, oprev_ref, ggla_ref, o_ref, h_ref, att_ref, cum_ref = refs
    else:
        p_ref, wgk_ref, bgk_ref, o_ref, h_ref, att_ref, cum_ref = refs

    @pl.when(pl.program_id(1) == 0)
    def _():
        h_ref[...] = jnp.zeros_like(h_ref)

    NB, R = p_ref.shape[0], p_ref.shape[1]
    C = SCAN_CHUNK
    nsub = R // C
    order = tuple(reversed(range(nsub))) if reverse else tuple(range(nsub))
    nk = GLA_HEADS * GLA_DK
    npair = GLA_HEADS // 2
    kw, vw = 2 * GLA_DK, 2 * GLA_DV
    incl2 = _pair_order_masks(C, reverse)[1]
    first_k = _first_head(kw, GLA_DK)
    first_v = _first_head(vw, GLA_DV)
    diag = _diag_blocks(kw, GLA_DV, vw)
    inclb = _block_order_mask(R, C, reverse).astype(BF16)
    q_s, q_t, q_m, k_t, k_e, e_tot, v, og = {}, {}, {}, {}, {}, {}, {}, {}
    span = None
    for nb in range(NB):
        p = p_ref[nb].astype(F32)
        q = p[:, 0:nk] * GLA_DK ** -0.5
        k = p[:, nk:2 * nk]
        v[nb] = p[:, 2 * nk:2 * nk + MIX_W]
        og[nb] = p[:, 2 * nk + MIX_W:2 * nk + 2 * MIX_W]
        gk = p[:, 2 * nk + 2 * MIX_W:]
        lg = -_softplus(-(_mm(gk, wgk_ref[...]) + bgk_ref[...])) / GLA_GATE_NORM
        cum = _mm_01(inclb, lg)
        tot = _chunk_totals(lg, C)
        mid = _chunk_row(cum, C, C // 2)
        rel = cum - mid
        part = jnp.max(jnp.abs(rel))
        span = part if span is None else jnp.maximum(span, part)
        cum_ref[nb] = cum
        q_s[nb] = q
        q_t[nb] = q * jnp.exp(cum)
        q_m[nb] = q * jnp.exp(rel)
        k_t[nb] = k * jnp.exp(-rel)
        k_e[nb] = k * jnp.exp(tot - cum)
        e_tot[nb] = jnp.exp(tot)
    ksl = lambda pr: slice(pr * kw, (pr + 1) * kw)
    vsl = lambda pr: slice(pr * vw, (pr + 1) * vw)
    rows = lambda j: slice(j * C, (j + 1) * C)
    chains = [(nb, pr) for nb in range(NB) for pr in range(npair)]
    units = [(j, nb, pr) for j in order for (nb, pr) in chains]
    qp = {(j, nb, pr): q_t[nb][rows(j), ksl(pr)] for (j, nb, pr) in units}
    vp = {(j, nb, pr): v[nb][rows(j), vsl(pr)] for (j, nb, pr) in units}
    kv = {(j, nb, pr): jnp.where(diag, _mm_tn(k_e[nb][rows(j), ksl(pr)], vp[(j, nb, pr)]), 0.0) for (j, nb, pr) in units}
    dec = {(j, nb, pr): _row_vector_to_column(e_tot[nb][j * C:j * C + 1, ksl(pr)]) for (j, nb, pr) in units}
    bdv = {u: _block_diag(vp[u], first_v) for u in units}
    safe = span <= GLA_SPLIT_EXP_LIMIT

    @pl.when(safe)
    def _():
        for idx, (j, nb, pr) in enumerate(units):
            att_ref[idx] = jnp.where(incl2, _mm_nt(q_m[nb][rows(j), ksl(pr)],
                                                   _block_diag(k_t[nb][rows(j), ksl(pr)], first_k)), 0.0)

    @pl.when(jnp.logical_not(safe))
    def _():
        tok = lax.broadcasted_iota(jnp.int32, (C, 1), 0)
        lane = lax.broadcasted_iota(jnp.int32, (1, kw), 1)
        for idx, (j, nb, pr) in enumerate(units):
            q_i = q_s[nb][rows(j), ksl(pr)]
            c_i = cum_ref[nb, rows(j), ksl(pr)]

            def columns(tile, acc, j=j, nb=nb, pr=pr, q_i=q_i, c_i=c_i):
                r0 = pl.multiple_of(j * C + tile * BF16_ROWS, BF16_ROWS)
                k_rows = p_ref[nb, pl.ds(r0, BF16_ROWS), nk + pr * kw:nk + (pr + 1) * kw].astype(F32)
                c_rows = cum_ref[nb, pl.ds(r0, BF16_ROWS), ksl(pr)]
                for r in range(BF16_ROWS):
                    t = tile * BF16_ROWS + r
                    x = q_i * k_rows[r:r + 1] * jnp.exp(jnp.minimum(c_i - c_rows[r:r + 1], 0.0))
                    s_a = jnp.sum(jnp.where(first_k, x, 0.0), axis=1, keepdims=True)
                    s_b = jnp.sum(jnp.where(first_k, 0.0, x), axis=1, keepdims=True)
                    seen = (tok <= t) if reverse else (tok >= t)
                    acc = jnp.where(lane == t, jnp.where(seen, s_a, 0.0), acc)
                    acc = jnp.where(lane == t + GLA_DK, jnp.where(seen, s_b, 0.0), acc)
                return acc

            att_ref[idx] = lax.fori_loop(0, C // BF16_ROWS, columns, jnp.zeros((C, kw), F32))

    att = {u: att_ref[idx] for idx, u in enumerate(units)}
    lhs = {u: jnp.concatenate([att[u], qp[u]], axis=1) for u in units}
    hs = {ch: h_ref[ch[0], ch[1]] for ch in chains}
    outs = {}
    for j in order:
        for ch in chains:
            u = (j,) + ch
            outs[u] = _mm(lhs[u], jnp.concatenate([bdv[u], hs[ch]], axis=0))
        for ch in chains:
            u = (j,) + ch
            hs[ch] = dec[u] * hs[ch] + kv[u]
    for ch in chains:
        h_ref[ch[0], ch[1]] = hs[ch]
    for nb in range(NB):
        o = jnp.concatenate([jnp.concatenate([outs[(j, nb, pr)] for pr in range(npair)], axis=1) for j in range(nsub)], axis=0)
        if final:
            o = o + oprev_ref[nb].astype(F32)
            g = ggla_ref[...]
            ys = []
            for h in range(GLA_HEADS):
                oh = o[:, h * GLA_DV:(h + 1) * GLA_DV]
                ys.append(oh * lax.rsqrt(jnp.mean(oh * oh, axis=-1, keepdims=True) + NORM_EPS) * g)
            o = jnp.concatenate(ys, axis=1) * _silu(og[nb])
        o_ref[nb] = o.astype(o_ref.dtype)


def _scan_batch(B, most=2):
    return max(n for n in (1, 2, 4) if n <= most and B % n == 0)


def _gla_scan(p, wgk, bgk, nzc, reverse, oprev=None, ggla=None):
    B, L, _ = p.shape
    C = SCAN_CHUNK * GLA_SUB
    NB = _scan_batch(B)
    nc = L // C
    nzc = nzc // GLA_SUB
    cidx = _chunk_index(reverse, nzc, nc)
    tok = lambda wd: pl.BlockSpec((NB, C, wd), lambda b, c: (b, cidx(c), 0))
    final = oprev is not None
    ins = [p, wgk, bgk]
    specs = [tok(GLA_W), _full(wgk.shape), _full(bgk.shape)]
    if final:
        ins += [oprev, ggla]
        specs += [tok(MIX_W), _full(ggla.shape)]
    return pl.pallas_call(
        functools.partial(_gla_kernel, reverse, final),
        grid=(B // NB, nc),
        in_specs=specs,
        out_specs=tok(MIX_W),
        out_shape=jax.ShapeDtypeStruct((B, L, MIX_W), F32 if final else BF16),
        scratch_shapes=[pltpu.VMEM((NB, GLA_HEADS // 2, 2 * GLA_DK, 2 * GLA_DV), F32),
                        pltpu.VMEM((GLA_SUB * NB * (GLA_HEADS // 2), SCAN_CHUNK, 2 * GLA_DK), F32),
                        pltpu.VMEM((NB, C, GLA_HEADS * GLA_DK), F32)],
        compiler_params=_params("parallel", "arbitrary"),
        name="gla_bwd" if reverse else "gla_fwd",
    )(*ins)


def _swa_kernel(nzb, nblk, q_ref, kv_ref, cs_ref, sink_ref, o_ref):
    BL = SWA_BLOCK
    nper = q_ref.shape[0] // BL
    kvw = SWA_KV_HEADS * SWA_HEAD_DIM
    qw = SWA_Q_HEADS * SWA_HEAD_DIM
    nloc = 3 * BL
    lz = nzb * BL
    nkeys = nloc + lz

    def rope(x, xs, cs, reps):
        cos = jnp.concatenate([cs[:, :LANE]] * reps, axis=1)
        sin = jnp.concatenate([cs[:, LANE:]] * reps, axis=1)
        return x * cos + xs * sin

    kvz = kv_ref[0:lz, :]
    q, keys, vals, bias = {}, {}, {}, {}
    for sub in range(nper):
        n = pl.program_id(1) * nper + sub
        qq = q_ref[sub * BL:(sub + 1) * BL, :]
        q0 = pl.multiple_of(n * BL, BL)
        q[sub] = rope(qq[:, :qw], qq[:, qw:], cs_ref[pl.ds(q0, BL), :], qw // LANE) * (SWA_HEAD_DIM ** -0.5 * LOG2E)
        k0 = pl.multiple_of(jnp.clip((n - 1) * BL, 0, (nblk - 3) * BL), BL)
        kv = kv_ref[pl.ds(k0, nloc), :]
        keys[sub] = jnp.concatenate([rope(kv[:, :kvw], kv[:, kvw:2 * kvw], cs_ref[pl.ds(k0, nloc), :], kvw // LANE),
                                     kvz[:, :kvw]], axis=0)
        vals[sub] = jnp.concatenate([kv[:, 2 * kvw:], kvz[:, 2 * kvw:]], axis=0)
        qpos = n * BL + lax.broadcasted_iota(jnp.int32, (BL, nkeys), 0)
        col = lax.broadcasted_iota(jnp.int32, (BL, nkeys), 1)
        kpos = k0 + col
        lo = jnp.where(n >= nzb, lz, nblk * BL)
        band = (jnp.abs(kpos - qpos) <= SWA_WINDOW) & (kpos >= lo)
        bias[sub] = jnp.where(band | (col >= nloc), 0.0, NEG_INF)
    chains = [(sub, h) for sub in range(nper) for h in range(SWA_Q_HEADS)]
    kvh = lambda h: slice((h // SWA_GROUP) * SWA_HEAD_DIM, (h // SWA_GROUP + 1) * SWA_HEAD_DIM)
    s = {(sub, h): _mm_nt(q[sub][:, h * SWA_HEAD_DIM:(h + 1) * SWA_HEAD_DIM], keys[sub][:, kvh(h)]) + bias[sub]
         for (sub, h) in chains}
    sk = [sink_ref[h] * LOG2E for h in range(SWA_Q_HEADS)]
    m = {c: jnp.maximum(jnp.max(s[c], axis=-1, keepdims=True), sk[c[1]]) for c in chains}
    e = {c: jnp.exp2(s[c] - m[c]) for c in chains}
    den = {c: jnp.sum(e[c], axis=-1, keepdims=True) + jnp.exp2(sk[c[1]] - m[c]) for c in chains}
    outs = {c: _mm(e[c], vals[c[0]][:, kvh(c[1])]) / den[c] for c in chains}
    for sub in range(nper):
        o_ref[sub * BL:(sub + 1) * BL, :] = jnp.concatenate([outs[(sub, h)] for h in range(SWA_Q_HEADS)], axis=1)


def _swa(pq, pkv, cs, sink, nzb):
    B, L, _ = pq.shape
    BL = SWA_BLOCK
    nblk = L // BL
    nper = 2 if nblk % 2 == 0 else 1
    assert nblk >= 3
    return pl.pallas_call(
        functools.partial(_swa_kernel, nzb, nblk),
        grid=(B, nblk // nper),
        in_specs=[pl.BlockSpec((None, nper * BL, SWA_Q_W), lambda b, n: (b, n, 0)),
                  pl.BlockSpec((None, L, SWA_KV_W), lambda b, n: (b, 0, 0)),
                  _full(cs.shape),
                  pl.BlockSpec(memory_space=pltpu.SMEM)],
        out_specs=pl.BlockSpec((None, nper * BL, MIX_W), lambda b, n: (b, n, 0)),
        out_shape=jax.ShapeDtypeStruct((B, L, MIX_W), F32),
        compiler_params=_params("parallel", "parallel"),
        name="swa",
    )(pq, pkv, cs, sink)


def _rwkv_feat_kernel(nzt, p_ref, pp_ref, pn_ref, vec_ref, ww_ref, wa_ref, wg_ref, e_ref,
                      o_r, o_v, o_kk, o_g, o_gb, o_lwf, o_kf, o_bf, o_lwb, o_kb, o_bb):
    i = pl.program_id(1)
    n = pl.num_programs(1)
    TM = p_ref.shape[0]
    first = (i == 0) | (i == nzt)
    last = (i == nzt - 1) | (i == n - 1)
    p_bf = p_ref[...]
    p_ext = jnp.concatenate([pp_ref[...], p_bf, pn_ref[...]], axis=0)
    row = lax.broadcasted_iota(jnp.int32, (TM, TM + 2 * HALO), 0)
    col = lax.broadcasted_iota(jnp.int32, (TM, TM + 2 * HALO), 1)
    lo = jnp.where(first, HALO, 0)
    hi = jnp.where(last, TM + HALO - 1, TM + 2 * HALO - 1)
    nbr = ((col == row + (HALO - 1)) & (col >= lo)) | ((col == row + (HALO + 1)) & (col <= hi))
    around = jnp.dot(jnp.where(nbr, 1.0, 0.0).astype(BF16), p_ext, preferred_element_type=F32)
    p = p_bf.astype(F32)
    vec = vec_ref[...]
    mu = jnp.concatenate([vec[8 + j:9 + j] for j in range(4)], axis=1)[:, :RWKV_W]
    pm = p * (1.0 - mu) + (0.5 * mu) * around
    W = MIX_W
    r, k, v = pm[:, :W], pm[:, W:2 * W], pm[:, 2 * W:3 * W]
    wa = pm[:, 3 * W:3 * W + LANE]
    gl = pm[:, 3 * W + LANE:]
    e = e_ref[...]
    k_k, k_a, r_k = vec[4:5], vec[5:6], vec[6:7]
    g = _mm(_sigmoid(gl), wg_ref[...])
    kkn = k * k_k
    kk = kkn * lax.rsqrt(jnp.maximum(_segsum(kkn * kkn, e), 1e-24))
    twa = jnp.tanh(wa)
    ksum = None
    for d, (o_lw, o_k, o_b) in enumerate(((o_lwf, o_kf, o_bf), (o_lwb, o_kb, o_bb))):
        a = _sigmoid(vec[2 + d:3 + d] + _mm(wa, wa_ref[d]))
        kd = k * (1.0 + (a - 1.0) * k_a)
        o_lw[...] = -DECAY_SCALE * _sigmoid(vec[d:d + 1] + _mm(twa, ww_ref[d]))
        o_k[...] = kd.astype(o_k.dtype)
        o_b[...] = (kk * a).astype(o_b.dtype)
        ksum = kd if ksum is None else ksum + kd
    o_r[...] = r.astype(o_r.dtype)
    o_v[...] = v.astype(o_v.dtype)
    o_kk[...] = kk.astype(o_kk.dtype)
    o_g[...] = g.astype(o_g.dtype)
    o_gb[...] = (_segsum(r * ksum * r_k, e) * v * g).astype(o_gb.dtype)


def _rwkv_feat(p, vec, ww, wa, wg, e, nzt):
    B, L, _ = p.shape
    TM = ROW_TILE
    rh = TM // HALO
    tok = pl.BlockSpec((None, TM, RWKV_W), lambda b, i: (b, i, 0))
    halo_p = pl.BlockSpec((None, HALO, RWKV_W), lambda b, i: (b, jnp.maximum(i * rh - 1, 0), 0))
    halo_n = pl.BlockSpec((None, HALO, RWKV_W), lambda b, i: (b, jnp.minimum((i + 1) * rh, L // HALO - 1), 0))
    out = pl.BlockSpec((None, TM, MIX_W), lambda b, i: (b, i, 0))
    dts = [BF16] * 5 + [F32, BF16, BF16] * 2
    return pl.pallas_call(
        functools.partial(_rwkv_feat_kernel, nzt),
        grid=(B, L // TM),
        in_specs=[tok, halo_p, halo_n, _full(vec.shape), _full(ww.shape), _full(wa.shape), _full(wg.shape), _full(e.shape)],
        out_specs=[out] * 11,
        out_shape=[jax.ShapeDtypeStruct((B, L, MIX_W), dt) for dt in dts],
        compiler_params=_params("parallel", "parallel"),
        name="rwkv_feat",
    )(p, p, p, vec, ww, wa, wg, e)


def _rwkv_scan_kernel(reverse, final, *refs):
    if final:
        r_ref, v_ref, kk_ref, lw_ref, k_ref, b_ref, yprev_ref, g_ref, gb_ref, gn_ref, o_ref, h_ref = refs
    else:
        r_ref, v_ref, kk_ref, lw_ref, k_ref, b_ref, o_ref, h_ref = refs

    @pl.when(pl.program_id(1) == 0)
    def _():
        h_ref[...] = jnp.zeros_like(h_ref)

    NB, R = lw_ref.shape[0], lw_ref.shape[1]
    C = SCAN_CHUNK
    nsub = R // C
    order = tuple(reversed(range(nsub))) if reverse else tuple(range(nsub))
    PW = LANE
    npair = RWKV_HEADS // 2
    strict2, incl2, ident2 = _pair_order_masks(C, reverse)
    first = _first_head(PW, RWKV_HEAD)
    diag = _diag_blocks(PW, RWKV_HEAD, PW)
    inclb = _block_order_mask(R, C, reverse).astype(BF16)
    kap_t, r_t, k_t, b_t, k_e, b_e, e_tot, v_all = {}, {}, {}, {}, {}, {}, {}, {}
    for nb in range(NB):
        lw = lw_ref[nb]
        cum = _mm_01(inclb, lw)
        tot = _chunk_totals(lw, C)
        e_inv = jnp.exp(-cum)
        e_end = jnp.exp(tot - cum)
        e_tot[nb] = jnp.exp(tot)
        kap_t[nb] = kk_ref[nb].astype(F32) * jnp.exp(cum - lw)
        r_t[nb] = r_ref[nb].astype(F32) * jnp.exp(cum)
        k_all, b_all = k_ref[nb].astype(F32), b_ref[nb].astype(F32)
        k_t[nb], b_t[nb] = k_all * e_inv, b_all * e_inv
        k_e[nb], b_e[nb] = k_all * e_end, b_all * e_end
        v_all[nb] = v_ref[nb].astype(F32)
    sl = lambda pr: slice(pr * PW, (pr + 1) * PW)
    rows = lambda j: slice(j * C, (j + 1) * C)
    pick = lambda d, u: d[u[1]][rows(u[0]), sl(u[2])]
    chains = [(nb, pr) for nb in range(NB) for pr in range(npair)]
    units = [(j,) + ch for j in order for ch in chains]
    vp = {u: pick(v_all, u) for u in units}
    G = {u: _mm_nt(jnp.concatenate([pick(kap_t, u), pick(r_t, u)], axis=0),
                   jnp.concatenate([_block_diag(pick(b_t, u), first), _block_diag(pick(k_t, u), first)], axis=0))
         for u in units}
    a_ab = {u: jnp.where(strict2, G[u][:C, :PW], 0.0) for u in units}
    a_ak = {u: jnp.where(strict2, G[u][:C, PW:], 0.0) for u in units}
    a_rb = {u: jnp.where(incl2, G[u][C:, :PW], 0.0) for u in units}
    a_rk = {u: jnp.where(incl2, G[u][C:, PW:], 0.0) for u in units}
    eye2 = jnp.where(ident2, 1.0, 0.0)
    t_inv = {u: eye2 - a_ab[u] for u in units}
    pw = {u: _mm(a_ab[u], _block_diag(a_ab[u], first)) for u in units}
    span = 2
    while span < C:
        span *= 2
        if span < C:
            prod = {u: _mm(jnp.concatenate([t_inv[u], pw[u]], axis=0), _block_diag(pw[u], first)) for u in units}
            t_inv = {u: t_inv[u] + prod[u][:C] for u in units}
            pw = {u: prod[u][C:] for u in units}
        else:
            t_inv = {u: t_inv[u] + _mm(t_inv[u], _block_diag(pw[u], first)) for u in units}
    bdv = {u: _block_diag(vp[u], first) for u in units}
    ulhs = {u: jnp.concatenate([a_ak[u], pick(kap_t, u)], axis=1) for u in units}
    ylhs = {u: jnp.concatenate([a_rk[u], -a_rb[u], pick(r_t, u)], axis=1) for u in units}
    hlhs = {u: jnp.concatenate([pick(k_e, u), -pick(b_e, u)], axis=0) for u in units}
    dec = {u: _row_vector_to_column(e_tot[u[1]][u[0] * C:u[0] * C + 1, sl(u[2])]) for u in units}
    hs = {ch: h_ref[ch[0], ch[1]] for ch in chains}
    ys = {}
    for j in order:
        units_j = [(j,) + ch for ch in chains]
        rhs = {u: _mm(ulhs[u], jnp.concatenate([bdv[u], hs[u[1:]]], axis=0)) for u in units_j}
        u_p = {u: _mm(t_inv[u], _block_diag(rhs[u], first)) for u in units_j}
        for u in units_j:
            ys[u] = _mm(ylhs[u], jnp.concatenate([bdv[u], _block_diag(u_p[u], first), hs[u[1:]]], axis=0))
        for u in units_j:
            upd = _mm_tn(hlhs[u], jnp.concatenate([vp[u], u_p[u]], axis=0))
            hs[u[1:]] = dec[u] * hs[u[1:]] + jnp.where(diag, upd, 0.0)
    for ch in chains:
        h_ref[ch[0], ch[1]] = hs[ch]
    for nb in range(NB):
        if final:
            yprev = yprev_ref[nb]
            outs = []
            for pr in range(npair):
                y = jnp.concatenate([ys[(j, nb, pr)] for j in range(nsub)], axis=0) + yprev[:, sl(pr)]
                s_a = jnp.sum(jnp.where(first, y, 0.0), axis=-1, keepdims=True)
                s_b = jnp.sum(jnp.where(first, 0.0, y), axis=-1, keepdims=True)
                yc = y - jnp.where(first, s_a, s_b) * (1.0 / RWKV_HEAD)
                q = yc * yc
                v_a = jnp.sum(jnp.where(first, q, 0.0), axis=-1, keepdims=True)
                v_b = jnp.sum(jnp.where(first, 0.0, q), axis=-1, keepdims=True)
                outs.append(yc * lax.rsqrt(jnp.where(first, v_a, v_b) * (1.0 / RWKV_HEAD) + RWKV_GN_EPS))
            gn = gn_ref[...]
            o_ref[nb] = (jnp.concatenate(outs, axis=1) * gn[0:1] + gn[1:2]) * g_ref[nb].astype(F32) + gb_ref[nb].astype(F32)
        else:
            o_ref[nb] = jnp.concatenate(
                [jnp.concatenate([ys[(j, nb, pr)] for pr in range(npair)], axis=1) for j in range(nsub)], axis=0)


def _rwkv_scan(r, v, kk, lw, k, b, nzc, reverse, yprev=None, g=None, gb=None, gn=None):
    B, L, _ = r.shape
    C = SCAN_CHUNK * SCAN_SUB
    NB = _scan_batch(B, RWKV_SCAN_BATCH)
    nc = L // C
    nzc = nzc // SCAN_SUB
    cidx = _chunk_index(reverse, nzc, nc)
    tok = pl.BlockSpec((NB, C, MIX_W), lambda b_, c: (b_, cidx(c), 0))
    final = yprev is not None
    ins = [r, v, kk, lw, k, b]
    specs = [tok] * 6
    if final:
        ins += [yprev, g, gb, gn]
        specs += [tok] * 3 + [_full(gn.shape)]
    return pl.pallas_call(
        functools.partial(_rwkv_scan_kernel, reverse, final),
        grid=(B // NB, nc),
        in_specs=specs,
        out_specs=tok,
        out_shape=jax.ShapeDtypeStruct((B, L, MIX_W), F32),
        scratch_shapes=[pltpu.VMEM((NB, RWKV_HEADS // 2, LANE, LANE), F32)],
        compiler_params=_params("parallel", "arbitrary"),
        name="rwkv_bwd" if reverse else "rwkv_fwd",
    )(*ins)


def _merge_kernel(stream, off, *refs):
    ns = len(stream.arrays)
    mod_ref, g_ref, ya_ref, yb_ref, yc_ref, wg_ref, wb_ref, wo_ref, o_ref = refs[ns:]
    m = mod_ref[...]
    s = stream.read(refs[:ns], off)
    h = _norm_mod(s, g_ref[...], m[0:1], m[1:2]).astype(BF16)
    gates = jnp.dot(h, wg_ref[...], preferred_element_type=F32)
    acc = None
    for i, y_ref in enumerate((ya_ref, yb_ref, yc_ref)):
        t = _sigmoid(gates[:, i * D_MODEL:(i + 1) * D_MODEL]) * _mm(y_ref[...], wb_ref[i])
        acc = t if acc is None else acc + t
    o_ref[...] = s + m[2:3] * _mm(acc, wo_ref[...])


def _merge(stream, l, mod, g, ya, yb, yc, wg, wb, wo, off):
    B, L = stream.batch, stream.rows
    TM = ROW_TILE
    nt = L // TM - off
    tok_in = lambda wd: pl.BlockSpec((None, TM, wd), lambda b, i: (b, i + off, 0))
    return pl.pallas_call(
        functools.partial(_merge_kernel, stream, off),
        grid=(B, nt),
        in_specs=stream.specs(off) + [_mod_spec(l, stream.nzt, B, off), _layer_spec(g, l),
                                      tok_in(MIX_W), tok_in(MIX_W), tok_in(MIX_W),
                                      _layer_spec(wg, l), _layer_spec(wb, l), _layer_spec(wo, l)],
        out_specs=pl.BlockSpec((None, TM, D_MODEL), lambda b, i: (b, i, 0)),
        out_shape=jax.ShapeDtypeStruct((B, nt * TM, D_MODEL), F32),
        compiler_params=_params("parallel", "parallel"),
        name="merge",
    )(*stream.arrays, mod, g, ya, yb, yc, wg, wb, wo)


def _ffn_kernel(final, s_ref, mod_ref, g_ref, w1_ref, w3_ref, w2_ref, gf_ref, o_ref):
    m = mod_ref[...]
    s = s_ref[...]
    h = _norm_mod(s, g_ref[...], m[3:4], m[4:5]).astype(BF16)
    a = jnp.dot(h, w1_ref[...], preferred_element_type=F32)
    b = jnp.dot(h, w3_ref[...], preferred_element_type=F32)
    o = s + m[5:6] * _mm(_silu(a) * b, w2_ref[...])
    if final:
        o = o * lax.rsqrt(jnp.mean(o * o, axis=-1, keepdims=True) + NORM_EPS) * gf_ref[...]
    o_ref[...] = o


def _ffn(s, l, mod, g, w1, w3, w2, gf, nzt, off, final):
    B, L, _ = s.shape
    TM = ROW_TILE
    nt = L // TM
    once = pl.Buffered(1)
    return pl.pallas_call(
        functools.partial(_ffn_kernel, final),
        grid=(B, nt),
        in_specs=[pl.BlockSpec((None, TM, D_MODEL), lambda b, i: (b, i, 0)), _mod_spec(l, nzt, B, off), _layer_spec(g, l),
                  _layer_spec(w1, l, pipeline_mode=once), _layer_spec(w3, l, pipeline_mode=once),
                  _layer_spec(w2, l, pipeline_mode=once), _full((1, D_MODEL))],
        out_specs=pl.BlockSpec((None, TM, D_MODEL), lambda b, i: (b, i, 0)),
        out_shape=jax.ShapeDtypeStruct((B, L, D_MODEL), F32),
        compiler_params=_params("parallel", "parallel"),
        name="ffn",
    )(s, mod, g, w1, w3, w2, gf)


def _rope_tables(T, lz):
    rows = T // GRID_W
    row = jnp.repeat(jnp.arange(rows), GRID_W).astype(F32)
    col = jnp.tile(jnp.arange(GRID_W), rows).astype(F32)
    inv = ROPE_BASE ** (-jnp.arange(ROPE_FREQS, dtype=F32) / ROPE_FREQS)
    ar, ac = row[:, None] * inv, col[:, None] * inv
    cos = jnp.concatenate([jnp.cos(ar), jnp.cos(ar), jnp.cos(ac), jnp.cos(ac)], axis=1)
    sin = jnp.concatenate([-jnp.sin(ar), jnp.sin(ar), -jnp.sin(ac), jnp.sin(ac)], axis=1)
    reps = LANE // SWA_HEAD_DIM
    x_tab = jnp.concatenate([jnp.tile(cos, (1, reps)), jnp.tile(sin, (1, reps))], axis=1)
    z_tab = jnp.concatenate([jnp.ones((lz, LANE), F32), jnp.zeros((lz, LANE), F32)], axis=1)
    return jnp.concatenate([z_tab, x_tab], axis=0)


def _partner_cols(w):
    return w.reshape(w.shape[:-1] + (-1, 2, ROPE_FREQS))[..., ::-1, :].reshape(w.shape)


def _mixer_weight(w_in):
    g0, g1, g2 = np.cumsum(GROUP_SPLIT)[:3]
    w_in = w_in.astype(BF16)
    gla, swa, rwkv = w_in[..., :g0], w_in[..., g0:g1], w_in[..., g1:g2]
    nk = GLA_HEADS * GLA_DK
    a = 2 * nk + MIX_W
    gla_p = jnp.concatenate([gla[..., :a], gla[..., a + 2 * GLA_GATE_RANK:], gla[..., a:a + 2 * GLA_GATE_RANK],
                             jnp.zeros(w_in.shape[:-1] + (LANE - 2 * GLA_GATE_RANK,), BF16)], axis=-1)
    qw = SWA_Q_HEADS * SWA_HEAD_DIM
    kvw = SWA_KV_HEADS * SWA_HEAD_DIM
    q, k, v = swa[..., :qw], swa[..., qw:qw + kvw], swa[..., qw + kvw:]
    w = jnp.concatenate([gla_p, q, _partner_cols(q), k, _partner_cols(k), v, rwkv], axis=-1)
    return w, w_in[..., g2:]


def _pad_rows(w, top, total):
    return jnp.concatenate([jnp.zeros((top, w.shape[1]), F32), w, jnp.zeros((total - top - w.shape[0], w.shape[1]), F32)], axis=0)


def kernel(x, c, ctx, c_ctx, w_ada, b_ada, g_mix, g_ffn, w_in, w_gk2, b_gk, g_gla, sink, mu_shift, w0, w_w2, a0, w_a2,
           w_g2, k_k, k_a, r_k, gn_w, gn_b, w_branch, w_out, w_ffn1, w_ffn3, w_ffn2, g_final):
    B, T, D = x.shape
    lz = ctx.shape[1]
    depth = w_in.shape[0]
    assert D == D_MODEL and lz % ROW_TILE == 0 and T % ROW_TILE == 0 and T % GRID_W == 0
    nzt, nzc, nzb = lz // ROW_TILE, lz // SCAN_CHUNK, lz // SWA_BLOCK
    cs = _rope_tables(T, lz)
    c8 = jnp.concatenate([c, c_ctx[None], jnp.zeros((8 - (B + 1) % 8 if (B + 1) % 8 else 0, D), F32)], axis=0)
    hidx = np.arange(MIX_W) // RWKV_HEAD
    seg = jnp.asarray(hidx[:, None] == hidx[None, :], BF16)
    mod = _ada(c8, w_ada, b_ada).reshape(depth, c8.shape[0], 6, D)
    w_mix, w_gate = _mixer_weight(w_in)
    w_branch, w_out = w_branch.astype(BF16), w_out.astype(BF16)
    w_ffn1, w_ffn3, w_ffn2 = w_ffn1.astype(BF16), w_ffn3.astype(BF16), w_ffn2.astype(BF16)
    g_mix, g_ffn = g_mix[:, None], g_ffn[:, None]
    stream = _Stream((ctx, x), nzt)
    out = None
    for l in range(depth):
        last = l == depth - 1
        p_gla, p_q, p_kv, p_rwkv = _inproj(stream, l, mod, g_mix, w_mix)

        wgk = [_pad_rows(w_gk2[l, d], d * GLA_GATE_RANK, LANE).astype(BF16) for d in range(2)]
        o_f = _gla_scan(p_gla, wgk[0], b_gk[l, 0][None], nzc, False)
        ya = _gla_scan(p_gla, wgk[1], b_gk[l, 1][None], nzc, True, o_f, g_gla[l][None])

        yb = _swa(p_q, p_kv, cs, sink[l], nzb)

        zero = jnp.zeros((MIX_W,), F32)
        mu4 = jnp.concatenate([mu_shift[l], jnp.zeros((4 * MIX_W - RWKV_W,), F32)]).reshape(4, MIX_W)
        vec = jnp.stack([w0[l, 0], w0[l, 1], a0[l, 0], a0[l, 1], k_k[l], k_a[l], r_k[l].reshape(-1), zero], axis=0)
        vec = jnp.concatenate([vec, mu4, jnp.zeros((4, MIX_W), F32)], axis=0)
        ww = jnp.stack([_pad_rows(w_w2[l, d], 0, LANE) for d in range(2)]).astype(BF16)
        wa = jnp.stack([_pad_rows(w_a2[l, d], RWKV_DECAY_RANK, LANE) for d in range(2)]).astype(BF16)
        f_r, f_v, f_kk, f_g, f_gb, lw_f, k_f, b_f, lw_b, k_b, b_b = _rwkv_feat(
            p_rwkv, vec, ww, wa, w_g2[l].astype(BF16), seg, nzt)
        y_f = _rwkv_scan(f_r, f_v, f_kk, lw_f, k_f, b_f, nzc, False)
        yc = _rwkv_scan(f_r, f_v, f_kk, lw_b, k_b, b_b, nzc, True, y_f, f_g, f_gb, jnp.stack([gn_w[l], gn_b[l]]))

        off = nzt if last else 0
        s_mix = _merge(stream, l, mod, g_mix, ya, yb, yc, w_gate, w_branch, w_out, off)
        out = _ffn(s_mix, l, mod, g_ffn, w_ffn1, w_ffn3, w_ffn2, g_final[None], nzt, off, last)
        stream = _Stream((out,), nzt)
    return out
```

```python
import functools

import jax
import jax.numpy as jnp
import numpy as np
from jax import lax
from jax.experimental import pallas as pl
from jax.experimental.pallas import tpu as pltpu

F32 = jnp.float32
BF16 = jnp.bfloat16
LOG2E = 1.4426950408889634
DECAY_SCALE = 0.6065306597126334

D_MODEL = 1024
GRID_W = 64
NORM_EPS = 1e-6
NEG_INF = -1e30
MIX_W = D_MODEL // 2
N_BRANCH = 3
GLA_HEADS = 4
GLA_DV = MIX_W // GLA_HEADS
GLA_DK = GLA_DV // 2
GLA_GATE_RANK = 16
GLA_GATE_NORM = 16.0
SWA_HEAD_DIM = 64
SWA_Q_HEADS = MIX_W // SWA_HEAD_DIM
SWA_KV_HEADS = SWA_Q_HEADS // 4
SWA_GROUP = SWA_Q_HEADS // SWA_KV_HEADS
SWA_WINDOW = 128
SWA_BLOCK = 128
ROPE_FREQS = SWA_HEAD_DIM // 4
ROPE_BASE = 10000.0
RWKV_HEAD = 64
RWKV_HEADS = MIX_W // RWKV_HEAD
RWKV_DECAY_RANK = 64
RWKV_A_RANK = 64
RWKV_GATE_RANK = 128
RWKV_GN_EPS = 64e-5
FFN_HIDDEN = -(-8 * D_MODEL // (3 * 256)) * 256
GLA_SPLIT = (GLA_HEADS * GLA_DK, GLA_HEADS * GLA_DK, MIX_W, GLA_GATE_RANK, GLA_GATE_RANK, MIX_W)
SWA_SPLIT = (SWA_Q_HEADS * SWA_HEAD_DIM, SWA_KV_HEADS * SWA_HEAD_DIM, SWA_KV_HEADS * SWA_HEAD_DIM)
RWKV_SPLIT = (MIX_W, MIX_W, MIX_W, RWKV_DECAY_RANK, RWKV_A_RANK, RWKV_GATE_RANK)
GROUP_SPLIT = (sum(GLA_SPLIT), sum(SWA_SPLIT), sum(RWKV_SPLIT), N_BRANCH * D_MODEL)

LANE = 128
BF16_ROWS = 16
ROW_TILE = 256
HALO = BF16_ROWS
SCAN_CHUNK = 64
SCAN_SUB = 2
GLA_SUB = 4
RWKV_SCAN_BATCH = 4
GLA_SCAN_BATCH = 4
GLA_SPLIT_EXP_LIMIT = 80.0
PAIR_HEAD = LANE // 2
assert GLA_DK == PAIR_HEAD and RWKV_HEAD == PAIR_HEAD and SCAN_CHUNK == PAIR_HEAD
GLA_W = 2 * GLA_HEADS * GLA_DK + 2 * MIX_W + LANE
SWA_Q_W = 1024
SWA_KV_W = 384
RWKV_W = sum(RWKV_SPLIT)
MIXER_W = GLA_W + SWA_Q_W + SWA_KV_W + RWKV_W
VMEM_LIMIT = 56 * 1024 * 1024


def _mm(a, b):
    return jnp.dot(a.astype(BF16), b.astype(BF16), preferred_element_type=F32)


def _mm_nt(a, b):
    return lax.dot_general(a.astype(BF16), b.astype(BF16), (((1,), (1,)), ((), ())), preferred_element_type=F32)


def _mm_tn(a, b):
    return lax.dot_general(a.astype(BF16), b.astype(BF16), (((0,), (0,)), ((), ())), preferred_element_type=F32)


def _sigmoid(x):
    return 0.5 * jnp.tanh(0.5 * x) + 0.5


def _silu(x):
    return x * _sigmoid(x)


def _softplus(x):
    return jnp.maximum(x, 0.0) + jnp.log(1.0 + jnp.exp(-jnp.abs(x)))


def _norm_mod(x, g, shift, scale):
    y = x * lax.rsqrt(jnp.mean(x * x, axis=-1, keepdims=True) + NORM_EPS)
    return (y * g) * (1.0 + scale) + shift


def _segsum(x, e):
    return jnp.dot(x.astype(BF16), e, preferred_element_type=F32)


def _params(*sem):
    return pltpu.CompilerParams(dimension_semantics=sem, vmem_limit_bytes=VMEM_LIMIT)


def _full(shape):
    nd = len(shape)
    return pl.BlockSpec(shape, lambda *_: (0,) * nd)


def _split_bf16(x):
    hi = x.astype(BF16)
    return hi, (x - hi.astype(F32)).astype(BF16)


def _ada_kernel(c_ref, w_ref, b_ref, o_ref):
    s_hi, s_lo = _split_bf16(_silu(c_ref[...]))
    w_hi, w_lo = _split_bf16(w_ref[...])
    dot = lambda a, b: jnp.dot(a, b, preferred_element_type=F32)
    o_ref[...] = dot(s_hi, w_hi) + dot(s_lo, w_hi) + dot(s_hi, w_lo) + b_ref[...]


def _ada(c8, w, b):
    depth, _, n = w.shape
    tn = 1536
    return pl.pallas_call(
        _ada_kernel,
        grid=(depth, n // tn),
        in_specs=[_full(c8.shape), pl.BlockSpec((None, D_MODEL, tn), lambda l, j: (l, 0, j)),
                  pl.BlockSpec((None, 1, tn), lambda l, j: (l, 0, j))],
        out_specs=pl.BlockSpec((None, c8.shape[0], tn), lambda l, j: (l, 0, j)),
        out_shape=jax.ShapeDtypeStruct((depth, c8.shape[0], n), F32),
        compiler_params=_params("arbitrary", "arbitrary"),
        name="ada",
    )(c8, w, b.reshape(depth, 1, n))


class _Stream:
    def __init__(self, arrays, nzt):
        self.arrays, self.nzt = tuple(arrays), nzt
        self.split = len(self.arrays) == 2
        self.batch = self.arrays[0].shape[0]
        self.rows = sum(a.shape[1] for a in self.arrays)

    def specs(self, off=0):
        TM, D, nzt = ROW_TILE, D_MODEL, self.nzt
        if not self.split:
            return [pl.BlockSpec((None, TM, D), lambda b, i: (b, i + off, 0))]
        return [pl.BlockSpec((None, TM, D), lambda b, i: (b, jnp.minimum(i + off, nzt - 1), 0)),
                pl.BlockSpec((None, TM, D), lambda b, i: (b, jnp.maximum(i + off - nzt, 0), 0))]

    def read(self, refs, off=0):
        if not self.split:
            return refs[0][...]
        return jnp.where(pl.program_id(1) + off < self.nzt, refs[0][...], refs[1][...])


def _layer_spec(w, l, **kw):
    nd = w.ndim - 1
    return pl.BlockSpec((None,) + w.shape[1:], lambda *_: (l,) + (0,) * nd, **kw)


def _mod_spec(l, nzt, nb, off=0):
    return pl.BlockSpec((None, None, 6, D_MODEL), lambda b, i: (l, jnp.where(i + off < nzt, nb, b), 0, 0))


def _inproj_kernel(stream, *refs):
    ns = len(stream.arrays)
    mod_ref, g_ref, w_ref, o_gla, o_q, o_kv, o_rwkv = refs[ns:]
    m = mod_ref[...]
    h = _norm_mod(stream.read(refs[:ns]), g_ref[...], m[0:1], m[1:2]).astype(BF16)
    p = jnp.dot(h, w_ref[...], preferred_element_type=F32)
    o0 = 0
    for o_ref in (o_gla, o_q, o_kv, o_rwkv):
        wdt = o_ref.shape[-1]
        o_ref[...] = p[:, o0:o0 + wdt].astype(o_ref.dtype)
        o0 += wdt


def _inproj(stream, l, mod, g, w):
    B, L = stream.batch, stream.rows
    widths = (GLA_W, SWA_Q_W, SWA_KV_W, RWKV_W)
    return pl.pallas_call(
        functools.partial(_inproj_kernel, stream),
        grid=(B, L // ROW_TILE),
        in_specs=stream.specs() + [_mod_spec(l, stream.nzt, B), _layer_spec(g, l), _layer_spec(w, l)],
        out_specs=[pl.BlockSpec((None, ROW_TILE, wd), lambda b, i: (b, i, 0)) for wd in widths],
        out_shape=[jax.ShapeDtypeStruct((B, L, wd), dt) for wd, dt in zip(widths, (BF16, F32, F32, BF16))],
        compiler_params=_params("parallel", "parallel"),
        name="inproj",
    )(*stream.arrays, mod, g, w)


def _chunk_index(reverse, nzc, nc):
    if not reverse:
        return lambda c: c
    return lambda c: jnp.where(c < nzc, nzc - 1 - c, nzc + nc - 1 - c)


def _order_masks(n, reverse):
    ii = lax.broadcasted_iota(jnp.int32, (n, n), 0)
    jj = lax.broadcasted_iota(jnp.int32, (n, n), 1)
    strict = (jj > ii) if reverse else (jj < ii)
    return strict, strict | (ii == jj), ii == jj


def _block_order_mask(n, chunk, reverse):
    ii = lax.broadcasted_iota(jnp.int32, (n, n), 0)
    jj = lax.broadcasted_iota(jnp.int32, (n, n), 1)
    return ((ii // chunk) == (jj // chunk)) & ((jj >= ii) if reverse else (jj <= ii))


def _mm_01(m01, x):
    hi = x.astype(BF16)
    r1 = x - hi.astype(F32)
    mid = r1.astype(BF16)
    lo = (r1 - mid.astype(F32)).astype(BF16)
    dot = lambda piece: jnp.dot(m01, piece, preferred_element_type=F32)
    return dot(hi) + dot(mid) + dot(lo)


def _chunk_row(x, chunk, r):
    parts = [jnp.broadcast_to(x[j + r:j + r + 1], (chunk, x.shape[1])) for j in range(0, x.shape[0], chunk)]
    return parts[0] if len(parts) == 1 else jnp.concatenate(parts, axis=0)


def _chunk_totals(x, chunk):
    parts = [jnp.broadcast_to(jnp.sum(x[j:j + chunk], axis=0, keepdims=True), (chunk, x.shape[1]))
             for j in range(0, x.shape[0], chunk)]
    return parts[0] if len(parts) == 1 else jnp.concatenate(parts, axis=0)


def _pair_order_masks(n, reverse):
    ii = lax.broadcasted_iota(jnp.int32, (n, LANE), 0)
    jj = lax.broadcasted_iota(jnp.int32, (n, LANE), 1) & (PAIR_HEAD - 1)
    strict = (jj > ii) if reverse else (jj < ii)
    return strict, strict | (ii == jj), ii == jj


def _first_head(width, head_w):
    return (lax.broadcasted_iota(jnp.int32, (1, width), 1) & head_w) == 0


def _block_diag(z, first):
    return jnp.concatenate([jnp.where(first, z, 0.0), jnp.where(first, 0.0, z)], axis=0)


def _diag_blocks(rows, head_w_cols, cols):
    ri = lax.broadcasted_iota(jnp.int32, (rows, cols), 0) // (rows // 2)
    ci = lax.broadcasted_iota(jnp.int32, (rows, cols), 1) // head_w_cols
    return ri == ci


def _row_vector_to_column(row):
    n = row.shape[1]
    eye = lax.broadcasted_iota(jnp.int32, (n, n), 0) == lax.broadcasted_iota(jnp.int32, (n, n), 1)
    return jnp.sum(jnp.where(eye, row, 0.0), axis=1, keepdims=True)


def _gla_kernel(reverse, final, *refs):
    if final:
        p_ref, wgk_ref, bgk_ref, oprev_ref, ggla_ref, o_ref, h_ref, att_ref, cum_ref = refs
    else:
        p_ref, wgk_ref, bgk_ref, o_ref, h_ref, att_ref, cum_ref = refs

    @pl.when(pl.program_id(1) == 0)
    def _():
        h_ref[...] = jnp.zeros_like(h_ref)

    NB, R = p_ref.shape[0], p_ref.shape[1]
    C = SCAN_CHUNK
    nsub = R // C
    order = tuple(reversed(range(nsub))) if reverse else tuple(range(nsub))
    nk = GLA_HEADS * GLA_DK
    npair = GLA_HEADS // 2
    kw, vw = 2 * GLA_DK, 2 * GLA_DV
    incl2 = _pair_order_masks(C, reverse)[1]
    first_k = _first_head(kw, GLA_DK)
    first_v = _first_head(vw, GLA_DV)
    diag = _diag_blocks(kw, GLA_DV, vw)
    inclb = _block_order_mask(R, C, reverse).astype(BF16)
    q_s, q_t, q_m, k_t, k_e, e_tot, v, og = {}, {}, {}, {}, {}, {}, {}, {}
    span = None
    for nb in range(NB):
        p = p_ref[nb].astype(F32)
        q = p[:, 0:nk] * GLA_DK ** -0.5
        k = p[:, nk:2 * nk]
        v[nb] = p[:, 2 * nk:2 * nk + MIX_W]
        og[nb] = p[:, 2 * nk + MIX_W:2 * nk + 2 * MIX_W]
        gk = p[:, 2 * nk + 2 * MIX_W:]
        lg = -_softplus(-(_mm(gk, wgk_ref[...]) + bgk_ref[...])) / GLA_GATE_NORM
        cum = _mm_01(inclb, lg)
        tot = _chunk_totals(lg, C)
        mid = _chunk_row(cum, C, C // 2)
        rel = cum - mid
        part = jnp.max(jnp.abs(rel))
        span = part if span is None else jnp.maximum(span, part)
        cum_ref[nb] = cum
        q_s[nb] = q
        q_t[nb] = q * jnp.exp(cum)
        q_m[nb] = q * jnp.exp(rel)
        k_t[nb] = k * jnp.exp(-rel)
        k_e[nb] = k * jnp.exp(tot - cum)
        e_tot[nb] = jnp.exp(tot)
    ksl = lambda pr: slice(pr * kw, (pr + 1) * kw)
    vsl = lambda pr: slice(pr * vw, (pr + 1) * vw)
    rows = lambda j: slice(j * C, (j + 1) * C)
    chains = [(nb, pr) for nb in range(NB) for pr in range(npair)]
    units = [(j, nb, pr) for j in order for (nb, pr) in chains]
    qp = {(j, nb, pr): q_t[nb][rows(j), ksl(pr)] for (j, nb, pr) in units}
    vp = {(j, nb, pr): v[nb][rows(j), vsl(pr)] for (j, nb, pr) in units}
    kv = {(j, nb, pr): jnp.where(diag, _mm_tn(k_e[nb][rows(j), ksl(pr)], vp[(j, nb, pr)]), 0.0) for (j, nb, pr) in units}
    dec = {(j, nb, pr): _row_vector_to_column(e_tot[nb][j * C:j * C + 1, ksl(pr)]) for (j, nb, pr) in units}
    bdv = {u: _block_diag(vp[u], first_v) for u in units}
    safe = span <= GLA_SPLIT_EXP_LIMIT

    @pl.when(safe)
    def _():
        for idx, (j, nb, pr) in enumerate(units):
            att_ref[idx] = jnp.where(incl2, _mm_nt(q_m[nb][rows(j), ksl(pr)],
                                                   _block_diag(k_t[nb][rows(j), ksl(pr)], first_k)), 0.0)

    @pl.when(jnp.logical_not(safe))
    def _():
        tok = lax.broadcasted_iota(jnp.int32, (C, 1), 0)
        lane = lax.broadcasted_iota(jnp.int32, (1, kw), 1)
        for idx, (j, nb, pr) in enumerate(units):
            q_i = q_s[nb][rows(j), ksl(pr)]
            c_i = cum_ref[nb, rows(j), ksl(pr)]

            def columns(tile, acc, j=j, nb=nb, pr=pr, q_i=q_i, c_i=c_i):
                r0 = pl.multiple_of(j * C + tile * BF16_ROWS, BF16_ROWS)
                k_rows = p_ref[nb, pl.ds(r0, BF16_ROWS), nk + pr * kw:nk + (pr + 1) * kw].astype(F32)
                c_rows = cum_ref[nb, pl.ds(r0, BF16_ROWS), ksl(pr)]
                for r in range(BF16_ROWS):
                    t = tile * BF16_ROWS + r
                    x = q_i * k_rows[r:r + 1] * jnp.exp(jnp.minimum(c_i - c_rows[r:r + 1], 0.0))
                    s_a = jnp.sum(jnp.where(first_k, x, 0.0), axis=1, keepdims=True)
                    s_b = jnp.sum(jnp.where(first_k, 0.0, x), axis=1, keepdims=True)
                    seen = (tok <= t) if reverse else (tok >= t)
                    acc = jnp.where(lane == t, jnp.where(seen, s_a, 0.0), acc)
                    acc = jnp.where(lane == t + GLA_DK, jnp.where(seen, s_b, 0.0), acc)
                return acc

            att_ref[idx] = lax.fori_loop(0, C // BF16_ROWS, columns, jnp.zeros((C, kw), F32))

    att = {u: att_ref[idx] for idx, u in enumerate(units)}
    lhs = {u: jnp.concatenate([att[u], qp[u]], axis=1) for u in units}
    hs = {ch: h_ref[ch[0], ch[1]] for ch in chains}
    outs = {}
    for j in order:
        for ch in chains:
            u = (j,) + ch
            outs[u] = _mm(lhs[u], jnp.concatenate([bdv[u], hs[ch]], axis=0))
        for ch in chains:
            u = (j,) + ch
            hs[ch] = dec[u] * hs[ch] + kv[u]
    for ch in chains:
        h_ref[ch[0], ch[1]] = hs[ch]
    for nb in range(NB):
        o = jnp.concatenate([jnp.concatenate([outs[(j, nb, pr)] for pr in range(npair)], axis=1) for j in range(nsub)], axis=0)
        if final:
            o = o + oprev_ref[nb].astype(F32)
            g = ggla_ref[...]
            ys = []
            for h in range(GLA_HEADS):
                oh = o[:, h * GLA_DV:(h + 1) * GLA_DV]
                ys.append(oh * lax.rsqrt(jnp.mean(oh * oh, axis=-1, keepdims=True) + NORM_EPS) * g)
            o = jnp.concatenate(ys, axis=1) * _silu(og[nb])
        o_ref[nb] = o.astype(o_ref.dtype)


def _scan_batch(B, most=2):
    return max(n for n in (1, 2, 4) if n <= most and B % n == 0)


def _gla_scan(p, wgk, bgk, nzc, reverse, oprev=None, ggla=None):
    B, L, _ = p.shape
    C = SCAN_CHUNK * GLA_SUB
    NB = _scan_batch(B, GLA_SCAN_BATCH)
    nc = L // C
    nzc = nzc // GLA_SUB
    cidx = _chunk_index(reverse, nzc, nc)
    tok = lambda wd: pl.BlockSpec((NB, C, wd), lambda b, c: (b, cidx(c), 0))
    final = oprev is not None
    ins = [p, wgk, bgk]
    specs = [tok(GLA_W), _full(wgk.shape), _full(bgk.shape)]
    if final:
        ins += [oprev, ggla]
        specs += [tok(MIX_W), _full(ggla.shape)]
    return pl.pallas_call(
        functools.partial(_gla_kernel, reverse, final),
        grid=(B // NB, nc),
        in_specs=specs,
        out_specs=tok(MIX_W),
        out_shape=jax.ShapeDtypeStruct((B, L, MIX_W), F32 if final else BF16),
        scratch_shapes=[pltpu.VMEM((NB, GLA_HEADS // 2, 2 * GLA_DK, 2 * GLA_DV), F32),
                        pltpu.VMEM((GLA_SUB * NB * (GLA_HEADS // 2), SCAN_CHUNK, 2 * GLA_DK), F32),
                        pltpu.VMEM((NB, C, GLA_HEADS * GLA_DK), F32)],
        compiler_params=_params("parallel", "arbitrary"),
        name="gla_bwd" if reverse else "gla_fwd",
    )(*ins)


def _swa_kernel(nzb, nblk, q_ref, kv_ref, cs_ref, sink_ref, o_ref):
    BL = SWA_BLOCK
    nper = q_ref.shape[0] // BL
    kvw = SWA_KV_HEADS * SWA_HEAD_DIM
    qw = SWA_Q_HEADS * SWA_HEAD_DIM
    nloc = 3 * BL
    lz = nzb * BL
    nkeys = nloc + lz

    def rope(x, xs, cs, reps):
        cos = jnp.concatenate([cs[:, :LANE]] * reps, axis=1)
        sin = jnp.concatenate([cs[:, LANE:]] * reps, axis=1)
        return x * cos + xs * sin

    kvz = kv_ref[0:lz, :]
    q, keys, vals, bias = {}, {}, {}, {}
    for sub in range(nper):
        n = pl.program_id(1) * nper + sub
        qq = q_ref[sub * BL:(sub + 1) * BL, :]
        q0 = pl.multiple_of(n * BL, BL)
        q[sub] = rope(qq[:, :qw], qq[:, qw:], cs_ref[pl.ds(q0, BL), :], qw // LANE) * (SWA_HEAD_DIM ** -0.5 * LOG2E)
        k0 = pl.multiple_of(jnp.clip((n - 1) * BL, 0, (nblk - 3) * BL), BL)
        kv = kv_ref[pl.ds(k0, nloc), :]
        keys[sub] = jnp.concatenate([rope(kv[:, :kvw], kv[:, kvw:2 * kvw], cs_ref[pl.ds(k0, nloc), :], kvw // LANE),
                                     kvz[:, :kvw]], axis=0)
        vals[sub] = jnp.concatenate([kv[:, 2 * kvw:], kvz[:, 2 * kvw:]], axis=0)
        qpos = n * BL + lax.broadcasted_iota(jnp.int32, (BL, nkeys), 0)
        col = lax.broadcasted_iota(jnp.int32, (BL, nkeys), 1)
        kpos = k0 + col
        lo = jnp.where(n >= nzb, lz, nblk * BL)
        band = (jnp.abs(kpos - qpos) <= SWA_WINDOW) & (kpos >= lo)
        bias[sub] = jnp.where(band | (col >= nloc), 0.0, NEG_INF)
    chains = [(sub, h) for sub in range(nper) for h in range(SWA_Q_HEADS)]
    kvh = lambda h: slice((h // SWA_GROUP) * SWA_HEAD_DIM, (h // SWA_GROUP + 1) * SWA_HEAD_DIM)
    s = {(sub, h): _mm_nt(q[sub][:, h * SWA_HEAD_DIM:(h + 1) * SWA_HEAD_DIM], keys[sub][:, kvh(h)]) + bias[sub]
         for (sub, h) in chains}
    sk = [sink_ref[h] * LOG2E for h in range(SWA_Q_HEADS)]
    m = {c: jnp.maximum(jnp.max(s[c], axis=-1, keepdims=True), sk[c[1]]) for c in chains}
    e = {c: jnp.exp2(s[c] - m[c]) for c in chains}
    den = {c: jnp.sum(e[c], axis=-1, keepdims=True) + jnp.exp2(sk[c[1]] - m[c]) for c in chains}
    outs = {c: _mm(e[c], vals[c[0]][:, kvh(c[1])]) / den[c] for c in chains}
    for sub in range(nper):
        o_ref[sub * BL:(sub + 1) * BL, :] = jnp.concatenate([outs[(sub, h)] for h in range(SWA_Q_HEADS)], axis=1)


def _swa(pq, pkv, cs, sink, nzb):
    B, L, _ = pq.shape
    BL = SWA_BLOCK
    nblk = L // BL
    nper = 2 if nblk % 2 == 0 else 1
    assert nblk >= 3
    return pl.pallas_call(
        functools.partial(_swa_kernel, nzb, nblk),
        grid=(B, nblk // nper),
        in_specs=[pl.BlockSpec((None, nper * BL, SWA_Q_W), lambda b, n: (b, n, 0)),
                  pl.BlockSpec((None, L, SWA_KV_W), lambda b, n: (b, 0, 0)),
                  _full(cs.shape),
                  pl.BlockSpec(memory_space=pltpu.SMEM)],
        out_specs=pl.BlockSpec((None, nper * BL, MIX_W), lambda b, n: (b, n, 0)),
        out_shape=jax.ShapeDtypeStruct((B, L, MIX_W), F32),
        compiler_params=_params("parallel", "parallel"),
        name="swa",
    )(pq, pkv, cs, sink)


def _rwkv_feat_kernel(nzt, p_ref, pp_ref, pn_ref, vec_ref, ww_ref, wa_ref, wg_ref, e_ref,
                      o_r, o_v, o_kk, o_g, o_gb, o_lwf, o_kf, o_bf, o_lwb, o_kb, o_bb):
    i = pl.program_id(1)
    n = pl.num_programs(1)
    TM = p_ref.shape[0]
    first = (i == 0) | (i == nzt)
    last = (i == nzt - 1) | (i == n - 1)
    p_bf = p_ref[...]
    p_ext = jnp.concatenate([pp_ref[...], p_bf, pn_ref[...]], axis=0)
    row = lax.broadcasted_iota(jnp.int32, (TM, TM + 2 * HALO), 0)
    col = lax.broadcasted_iota(jnp.int32, (TM, TM + 2 * HALO), 1)
    lo = jnp.where(first, HALO, 0)
    hi = jnp.where(last, TM + HALO - 1, TM + 2 * HALO - 1)
    nbr = ((col == row + (HALO - 1)) & (col >= lo)) | ((col == row + (HALO + 1)) & (col <= hi))
    around = jnp.dot(jnp.where(nbr, 1.0, 0.0).astype(BF16), p_ext, preferred_element_type=F32)
    p = p_bf.astype(F32)
    vec = vec_ref[...]
    mu = jnp.concatenate([vec[8 + j:9 + j] for j in range(4)], axis=1)[:, :RWKV_W]
    pm = p * (1.0 - mu) + (0.5 * mu) * around
    W = MIX_W
    r, k, v = pm[:, :W], pm[:, W:2 * W], pm[:, 2 * W:3 * W]
    wa = pm[:, 3 * W:3 * W + LANE]
    gl = pm[:, 3 * W + LANE:]
    e = e_ref[...]
    k_k, k_a, r_k = vec[4:5], vec[5:6], vec[6:7]
    g = _mm(_sigmoid(gl), wg_ref[...])
    kkn = k * k_k
    kk = kkn * lax.rsqrt(jnp.maximum(_segsum(kkn * kkn, e), 1e-24))
    twa = jnp.tanh(wa)
    ksum = None
    for d, (o_lw, o_k, o_b) in enumerate(((o_lwf, o_kf, o_bf), (o_lwb, o_kb, o_bb))):
        a = _sigmoid(vec[2 + d:3 + d] + _mm(wa, wa_ref[d]))
        kd = k * (1.0 + (a - 1.0) * k_a)
        o_lw[...] = -DECAY_SCALE * _sigmoid(vec[d:d + 1] + _mm(twa, ww_ref[d]))
        o_k[...] = kd.astype(o_k.dtype)
        o_b[...] = (kk * a).astype(o_b.dtype)
        ksum = kd if ksum is None else ksum + kd
    o_r[...] = r.astype(o_r.dtype)
    o_v[...] = v.astype(o_v.dtype)
    o_kk[...] = kk.astype(o_kk.dtype)
    o_g[...] = g.astype(o_g.dtype)
    o_gb[...] = (_segsum(r * ksum * r_k, e) * v * g).astype(o_gb.dtype)


def _rwkv_feat(p, vec, ww, wa, wg, e, nzt):
    B, L, _ = p.shape
    TM = ROW_TILE
    rh = TM // HALO
    tok = pl.BlockSpec((None, TM, RWKV_W), lambda b, i: (b, i, 0))
    halo_p = pl.BlockSpec((None, HALO, RWKV_W), lambda b, i: (b, jnp.maximum(i * rh - 1, 0), 0))
    halo_n = pl.BlockSpec((None, HALO, RWKV_W), lambda b, i: (b, jnp.minimum((i + 1) * rh, L // HALO - 1), 0))
    out = pl.BlockSpec((None, TM, MIX_W), lambda b, i: (b, i, 0))
    dts = [BF16] * 5 + [F32, BF16, BF16] * 2
    return pl.pallas_call(
        functools.partial(_rwkv_feat_kernel, nzt),
        grid=(B, L // TM),
        in_specs=[tok, halo_p, halo_n, _full(vec.shape), _full(ww.shape), _full(wa.shape), _full(wg.shape), _full(e.shape)],
        out_specs=[out] * 11,
        out_shape=[jax.ShapeDtypeStruct((B, L, MIX_W), dt) for dt in dts],
        compiler_params=_params("parallel", "parallel"),
        name="rwkv_feat",
    )(p, p, p, vec, ww, wa, wg, e)


def _rwkv_scan_kernel(reverse, final, *refs):
    if final:
        r_ref, v_ref, kk_ref, lw_ref, k_ref, b_ref, yprev_ref, g_ref, gb_ref, gn_ref, o_ref, h_ref = refs
    else:
        r_ref, v_ref, kk_ref, lw_ref, k_ref, b_ref, o_ref, h_ref = refs

    @pl.when(pl.program_id(1) == 0)
    def _():
        h_ref[...] = jnp.zeros_like(h_ref)

    NB, R = lw_ref.shape[0], lw_ref.shape[1]
    C = SCAN_CHUNK
    nsub = R // C
    order = tuple(reversed(range(nsub))) if reverse else tuple(range(nsub))
    PW = LANE
    npair = RWKV_HEADS // 2
    strict2, incl2, ident2 = _pair_order_masks(C, reverse)
    first = _first_head(PW, RWKV_HEAD)
    diag = _diag_blocks(PW, RWKV_HEAD, PW)
    inclb = _block_order_mask(R, C, reverse).astype(BF16)
    kap_t, r_t, k_t, b_t, k_e, b_e, e_tot, v_all = {}, {}, {}, {}, {}, {}, {}, {}
    for nb in range(NB):
        lw = lw_ref[nb]
        cum = _mm_01(inclb, lw)
        tot = _chunk_totals(lw, C)
        e_inv = jnp.exp(-cum)
        e_end = jnp.exp(tot - cum)
        e_tot[nb] = jnp.exp(tot)
        kap_t[nb] = kk_ref[nb].astype(F32) * jnp.exp(cum - lw)
        r_t[nb] = r_ref[nb].astype(F32) * jnp.exp(cum)
        k_all, b_all = k_ref[nb].astype(F32), b_ref[nb].astype(F32)
        k_t[nb], b_t[nb] = k_all * e_inv, b_all * e_inv
        k_e[nb], b_e[nb] = k_all * e_end, b_all * e_end
        v_all[nb] = v_ref[nb].astype(F32)
    sl = lambda pr: slice(pr * PW, (pr + 1) * PW)
    rows = lambda j: slice(j * C, (j + 1) * C)
    pick = lambda d, u: d[u[1]][rows(u[0]), sl(u[2])]
    chains = [(nb, pr) for nb in range(NB) for pr in range(npair)]
    units = [(j,) + ch for j in order for ch in chains]
    vp = {u: pick(v_all, u) for u in units}
    G = {u: _mm_nt(jnp.concatenate([pick(kap_t, u), pick(r_t, u)], axis=0),
                   jnp.concatenate([_block_diag(pick(b_t, u), first), _block_diag(pick(k_t, u), first)], axis=0))
         for u in units}
    a_ab = {u: jnp.where(strict2, G[u][:C, :PW], 0.0) for u in units}
    a_ak = {u: jnp.where(strict2, G[u][:C, PW:], 0.0) for u in units}
    a_rb = {u: jnp.where(incl2, G[u][C:, :PW], 0.0) for u in units}
    a_rk = {u: jnp.where(incl2, G[u][C:, PW:], 0.0) for u in units}
    eye2 = jnp.where(ident2, 1.0, 0.0)
    t_inv = {u: eye2 - a_ab[u] for u in units}
    pw = {u: _mm(a_ab[u], _block_diag(a_ab[u], first)) for u in units}
    span = 2
    while span < C:
        span *= 2
        if span < C:
            prod = {u: _mm(jnp.concatenate([t_inv[u], pw[u]], axis=0), _block_diag(pw[u], first)) for u in units}
            t_inv = {u: t_inv[u] + prod[u][:C] for u in units}
            pw = {u: prod[u][C:] for u in units}
        else:
            t_inv = {u: t_inv[u] + _mm(t_inv[u], _block_diag(pw[u], first)) for u in units}
    bdv = {u: _block_diag(vp[u], first) for u in units}
    ulhs = {u: jnp.concatenate([a_ak[u], pick(kap_t, u)], axis=1) for u in units}
    ylhs = {u: jnp.concatenate([a_rk[u], -a_rb[u], pick(r_t, u)], axis=1) for u in units}
    hlhs = {u: jnp.concatenate([pick(k_e, u), -pick(b_e, u)], axis=0) for u in units}
    dec = {u: _row_vector_to_column(e_tot[u[1]][u[0] * C:u[0] * C + 1, sl(u[2])]) for u in units}
    hs = {ch: h_ref[ch[0], ch[1]] for ch in chains}
    ys = {}
    for j in order:
        units_j = [(j,) + ch for ch in chains]
        rhs = {u: _mm(ulhs[u], jnp.concatenate([bdv[u], hs[u[1:]]], axis=0)) for u in units_j}
        u_p = {u: _mm(t_inv[u], _block_diag(rhs[u], first)) for u in units_j}
        for u in units_j:
            ys[u] = _mm(ylhs[u], jnp.concatenate([bdv[u], _block_diag(u_p[u], first), hs[u[1:]]], axis=0))
        for u in units_j:
            upd = _mm_tn(hlhs[u], jnp.concatenate([vp[u], u_p[u]], axis=0))
            hs[u[1:]] = dec[u] * hs[u[1:]] + jnp.where(diag, upd, 0.0)
    for ch in chains:
        h_ref[ch[0], ch[1]] = hs[ch]
    for nb in range(NB):
        if final:
            yprev = yprev_ref[nb]
            outs = []
            for pr in range(npair):
                y = jnp.concatenate([ys[(j, nb, pr)] for j in range(nsub)], axis=0) + yprev[:, sl(pr)]
                s_a = jnp.sum(jnp.where(first, y, 0.0), axis=-1, keepdims=True)
                s_b = jnp.sum(jnp.where(first, 0.0, y), axis=-1, keepdims=True)
                yc = y - jnp.where(first, s_a, s_b) * (1.0 / RWKV_HEAD)
                q = yc * yc
                v_a = jnp.sum(jnp.where(first, q, 0.0), axis=-1, keepdims=True)
                v_b = jnp.sum(jnp.where(first, 0.0, q), axis=-1, keepdims=True)
                outs.append(yc * lax.rsqrt(jnp.where(first, v_a, v_b) * (1.0 / RWKV_HEAD) + RWKV_GN_EPS))
            gn = gn_ref[...]
            o_ref[nb] = (jnp.concatenate(outs, axis=1) * gn[0:1] + gn[1:2]) * g_ref[nb].astype(F32) + gb_ref[nb].astype(F32)
        else:
            o_ref[nb] = jnp.concatenate(
                [jnp.concatenate([ys[(j, nb, pr)] for pr in range(npair)], axis=1) for j in range(nsub)], axis=0)


def _rwkv_scan(r, v, kk, lw, k, b, nzc, reverse, yprev=None, g=None, gb=None, gn=None):
    B, L, _ = r.shape
    C = SCAN_CHUNK * SCAN_SUB
    NB = _scan_batch(B, RWKV_SCAN_BATCH)
    nc = L // C
    nzc = nzc // SCAN_SUB
    cidx = _chunk_index(reverse, nzc, nc)
    tok = pl.BlockSpec((NB, C, MIX_W), lambda b_, c: (b_, cidx(c), 0))
    final = yprev is not None
    ins = [r, v, kk, lw, k, b]
    specs = [tok] * 6
    if final:
        ins += [yprev, g, gb, gn]
        specs += [tok] * 3 + [_full(gn.shape)]
    return pl.pallas_call(
        functools.partial(_rwkv_scan_kernel, reverse, final),
        grid=(B // NB, nc),
        in_specs=specs,
        out_specs=tok,
        out_shape=jax.ShapeDtypeStruct((B, L, MIX_W), F32),
        scratch_shapes=[pltpu.VMEM((NB, RWKV_HEADS // 2, LANE, LANE), F32)],
        compiler_params=_params("parallel", "arbitrary"),
        name="rwkv_bwd" if reverse else "rwkv_fwd",
    )(*ins)


def _merge_kernel(stream, off, *refs):
    ns = len(stream.arrays)
    mod_ref, g_ref, ya_ref, yb_ref, yc_ref, wg_ref, wb_ref, wo_ref, o_ref = refs[ns:]
    m = mod_ref[...]
    s = stream.read(refs[:ns], off)
    h = _norm_mod(s, g_ref[...], m[0:1], m[1:2]).astype(BF16)
    gates = jnp.dot(h, wg_ref[...], preferred_element_type=F32)
    acc = None
    for i, y_ref in enumerate((ya_ref, yb_ref, yc_ref)):
        t = _mm(y_ref[...], wb_ref[i]) / (1.0 + jnp.exp(-gates[:, i * D_MODEL:(i + 1) * D_MODEL]))
        acc = t if acc is None else acc + t
    o_ref[...] = s + m[2:3] * _mm(acc, wo_ref[...])


def _merge(stream, l, mod, g, ya, yb, yc, wg, wb, wo, off):
    B, L = stream.batch, stream.rows
    TM = ROW_TILE
    nt = L // TM - off
    tok_in = lambda wd: pl.BlockSpec((None, TM, wd), lambda b, i: (b, i + off, 0))
    return pl.pallas_call(
        functools.partial(_merge_kernel, stream, off),
        grid=(B, nt),
        in_specs=stream.specs(off) + [_mod_spec(l, stream.nzt, B, off), _layer_spec(g, l),
                                      tok_in(MIX_W), tok_in(MIX_W), tok_in(MIX_W),
                                      _layer_spec(wg, l), _layer_spec(wb, l), _layer_spec(wo, l)],
        out_specs=pl.BlockSpec((None, TM, D_MODEL), lambda b, i: (b, i, 0)),
        out_shape=jax.ShapeDtypeStruct((B, nt * TM, D_MODEL), F32),
        compiler_params=_params("parallel", "parallel"),
        name="merge",
    )(*stream.arrays, mod, g, ya, yb, yc, wg, wb, wo)


def _ffn_kernel(final, s_ref, mod_ref, g_ref, w1_ref, w3_ref, w2_ref, gf_ref, o_ref):
    m = mod_ref[...]
    s = s_ref[...]
    h = _norm_mod(s, g_ref[...], m[3:4], m[4:5]).astype(BF16)
    a = jnp.dot(h, w1_ref[...], preferred_element_type=F32)
    b = jnp.dot(h, w3_ref[...], preferred_element_type=F32)
    o = s + m[5:6] * _mm(_silu(a) * b, w2_ref[...])
    if final:
        o = o * lax.rsqrt(jnp.mean(o * o, axis=-1, keepdims=True) + NORM_EPS) * gf_ref[...]
    o_ref[...] = o


def _ffn(s, l, mod, g, w1, w3, w2, gf, nzt, off, final):
    B, L, _ = s.shape
    TM = ROW_TILE
    nt = L // TM
    once = pl.Buffered(1)
    return pl.pallas_call(
        functools.partial(_ffn_kernel, final),
        grid=(B, nt),
        in_specs=[pl.BlockSpec((None, TM, D_MODEL), lambda b, i: (b, i, 0)), _mod_spec(l, nzt, B, off), _layer_spec(g, l),
                  _layer_spec(w1, l, pipeline_mode=once), _layer_spec(w3, l, pipeline_mode=once),
                  _layer_spec(w2, l, pipeline_mode=once), _full((1, D_MODEL))],
        out_specs=pl.BlockSpec((None, TM, D_MODEL), lambda b, i: (b, i, 0)),
        out_shape=jax.ShapeDtypeStruct((B, L, D_MODEL), F32),
        compiler_params=_params("parallel", "parallel"),
        name="ffn",
    )(s, mod, g, w1, w3, w2, gf)


def _rope_tables(T, lz):
    rows = T // GRID_W
    row = jnp.repeat(jnp.arange(rows), GRID_W).astype(F32)
    col = jnp.tile(jnp.arange(GRID_W), rows).astype(F32)
    inv = ROPE_BASE ** (-jnp.arange(ROPE_FREQS, dtype=F32) / ROPE_FREQS)
    ar, ac = row[:, None] * inv, col[:, None] * inv
    cos = jnp.concatenate([jnp.cos(ar), jnp.cos(ar), jnp.cos(ac), jnp.cos(ac)], axis=1)
    sin = jnp.concatenate([-jnp.sin(ar), jnp.sin(ar), -jnp.sin(ac), jnp.sin(ac)], axis=1)
    reps = LANE // SWA_HEAD_DIM
    x_tab = jnp.concatenate([jnp.tile(cos, (1, reps)), jnp.tile(sin, (1, reps))], axis=1)
    z_tab = jnp.concatenate([jnp.ones((lz, LANE), F32), jnp.zeros((lz, LANE), F32)], axis=1)
    return jnp.concatenate([z_tab, x_tab], axis=0)


def _partner_cols(w):
    return w.reshape(w.shape[:-1] + (-1, 2, ROPE_FREQS))[..., ::-1, :].reshape(w.shape)


def _mixer_weight(w_in):
    nk = GLA_HEADS * GLA_DK
    a = 2 * nk + MIX_W
    r2 = 2 * GLA_GATE_RANK
    head, gk, tail = w_in[..., :a].astype(BF16), w_in[..., a:a + r2].astype(BF16), w_in[..., a + r2:].astype(BF16)
    qw = SWA_Q_HEADS * SWA_HEAD_DIM
    kvw = SWA_KV_HEADS * SWA_HEAD_DIM
    o = np.cumsum([0, MIX_W, qw, kvw, kvw, RWKV_W])
    og, q, k, v, rwkv = (tail[..., o[i]:o[i + 1]] for i in range(5))
    pad = jnp.zeros(w_in.shape[:-1] + (LANE - r2,), BF16)
    w = jnp.concatenate([head, og, gk, pad, q, _partner_cols(q), k, _partner_cols(k), v, rwkv], axis=-1)
    return w, tail[..., o[5]:]


def _pad_rows(w, top, total):
    return jnp.concatenate([jnp.zeros((top, w.shape[1]), F32), w, jnp.zeros((total - top - w.shape[0], w.shape[1]), F32)], axis=0)


def kernel(x, c, ctx, c_ctx, w_ada, b_ada, g_mix, g_ffn, w_in, w_gk2, b_gk, g_gla, sink, mu_shift, w0, w_w2, a0, w_a2,
           w_g2, k_k, k_a, r_k, gn_w, gn_b, w_branch, w_out, w_ffn1, w_ffn3, w_ffn2, g_final):
    B, T, D = x.shape
    lz = ctx.shape[1]
    depth = w_in.shape[0]
    assert D == D_MODEL and lz % ROW_TILE == 0 and T % ROW_TILE == 0 and T % GRID_W == 0
    nzt, nzc, nzb = lz // ROW_TILE, lz // SCAN_CHUNK, lz // SWA_BLOCK
    cs = _rope_tables(T, lz)
    c8 = jnp.concatenate([c, c_ctx[None], jnp.zeros((8 - (B + 1) % 8 if (B + 1) % 8 else 0, D), F32)], axis=0)
    hidx = np.arange(MIX_W) // RWKV_HEAD
    seg = jnp.asarray(hidx[:, None] == hidx[None, :], BF16)
    mod = _ada(c8, w_ada, b_ada).reshape(depth, c8.shape[0], 6, D)
    w_mix, w_gate = _mixer_weight(w_in)
    w_branch, w_out = w_branch.astype(BF16), w_out.astype(BF16)
    w_ffn1, w_ffn3, w_ffn2 = w_ffn1.astype(BF16), w_ffn3.astype(BF16), w_ffn2.astype(BF16)
    g_mix, g_ffn = g_mix[:, None], g_ffn[:, None]
    stream = _Stream((ctx, x), nzt)
    out = None
    for l in range(depth):
        last = l == depth - 1
        p_gla, p_q, p_kv, p_rwkv = _inproj(stream, l, mod, g_mix, w_mix)

        wgk = [_pad_rows(w_gk2[l, d], d * GLA_GATE_RANK, LANE).astype(BF16) for d in range(2)]
        o_f = _gla_scan(p_gla, wgk[0], b_gk[l, 0][None], nzc, False)
        ya = _gla_scan(p_gla, wgk[1], b_gk[l, 1][None], nzc, True, o_f, g_gla[l][None])

        yb = _swa(p_q, p_kv, cs, sink[l], nzb)

        zero = jnp.zeros((MIX_W,), F32)
        mu4 = jnp.concatenate([mu_shift[l], jnp.zeros((4 * MIX_W - RWKV_W,), F32)]).reshape(4, MIX_W)
        vec = jnp.stack([w0[l, 0], w0[l, 1], a0[l, 0], a0[l, 1], k_k[l], k_a[l], r_k[l].reshape(-1), zero], axis=0)
        vec = jnp.concatenate([vec, mu4, jnp.zeros((4, MIX_W), F32)], axis=0)
        ww = jnp.stack([_pad_rows(w_w2[l, d], 0, LANE) for d in range(2)]).astype(BF16)
        wa = jnp.stack([_pad_rows(w_a2[l, d], RWKV_DECAY_RANK, LANE) for d in range(2)]).astype(BF16)
        f_r, f_v, f_kk, f_g, f_gb, lw_f, k_f, b_f, lw_b, k_b, b_b = _rwkv_feat(
            p_rwkv, vec, ww, wa, w_g2[l].astype(BF16), seg, nzt)
        y_f = _rwkv_scan(f_r, f_v, f_kk, lw_f, k_f, b_f, nzc, False)
        yc = _rwkv_scan(f_r, f_v, f_kk, lw_b, k_b, b_b, nzc, True, y_f, f_g, f_gb, jnp.stack([gn_w[l], gn_b[l]]))

        off = nzt if last else 0
        s_mix = _merge(stream, l, mod, g_mix, ya, yb, yc, w_gate, w_branch, w_out, off)
        out = _ffn(s_mix, l, mod, g_ffn, w_ffn1, w_ffn3, w_ffn2, g_final[None], nzt, off, last)
        stream = _Stream((out,), nzt)
    return out
```

```python
import functools

import jax
import jax.numpy as jnp
import numpy as np
from jax import lax
from jax.experimental import pallas as pl
from jax.experimental.pallas import tpu as pltpu

F32 = jnp.float32
BF16 = jnp.bfloat16
LOG2E = 1.4426950408889634
DECAY_SCALE = 0.6065306597126334

D_MODEL = 1024
GRID_W = 64
NORM_EPS = 1e-6
NEG_INF = -1e30
MIX_W = D_MODEL // 2
N_BRANCH = 3
GLA_HEADS = 4
GLA_DV = MIX_W // GLA_HEADS
GLA_DK = GLA_DV // 2
GLA_GATE_RANK = 16
GLA_GATE_NORM = 16.0
SWA_HEAD_DIM = 64
SWA_Q_HEADS = MIX_W // SWA_HEAD_DIM
SWA_KV_HEADS = SWA_Q_HEADS // 4
SWA_GROUP = SWA_Q_HEADS // SWA_KV_HEADS
SWA_WINDOW = 128
SWA_BLOCK = 128
ROPE_FREQS = SWA_HEAD_DIM // 4
ROPE_BASE = 10000.0
RWKV_HEAD = 64
RWKV_HEADS = MIX_W // RWKV_HEAD
RWKV_DECAY_RANK = 64
RWKV_A_RANK = 64
RWKV_GATE_RANK = 128
RWKV_GN_EPS = 64e-5
FFN_HIDDEN = -(-8 * D_MODEL // (3 * 256)) * 256
GLA_SPLIT = (GLA_HEADS * GLA_DK, GLA_HEADS * GLA_DK, MIX_W, GLA_GATE_RANK, GLA_GATE_RANK, MIX_W)
SWA_SPLIT = (SWA_Q_HEADS * SWA_HEAD_DIM, SWA_KV_HEADS * SWA_HEAD_DIM, SWA_KV_HEADS * SWA_HEAD_DIM)
RWKV_SPLIT = (MIX_W, MIX_W, MIX_W, RWKV_DECAY_RANK, RWKV_A_RANK, RWKV_GATE_RANK)
GROUP_SPLIT = (sum(GLA_SPLIT), sum(SWA_SPLIT), sum(RWKV_SPLIT), N_BRANCH * D_MODEL)

LANE = 128
BF16_ROWS = 16
ROW_TILE = 256
HALO = BF16_ROWS
SCAN_CHUNK = 64
SCAN_SUB = 2
GLA_SUB = 4
RWKV_SCAN_BATCH = 4
GLA_SCAN_BATCH = 4
GLA_SPLIT_EXP_LIMIT = 80.0
PAIR_HEAD = LANE // 2
assert GLA_DK == PAIR_HEAD and RWKV_HEAD == PAIR_HEAD and SCAN_CHUNK == PAIR_HEAD
GLA_W = 2 * GLA_HEADS * GLA_DK + 2 * MIX_W + LANE
SWA_Q_W = 1024
SWA_KV_W = 384
RWKV_W = sum(RWKV_SPLIT)
MIXER_W = GLA_W + SWA_Q_W + SWA_KV_W + RWKV_W
VMEM_LIMIT = 56 * 1024 * 1024


def _mm(a, b):
    return jnp.dot(a.astype(BF16), b.astype(BF16), preferred_element_type=F32)


def _mm_nt(a, b):
    return lax.dot_general(a.astype(BF16), b.astype(BF16), (((1,), (1,)), ((), ())), preferred_element_type=F32)


def _mm_tn(a, b):
    return lax.dot_general(a.astype(BF16), b.astype(BF16), (((0,), (0,)), ((), ())), preferred_element_type=F32)


def _sigmoid(x):
    return 0.5 * jnp.tanh(0.5 * x) + 0.5


def _silu(x):
    return x * _sigmoid(x)


def _softplus(x):
    return jnp.maximum(x, 0.0) + jnp.log(1.0 + jnp.exp(-jnp.abs(x)))


def _norm_mod(x, g, shift, scale):
    y = x * lax.rsqrt(jnp.mean(x * x, axis=-1, keepdims=True) + NORM_EPS)
    return (y * g) * (1.0 + scale) + shift


def _segsum(x, e):
    return jnp.dot(x.astype(BF16), e, preferred_element_type=F32)


def _params(*sem):
    return pltpu.CompilerParams(dimension_semantics=sem, vmem_limit_bytes=VMEM_LIMIT)


def _full(shape):
    nd = len(shape)
    return pl.BlockSpec(shape, lambda *_: (0,) * nd)


def _split_bf16(x):
    hi = x.astype(BF16)
    return hi, (x - hi.astype(F32)).astype(BF16)


def _ada_kernel(c_ref, w_ref, b_ref, o_ref):
    s_hi, s_lo = _split_bf16(_silu(c_ref[...]))
    w_hi, w_lo = _split_bf16(w_ref[...])
    dot = lambda a, b: jnp.dot(a, b, preferred_element_type=F32)
    o_ref[...] = dot(s_hi, w_hi) + dot(s_lo, w_hi) + dot(s_hi, w_lo) + b_ref[...]


def _ada(c8, w, b):
    depth, _, n = w.shape
    tn = 1536
    return pl.pallas_call(
        _ada_kernel,
        grid=(depth, n // tn),
        in_specs=[_full(c8.shape), pl.BlockSpec((None, D_MODEL, tn), lambda l, j: (l, 0, j)),
                  pl.BlockSpec((None, 1, tn), lambda l, j: (l, 0, j))],
        out_specs=pl.BlockSpec((None, c8.shape[0], tn), lambda l, j: (l, 0, j)),
        out_shape=jax.ShapeDtypeStruct((depth, c8.shape[0], n), F32),
        compiler_params=_params("arbitrary", "arbitrary"),
        name="ada",
    )(c8, w, b.reshape(depth, 1, n))


class _Stream:
    def __init__(self, arrays, nzt):
        self.arrays, self.nzt = tuple(arrays), nzt
        self.split = len(self.arrays) == 2
        self.batch = self.arrays[0].shape[0]
        self.rows = sum(a.shape[1] for a in self.arrays)
        self.ns = 2 if self.batch % 2 == 0 else 1

    def specs(self, off=0):
        blk, nzt = (self.ns, ROW_TILE, D_MODEL), self.nzt
        if not self.split:
            return [pl.BlockSpec(blk, lambda b, i: (b, i + off, 0))]
        return [pl.BlockSpec(blk, lambda b, i: (b, jnp.minimum(i + off, nzt - 1), 0)),
                pl.BlockSpec(blk, lambda b, i: (b, jnp.maximum(i + off - nzt, 0), 0))]

    def read(self, refs, off=0):
        if not self.split:
            return refs[0][...]
        return jnp.where(pl.program_id(1) + off < self.nzt, refs[0][...], refs[1][...])


def _layer_spec(w, l, **kw):
    nd = w.ndim - 1
    return pl.BlockSpec((None,) + w.shape[1:], lambda *_: (l,) + (0,) * nd, **kw)


def _mod_specs(l, nzt, nb, ns, off=0):
    return [pl.BlockSpec((None, None, 6, D_MODEL),
                         lambda b, i, s=s: (l, jnp.where(i + off < nzt, nb, b * ns + s), 0, 0)) for s in range(ns)]


def _tok_spec(ns, width, off=0):
    return pl.BlockSpec((ns, ROW_TILE, width), lambda b, i: (b, i + off, 0))


def _inproj_kernel(stream, *refs):
    na, ns = len(stream.arrays), stream.ns
    mod_refs = refs[na:na + ns]
    g_ref, w_ref, o_gla, o_q, o_kv, o_rwkv = refs[na + ns:]
    x = stream.read(refs[:na])
    h = []
    for s in range(ns):
        m = mod_refs[s][...]
        h.append(_norm_mod(x[s], g_ref[...], m[0:1], m[1:2]).astype(BF16))
    p = [jnp.dot(h[s], w_ref[...], preferred_element_type=F32) for s in range(ns)]
    for s in range(ns):
        o0 = 0
        for o_ref in (o_gla, o_q, o_kv, o_rwkv):
            wdt = o_ref.shape[-1]
            o_ref[s] = p[s][:, o0:o0 + wdt].astype(o_ref.dtype)
            o0 += wdt


def _inproj(stream, l, mod, g, w):
    B, L, ns = stream.batch, stream.rows, stream.ns
    widths = (GLA_W, SWA_Q_W, SWA_KV_W, RWKV_W)
    return pl.pallas_call(
        functools.partial(_inproj_kernel, stream),
        grid=(B // ns, L // ROW_TILE),
        in_specs=stream.specs() + _mod_specs(l, stream.nzt, B, ns) + [_layer_spec(g, l), _layer_spec(w, l)],
        out_specs=[_tok_spec(ns, wd) for wd in widths],
        out_shape=[jax.ShapeDtypeStruct((B, L, wd), dt) for wd, dt in zip(widths, (BF16, F32, F32, BF16))],
        compiler_params=_params("parallel", "parallel"),
        name="inproj",
    )(*stream.arrays, *([mod] * ns), g, w)


def _chunk_index(reverse, nzc, nc):
    if not reverse:
        return lambda c: c
    return lambda c: jnp.where(c < nzc, nzc - 1 - c, nzc + nc - 1 - c)


def _order_masks(n, reverse):
    ii = lax.broadcasted_iota(jnp.int32, (n, n), 0)
    jj = lax.broadcasted_iota(jnp.int32, (n, n), 1)
    strict = (jj > ii) if reverse else (jj < ii)
    return strict, strict | (ii == jj), ii == jj


def _block_order_mask(n, chunk, reverse):
    ii = lax.broadcasted_iota(jnp.int32, (n, n), 0)
    jj = lax.broadcasted_iota(jnp.int32, (n, n), 1)
    return ((ii // chunk) == (jj // chunk)) & ((jj >= ii) if reverse else (jj <= ii))


def _mm_01(m01, x):
    hi = x.astype(BF16)
    r1 = x - hi.astype(F32)
    mid = r1.astype(BF16)
    lo = (r1 - mid.astype(F32)).astype(BF16)
    dot = lambda piece: jnp.dot(m01, piece, preferred_element_type=F32)
    return dot(hi) + dot(mid) + dot(lo)


def _chunk_row(x, chunk, r):
    parts = [jnp.broadcast_to(x[j + r:j + r + 1], (chunk, x.shape[1])) for j in range(0, x.shape[0], chunk)]
    return parts[0] if len(parts) == 1 else jnp.concatenate(parts, axis=0)


def _chunk_totals(x, chunk):
    parts = [jnp.broadcast_to(jnp.sum(x[j:j + chunk], axis=0, keepdims=True), (chunk, x.shape[1]))
             for j in range(0, x.shape[0], chunk)]
    return parts[0] if len(parts) == 1 else jnp.concatenate(parts, axis=0)


def _pair_order_masks(n, reverse):
    ii = lax.broadcasted_iota(jnp.int32, (n, LANE), 0)
    jj = lax.broadcasted_iota(jnp.int32, (n, LANE), 1) & (PAIR_HEAD - 1)
    strict = (jj > ii) if reverse else (jj < ii)
    return strict, strict | (ii == jj), ii == jj


def _first_head(width, head_w):
    return (lax.broadcasted_iota(jnp.int32, (1, width), 1) & head_w) == 0


def _block_diag(z, first):
    return jnp.concatenate([jnp.where(first, z, 0.0), jnp.where(first, 0.0, z)], axis=0)


def _diag_blocks(rows, head_w_cols, cols):
    ri = lax.broadcasted_iota(jnp.int32, (rows, cols), 0) // (rows // 2)
    ci = lax.broadcasted_iota(jnp.int32, (rows, cols), 1) // head_w_cols
    return ri == ci


def _row_vector_to_column(row):
    n = row.shape[1]
    eye = lax.broadcasted_iota(jnp.int32, (n, n), 0) == lax.broadcasted_iota(jnp.int32, (n, n), 1)
    return jnp.sum(jnp.where(eye, row, 0.0), axis=1, keepdims=True)


def _gla_kernel(reverse, final, *refs):
    if final:
        p_ref, wgk_ref, bgk_ref, oprev_ref, ggla_ref, o_ref, h_ref, att_ref, cum_ref = refs
    else:
        p_ref, wgk_ref, bgk_ref, o_ref, h_ref, att_ref, cum_ref = refs

    @pl.when(pl.program_id(1) == 0)
    def _():
        h_ref[...] = jnp.zeros_like(h_ref)

    NB, R = p_ref.shape[0], p_ref.shape[1]
    C = SCAN_CHUNK
    nsub = R // C
    order = tuple(reversed(range(nsub))) if reverse else tuple(range(nsub))
    nk = GLA_HEADS * GLA_DK
    npair = GLA_HEADS // 2
    kw, vw = 2 * GLA_DK, 2 * GLA_DV
    incl2 = _pair_order_masks(C, reverse)[1]
    first_k = _first_head(kw, GLA_DK)
    first_v = _first_head(vw, GLA_DV)
    diag = _diag_blocks(kw, GLA_DV, vw)
    inclb = _block_order_mask(R, C, reverse).astype(BF16)
    q_s, q_t, q_m, k_t, k_e, e_tot, v, og = {}, {}, {}, {}, {}, {}, {}, {}
    span = None
    for nb in range(NB):
        p = p_ref[nb].astype(F32)
        q = p[:, 0:nk] * GLA_DK ** -0.5
        k = p[:, nk:2 * nk]
        v[nb] = p[:, 2 * nk:2 * nk + MIX_W]
        og[nb] = p[:, 2 * nk + MIX_W:2 * nk + 2 * MIX_W]
        gk = p[:, 2 * nk + 2 * MIX_W:]
        lg = -_softplus(-(_mm(gk, wgk_ref[...]) + bgk_ref[...])) / GLA_GATE_NORM
        cum = _mm_01(inclb, lg)
        tot = _chunk_totals(lg, C)
        mid = _chunk_row(cum, C, C // 2)
        rel = cum - mid
        part = jnp.max(jnp.abs(rel))
        span = part if span is None else jnp.maximum(span, part)
        cum_ref[nb] = cum
        q_s[nb] = q
        q_t[nb] = q * jnp.exp(cum)
        q_m[nb] = q * jnp.exp(rel)
        k_t[nb] = k * jnp.exp(-rel)
        k_e[nb] = k * jnp.exp(tot - cum)
        e_tot[nb] = jnp.exp(tot)
    ksl = lambda pr: slice(pr * kw, (pr + 1) * kw)
    vsl = lambda pr: slice(pr * vw, (pr + 1) * vw)
    rows = lambda j: slice(j * C, (j + 1) * C)
    chains = [(nb, pr) for nb in range(NB) for pr in range(npair)]
    units = [(j, nb, pr) for j in order for (nb, pr) in chains]
    qp = {(j, nb, pr): q_t[nb][rows(j), ksl(pr)] for (j, nb, pr) in units}
    vp = {(j, nb, pr): v[nb][rows(j), vsl(pr)] for (j, nb, pr) in units}
    kv = {(j, nb, pr): jnp.where(diag, _mm_tn(k_e[nb][rows(j), ksl(pr)], vp[(j, nb, pr)]), 0.0) for (j, nb, pr) in units}
    dec = {(j, nb, pr): _row_vector_to_column(e_tot[nb][j * C:j * C + 1, ksl(pr)]) for (j, nb, pr) in units}
    bdv = {u: _block_diag(vp[u], first_v) for u in units}
    safe = span <= GLA_SPLIT_EXP_LIMIT

    @pl.when(safe)
    def _():
        for idx, (j, nb, pr) in enumerate(units):
            att_ref[idx] = jnp.where(incl2, _mm_nt(q_m[nb][rows(j), ksl(pr)],
                                                   _block_diag(k_t[nb][rows(j), ksl(pr)], first_k)), 0.0)

    @pl.when(jnp.logical_not(safe))
    def _():
        tok = lax.broadcasted_iota(jnp.int32, (C, 1), 0)
        lane = lax.broadcasted_iota(jnp.int32, (1, kw), 1)
        for idx, (j, nb, pr) in enumerate(units):
            q_i = q_s[nb][rows(j), ksl(pr)]
            c_i = cum_ref[nb, rows(j), ksl(pr)]

            def columns(tile, acc, j=j, nb=nb, pr=pr, q_i=q_i, c_i=c_i):
                r0 = pl.multiple_of(j * C + tile * BF16_ROWS, BF16_ROWS)
                k_rows = p_ref[nb, pl.ds(r0, BF16_ROWS), nk + pr * kw:nk + (pr + 1) * kw].astype(F32)
                c_rows = cum_ref[nb, pl.ds(r0, BF16_ROWS), ksl(pr)]
                for r in range(BF16_ROWS):
                    t = tile * BF16_ROWS + r
                    x = q_i * k_rows[r:r + 1] * jnp.exp(jnp.minimum(c_i - c_rows[r:r + 1], 0.0))
                    s_a = jnp.sum(jnp.where(first_k, x, 0.0), axis=1, keepdims=True)
                    s_b = jnp.sum(jnp.where(first_k, 0.0, x), axis=1, keepdims=True)
                    seen = (tok <= t) if reverse else (tok >= t)
                    acc = jnp.where(lane == t, jnp.where(seen, s_a, 0.0), acc)
                    acc = jnp.where(lane == t + GLA_DK, jnp.where(seen, s_b, 0.0), acc)
                return acc

            att_ref[idx] = lax.fori_loop(0, C // BF16_ROWS, columns, jnp.zeros((C, kw), F32))

    att = {u: att_ref[idx] for idx, u in enumerate(units)}
    lhs = {u: jnp.concatenate([att[u], qp[u]], axis=1) for u in units}
    hs = {ch: h_ref[ch[0], ch[1]] for ch in chains}
    outs = {}
    for j in order:
        for ch in chains:
            u = (j,) + ch
            outs[u] = _mm(lhs[u], jnp.concatenate([bdv[u], hs[ch]], axis=0))
        for ch in chains:
            u = (j,) + ch
            hs[ch] = dec[u] * hs[ch] + kv[u]
    for ch in chains:
        h_ref[ch[0], ch[1]] = hs[ch]
    for nb in range(NB):
        o = jnp.concatenate([jnp.concatenate([outs[(j, nb, pr)] for pr in range(npair)], axis=1) for j in range(nsub)], axis=0)
        if final:
            o = o + oprev_ref[nb].astype(F32)
            g = ggla_ref[...]
            ys = []
            for h in range(GLA_HEADS):
                oh = o[:, h * GLA_DV:(h + 1) * GLA_DV]
                ys.append(oh * lax.rsqrt(jnp.mean(oh * oh, axis=-1, keepdims=True) + NORM_EPS) * g)
            o = jnp.concatenate(ys, axis=1) * _silu(og[nb])
        o_ref[nb] = o.astype(o_ref.dtype)


def _scan_batch(B, most=2):
    return max(n for n in (1, 2, 4) if n <= most and B % n == 0)


def _gla_scan(p, wgk, bgk, nzc, reverse, oprev=None, ggla=None):
    B, L, _ = p.shape
    C = SCAN_CHUNK * GLA_SUB
    NB = _scan_batch(B, GLA_SCAN_BATCH)
    nc = L // C
    nzc = nzc // GLA_SUB
    cidx = _chunk_index(reverse, nzc, nc)
    tok = lambda wd: pl.BlockSpec((NB, C, wd), lambda b, c: (b, cidx(c), 0))
    final = oprev is not None
    ins = [p, wgk, bgk]
    specs = [tok(GLA_W), _full(wgk.shape), _full(bgk.shape)]
    if final:
        ins += [oprev, ggla]
        specs += [tok(MIX_W), _full(ggla.shape)]
    return pl.pallas_call(
        functools.partial(_gla_kernel, reverse, final),
        grid=(B // NB, nc),
        in_specs=specs,
        out_specs=tok(MIX_W),
        out_shape=jax.ShapeDtypeStruct((B, L, MIX_W), F32 if final else BF16),
        scratch_shapes=[pltpu.VMEM((NB, GLA_HEADS // 2, 2 * GLA_DK, 2 * GLA_DV), F32),
                        pltpu.VMEM((GLA_SUB * NB * (GLA_HEADS // 2), SCAN_CHUNK, 2 * GLA_DK), F32),
                        pltpu.VMEM((NB, C, GLA_HEADS * GLA_DK), F32)],
        compiler_params=_params("parallel", "arbitrary"),
        name="gla_bwd" if reverse else "gla_fwd",
    )(*ins)


def _swa_kernel(nzb, nblk, q_ref, kv_ref, cs_ref, sink_ref, o_ref):
    BL = SWA_BLOCK
    nper = q_ref.shape[0] // BL
    kvw = SWA_KV_HEADS * SWA_HEAD_DIM
    qw = SWA_Q_HEADS * SWA_HEAD_DIM
    nloc = 3 * BL
    lz = nzb * BL
    nkeys = nloc + lz

    def rope(x, xs, cs, reps):
        cos = jnp.concatenate([cs[:, :LANE]] * reps, axis=1)
        sin = jnp.concatenate([cs[:, LANE:]] * reps, axis=1)
        return x * cos + xs * sin

    kvz = kv_ref[0:lz, :]
    q, keys, vals, bias = {}, {}, {}, {}
    for sub in range(nper):
        n = pl.program_id(1) * nper + sub
        qq = q_ref[sub * BL:(sub + 1) * BL, :]
        q0 = pl.multiple_of(n * BL, BL)
        q[sub] = rope(qq[:, :qw], qq[:, qw:], cs_ref[pl.ds(q0, BL), :], qw // LANE) * (SWA_HEAD_DIM ** -0.5 * LOG2E)
        k0 = pl.multiple_of(jnp.clip((n - 1) * BL, 0, (nblk - 3) * BL), BL)
        kv = kv_ref[pl.ds(k0, nloc), :]
        keys[sub] = jnp.concatenate([rope(kv[:, :kvw], kv[:, kvw:2 * kvw], cs_ref[pl.ds(k0, nloc), :], kvw // LANE),
                                     kvz[:, :kvw]], axis=0)
        vals[sub] = jnp.concatenate([kv[:, 2 * kvw:], kvz[:, 2 * kvw:]], axis=0)
        qpos = n * BL + lax.broadcasted_iota(jnp.int32, (BL, nkeys), 0)
        col = lax.broadcasted_iota(jnp.int32, (BL, nkeys), 1)
        kpos = k0 + col
        lo = jnp.where(n >= nzb, lz, nblk * BL)
        band = (jnp.abs(kpos - qpos) <= SWA_WINDOW) & (kpos >= lo)
        bias[sub] = jnp.where(band | (col >= nloc), 0.0, NEG_INF)
    chains = [(sub, h) for sub in range(nper) for h in range(SWA_Q_HEADS)]
    kvh = lambda h: slice((h // SWA_GROUP) * SWA_HEAD_DIM, (h // SWA_GROUP + 1) * SWA_HEAD_DIM)
    s = {(sub, h): _mm_nt(q[sub][:, h * SWA_HEAD_DIM:(h + 1) * SWA_HEAD_DIM], keys[sub][:, kvh(h)]) + bias[sub]
         for (sub, h) in chains}
    sk = [sink_ref[h] * LOG2E for h in range(SWA_Q_HEADS)]
    m = {c: jnp.maximum(jnp.max(s[c], axis=-1, keepdims=True), sk[c[1]]) for c in chains}
    e = {c: jnp.exp2(s[c] - m[c]) for c in chains}
    den = {c: jnp.sum(e[c], axis=-1, keepdims=True) + jnp.exp2(sk[c[1]] - m[c]) for c in chains}
    outs = {c: _mm(e[c], vals[c[0]][:, kvh(c[1])]) / den[c] for c in chains}
    for sub in range(nper):
        o_ref[sub * BL:(sub + 1) * BL, :] = jnp.concatenate([outs[(sub, h)] for h in range(SWA_Q_HEADS)], axis=1)


def _swa(pq, pkv, cs, sink, nzb):
    B, L, _ = pq.shape
    BL = SWA_BLOCK
    nblk = L // BL
    nper = 2 if nblk % 2 == 0 else 1
    assert nblk >= 3
    return pl.pallas_call(
        functools.partial(_swa_kernel, nzb, nblk),
        grid=(B, nblk // nper),
        in_specs=[pl.BlockSpec((None, nper * BL, SWA_Q_W), lambda b, n: (b, n, 0)),
                  pl.BlockSpec((None, L, SWA_KV_W), lambda b, n: (b, 0, 0)),
                  _full(cs.shape),
                  pl.BlockSpec(memory_space=pltpu.SMEM)],
        out_specs=pl.BlockSpec((None, nper * BL, MIX_W), lambda b, n: (b, n, 0)),
        out_shape=jax.ShapeDtypeStruct((B, L, MIX_W), F32),
        compiler_params=_params("parallel", "parallel"),
        name="swa",
    )(pq, pkv, cs, sink)


def _rwkv_feat_kernel(nzt, p_ref, pp_ref, pn_ref, vec_ref, ww_ref, wa_ref, wg_ref, e_ref,
                      o_r, o_v, o_kk, o_g, o_gb, o_lwf, o_kf, o_bf, o_lwb, o_kb, o_bb):
    i = pl.program_id(1)
    n = pl.num_programs(1)
    TM = p_ref.shape[0]
    first = (i == 0) | (i == nzt)
    last = (i == nzt - 1) | (i == n - 1)
    p_bf = p_ref[...]
    p_ext = jnp.concatenate([pp_ref[...], p_bf, pn_ref[...]], axis=0)
    row = lax.broadcasted_iota(jnp.int32, (TM, TM + 2 * HALO), 0)
    col = lax.broadcasted_iota(jnp.int32, (TM, TM + 2 * HALO), 1)
    lo = jnp.where(first, HALO, 0)
    hi = jnp.where(last, TM + HALO - 1, TM + 2 * HALO - 1)
    nbr = ((col == row + (HALO - 1)) & (col >= lo)) | ((col == row + (HALO + 1)) & (col <= hi))
    around = jnp.dot(jnp.where(nbr, 1.0, 0.0).astype(BF16), p_ext, preferred_element_type=F32)
    p = p_bf.astype(F32)
    vec = vec_ref[...]
    mu = jnp.concatenate([vec[8 + j:9 + j] for j in range(4)], axis=1)[:, :RWKV_W]
    pm = p * (1.0 - mu) + (0.5 * mu) * around
    W = MIX_W
    r, k, v = pm[:, :W], pm[:, W:2 * W], pm[:, 2 * W:3 * W]
    wa = pm[:, 3 * W:3 * W + LANE]
    gl = pm[:, 3 * W + LANE:]
    e = e_ref[...]
    k_k, k_a, r_k = vec[4:5], vec[5:6], vec[6:7]
    g = _mm(_sigmoid(gl), wg_ref[...])
    kkn = k * k_k
    kk = kkn * lax.rsqrt(jnp.maximum(_segsum(kkn * kkn, e), 1e-24))
    twa = jnp.tanh(wa)
    ksum = None
    for d, (o_lw, o_k, o_b) in enumerate(((o_lwf, o_kf, o_bf), (o_lwb, o_kb, o_bb))):
        a = _sigmoid(vec[2 + d:3 + d] + _mm(wa, wa_ref[d]))
        kd = k * (1.0 + (a - 1.0) * k_a)
        o_lw[...] = -DECAY_SCALE * _sigmoid(vec[d:d + 1] + _mm(twa, ww_ref[d]))
        o_k[...] = kd.astype(o_k.dtype)
        o_b[...] = (kk * a).astype(o_b.dtype)
        ksum = kd if ksum is None else ksum + kd
    o_r[...] = r.astype(o_r.dtype)
    o_v[...] = v.astype(o_v.dtype)
    o_kk[...] = kk.astype(o_kk.dtype)
    o_g[...] = g.astype(o_g.dtype)
    o_gb[...] = (_segsum(r * ksum * r_k, e) * v * g).astype(o_gb.dtype)


def _rwkv_feat(p, vec, ww, wa, wg, e, nzt):
    B, L, _ = p.shape
    TM = ROW_TILE
    rh = TM // HALO
    tok = pl.BlockSpec((None, TM, RWKV_W), lambda b, i: (b, i, 0))
    halo_p = pl.BlockSpec((None, HALO, RWKV_W), lambda b, i: (b, jnp.maximum(i * rh - 1, 0), 0))
    halo_n = pl.BlockSpec((None, HALO, RWKV_W), lambda b, i: (b, jnp.minimum((i + 1) * rh, L // HALO - 1), 0))
    out = pl.BlockSpec((None, TM, MIX_W), lambda b, i: (b, i, 0))
    dts = [BF16] * 5 + [F32, BF16, BF16] * 2
    return pl.pallas_call(
        functools.partial(_rwkv_feat_kernel, nzt),
        grid=(B, L // TM),
        in_specs=[tok, halo_p, halo_n, _full(vec.shape), _full(ww.shape), _full(wa.shape), _full(wg.shape), _full(e.shape)],
        out_specs=[out] * 11,
        out_shape=[jax.ShapeDtypeStruct((B, L, MIX_W), dt) for dt in dts],
        compiler_params=_params("parallel", "parallel"),
        name="rwkv_feat",
    )(p, p, p, vec, ww, wa, wg, e)


def _rwkv_scan_kernel(reverse, final, *refs):
    if final:
        r_ref, v_ref, kk_ref, lw_ref, k_ref, b_ref, yprev_ref, g_ref, gb_ref, gn_ref, o_ref, h_ref = refs
    else:
        r_ref, v_ref, kk_ref, lw_ref, k_ref, b_ref, o_ref, h_ref = refs

    @pl.when(pl.program_id(1) == 0)
    def _():
        h_ref[...] = jnp.zeros_like(h_ref)

    NB, R = lw_ref.shape[0], lw_ref.shape[1]
    C = SCAN_CHUNK
    nsub = R // C
    order = tuple(reversed(range(nsub))) if reverse else tuple(range(nsub))
    PW = LANE
    npair = RWKV_HEADS // 2
    strict2, incl2, ident2 = _pair_order_masks(C, reverse)
    first = _first_head(PW, RWKV_HEAD)
    diag = _diag_blocks(PW, RWKV_HEAD, PW)
    inclb = _block_order_mask(R, C, reverse).astype(BF16)
    kap_t, r_t, k_t, b_t, k_e, b_e, e_tot, v_all = {}, {}, {}, {}, {}, {}, {}, {}
    for nb in range(NB):
        lw = lw_ref[nb]
        cum = _mm_01(inclb, lw)
        tot = _chunk_totals(lw, C)
        e_inv = jnp.exp(-cum)
        e_end = jnp.exp(tot - cum)
        e_tot[nb] = jnp.exp(tot)
        kap_t[nb] = kk_ref[nb].astype(F32) * jnp.exp(cum - lw)
        r_t[nb] = r_ref[nb].astype(F32) * jnp.exp(cum)
        k_all, b_all = k_ref[nb].astype(F32), b_ref[nb].astype(F32)
        k_t[nb], b_t[nb] = k_all * e_inv, b_all * e_inv
        k_e[nb], b_e[nb] = k_all * e_end, b_all * e_end
        v_all[nb] = v_ref[nb].astype(F32)
    sl = lambda pr: slice(pr * PW, (pr + 1) * PW)
    rows = lambda j: slice(j * C, (j + 1) * C)
    pick = lambda d, u: d[u[1]][rows(u[0]), sl(u[2])]
    chains = [(nb, pr) for nb in range(NB) for pr in range(npair)]
    units = [(j,) + ch for j in order for ch in chains]
    vp = {u: pick(v_all, u) for u in units}
    G = {u: _mm_nt(jnp.concatenate([pick(kap_t, u), pick(r_t, u)], axis=0),
                   jnp.concatenate([_block_diag(pick(b_t, u), first), _block_diag(pick(k_t, u), first)], axis=0))
         for u in units}
    a_ab = {u: jnp.where(strict2, G[u][:C, :PW], 0.0) for u in units}
    a_ak = {u: jnp.where(strict2, G[u][:C, PW:], 0.0) for u in units}
    a_rb = {u: jnp.where(incl2, G[u][C:, :PW], 0.0) for u in units}
    a_rk = {u: jnp.where(incl2, G[u][C:, PW:], 0.0) for u in units}
    eye2 = jnp.where(ident2, 1.0, 0.0)
    t_inv = {u: eye2 - a_ab[u] for u in units}
    pw = {u: _mm(a_ab[u], _block_diag(a_ab[u], first)) for u in units}
    span = 2
    while span < C:
        span *= 2
        if span < C:
            prod = {u: _mm(jnp.concatenate([t_inv[u], pw[u]], axis=0), _block_diag(pw[u], first)) for u in units}
            t_inv = {u: t_inv[u] + prod[u][:C] for u in units}
            pw = {u: prod[u][C:] for u in units}
        else:
            t_inv = {u: t_inv[u] + _mm(t_inv[u], _block_diag(pw[u], first)) for u in units}
    bdv = {u: _block_diag(vp[u], first) for u in units}
    ulhs = {u: jnp.concatenate([a_ak[u], pick(kap_t, u)], axis=1) for u in units}
    ylhs = {u: jnp.concatenate([a_rk[u], -a_rb[u], pick(r_t, u)], axis=1) for u in units}
    hlhs = {u: jnp.concatenate([pick(k_e, u), -pick(b_e, u)], axis=0) for u in units}
    dec = {u: _row_vector_to_column(e_tot[u[1]][u[0] * C:u[0] * C + 1, sl(u[2])]) for u in units}
    hs = {ch: h_ref[ch[0], ch[1]] for ch in chains}
    ys = {}
    for j in order:
        units_j = [(j,) + ch for ch in chains]
        rhs = {u: _mm(ulhs[u], jnp.concatenate([bdv[u], hs[u[1:]]], axis=0)) for u in units_j}
        u_p = {u: _mm(t_inv[u], _block_diag(rhs[u], first)) for u in units_j}
        for u in units_j:
            ys[u] = _mm(ylhs[u], jnp.concatenate([bdv[u], _block_diag(u_p[u], first), hs[u[1:]]], axis=0))
        for u in units_j:
            upd = _mm_tn(hlhs[u], jnp.concatenate([vp[u], u_p[u]], axis=0))
            hs[u[1:]] = dec[u] * hs[u[1:]] + jnp.where(diag, upd, 0.0)
    for ch in chains:
        h_ref[ch[0], ch[1]] = hs[ch]
    for nb in range(NB):
        if final:
            yprev = yprev_ref[nb]
            outs = []
            for pr in range(npair):
                y = jnp.concatenate([ys[(j, nb, pr)] for j in range(nsub)], axis=0) + yprev[:, sl(pr)]
                s_a = jnp.sum(jnp.where(first, y, 0.0), axis=-1, keepdims=True)
                s_b = jnp.sum(jnp.where(first, 0.0, y), axis=-1, keepdims=True)
                yc = y - jnp.where(first, s_a, s_b) * (1.0 / RWKV_HEAD)
                q = yc * yc
                v_a = jnp.sum(jnp.where(first, q, 0.0), axis=-1, keepdims=True)
                v_b = jnp.sum(jnp.where(first, 0.0, q), axis=-1, keepdims=True)
                outs.append(yc * lax.rsqrt(jnp.where(first, v_a, v_b) * (1.0 / RWKV_HEAD) + RWKV_GN_EPS))
            gn = gn_ref[...]
            o_ref[nb] = (jnp.concatenate(outs, axis=1) * gn[0:1] + gn[1:2]) * g_ref[nb].astype(F32) + gb_ref[nb].astype(F32)
        else:
            o_ref[nb] = jnp.concatenate(
                [jnp.concatenate([ys[(j, nb, pr)] for pr in range(npair)], axis=1) for j in range(nsub)], axis=0)


def _rwkv_scan(r, v, kk, lw, k, b, nzc, reverse, yprev=None, g=None, gb=None, gn=None):
    B, L, _ = r.shape
    C = SCAN_CHUNK * SCAN_SUB
    NB = _scan_batch(B, RWKV_SCAN_BATCH)
    nc = L // C
    nzc = nzc // SCAN_SUB
    cidx = _chunk_index(reverse, nzc, nc)
    tok = pl.BlockSpec((NB, C, MIX_W), lambda b_, c: (b_, cidx(c), 0))
    final = yprev is not None
    ins = [r, v, kk, lw, k, b]
    specs = [tok] * 6
    if final:
        ins += [yprev, g, gb, gn]
        specs += [tok] * 3 + [_full(gn.shape)]
    return pl.pallas_call(
        functools.partial(_rwkv_scan_kernel, reverse, final),
        grid=(B // NB, nc),
        in_specs=specs,
        out_specs=tok,
        out_shape=jax.ShapeDtypeStruct((B, L, MIX_W), F32),
        scratch_shapes=[pltpu.VMEM((NB, RWKV_HEADS // 2, LANE, LANE), F32)],
        compiler_params=_params("parallel", "arbitrary"),
        name="rwkv_bwd" if reverse else "rwkv_fwd",
    )(*ins)


def _merge_kernel(stream, off, *refs):
    na, ns = len(stream.arrays), stream.ns
    mod_refs = refs[na:na + ns]
    g_ref, ya_ref, yb_ref, yc_ref, wg_ref, wb_ref, wo_ref, o_ref = refs[na + ns:]
    x = stream.read(refs[:na], off)
    m = [mod_refs[s][...] for s in range(ns)]
    h = [_norm_mod(x[s], g_ref[...], m[s][0:1], m[s][1:2]).astype(BF16) for s in range(ns)]
    gates = [jnp.dot(h[s], wg_ref[...], preferred_element_type=F32) for s in range(ns)]
    acc = [None] * ns
    for i, y_ref in enumerate((ya_ref, yb_ref, yc_ref)):
        for s in range(ns):
            t = _mm(y_ref[s], wb_ref[i]) / (1.0 + jnp.exp(-gates[s][:, i * D_MODEL:(i + 1) * D_MODEL]))
            acc[s] = t if acc[s] is None else acc[s] + t
    for s in range(ns):
        o_ref[s] = x[s] + m[s][2:3] * _mm(acc[s], wo_ref[...])


def _merge(stream, l, mod, g, ya, yb, yc, wg, wb, wo, off):
    B, L, ns = stream.batch, stream.rows, stream.ns
    nt = L // ROW_TILE - off
    return pl.pallas_call(
        functools.partial(_merge_kernel, stream, off),
        grid=(B // ns, nt),
        in_specs=stream.specs(off) + _mod_specs(l, stream.nzt, B, ns, off) + [
            _layer_spec(g, l), _tok_spec(ns, MIX_W, off), _tok_spec(ns, MIX_W, off), _tok_spec(ns, MIX_W, off),
            _layer_spec(wg, l), _layer_spec(wb, l), _layer_spec(wo, l)],
        out_specs=_tok_spec(ns, D_MODEL),
        out_shape=jax.ShapeDtypeStruct((B, nt * ROW_TILE, D_MODEL), F32),
        compiler_params=_params("parallel", "parallel"),
        name="merge",
    )(*stream.arrays, *([mod] * ns), g, ya, yb, yc, wg, wb, wo)


def _ffn_kernel(final, ns, s_ref, *refs):
    mod_refs = refs[:ns]
    g_ref, w1_ref, w3_ref, w2_ref, gf_ref, o_ref = refs[ns:]
    m = [mod_refs[s][...] for s in range(ns)]
    x = [s_ref[s] for s in range(ns)]
    h = [_norm_mod(x[s], g_ref[...], m[s][3:4], m[s][4:5]).astype(BF16) for s in range(ns)]
    a = [jnp.dot(h[s], w1_ref[...], preferred_element_type=F32) for s in range(ns)]
    b = [jnp.dot(h[s], w3_ref[...], preferred_element_type=F32) for s in range(ns)]
    for s in range(ns):
        o = x[s] + m[s][5:6] * _mm(_silu(a[s]) * b[s], w2_ref[...])
        if final:
            o = o * lax.rsqrt(jnp.mean(o * o, axis=-1, keepdims=True) + NORM_EPS) * gf_ref[...]
        o_ref[s] = o


def _ffn(s, l, mod, g, w1, w3, w2, gf, nzt, off, final):
    B, L, _ = s.shape
    ns = 2 if B % 2 == 0 else 1
    once = pl.Buffered(1)
    return pl.pallas_call(
        functools.partial(_ffn_kernel, final, ns),
        grid=(B // ns, L // ROW_TILE),
        in_specs=[_tok_spec(ns, D_MODEL)] + _mod_specs(l, nzt, B, ns, off) + [
            _layer_spec(g, l), _layer_spec(w1, l, pipeline_mode=once), _layer_spec(w3, l, pipeline_mode=once),
            _layer_spec(w2, l, pipeline_mode=once), _full((1, D_MODEL))],
        out_specs=_tok_spec(ns, D_MODEL),
        out_shape=jax.ShapeDtypeStruct((B, L, D_MODEL), F32),
        compiler_params=_params("parallel", "parallel"),
        name="ffn",
    )(s, *([mod] * ns), g, w1, w3, w2, gf)


def _rope_tables(T, lz):
    rows = T // GRID_W
    row = jnp.repeat(jnp.arange(rows), GRID_W).astype(F32)
    col = jnp.tile(jnp.arange(GRID_W), rows).astype(F32)
    inv = ROPE_BASE ** (-jnp.arange(ROPE_FREQS, dtype=F32) / ROPE_FREQS)
    ar, ac = row[:, None] * inv, col[:, None] * inv
    cos = jnp.concatenate([jnp.cos(ar), jnp.cos(ar), jnp.cos(ac), jnp.cos(ac)], axis=1)
    sin = jnp.concatenate([-jnp.sin(ar), jnp.sin(ar), -jnp.sin(ac), jnp.sin(ac)], axis=1)
    reps = LANE // SWA_HEAD_DIM
    x_tab = jnp.concatenate([jnp.tile(cos, (1, reps)), jnp.tile(sin, (1, reps))], axis=1)
    z_tab = jnp.concatenate([jnp.ones((lz, LANE), F32), jnp.zeros((lz, LANE), F32)], axis=1)
    return jnp.concatenate([z_tab, x_tab], axis=0)


def _partner_cols(w):
    return w.reshape(w.shape[:-1] + (-1, 2, ROPE_FREQS))[..., ::-1, :].reshape(w.shape)


def _mixer_weight(w_in):
    nk = GLA_HEADS * GLA_DK
    a = 2 * nk + MIX_W
    r2 = 2 * GLA_GATE_RANK
    head, gk, tail = w_in[..., :a].astype(BF16), w_in[..., a:a + r2].astype(BF16), w_in[..., a + r2:].astype(BF16)
    qw = SWA_Q_HEADS * SWA_HEAD_DIM
    kvw = SWA_KV_HEADS * SWA_HEAD_DIM
    o = np.cumsum([0, MIX_W, qw, kvw, kvw, RWKV_W])
    og, q, k, v, rwkv = (tail[..., o[i]:o[i + 1]] for i in range(5))
    pad = jnp.zeros(w_in.shape[:-1] + (LANE - r2,), BF16)
    w = jnp.concatenate([head, og, gk, pad, q, _partner_cols(q), k, _partner_cols(k), v, rwkv], axis=-1)
    return w, tail[..., o[5]:]


def _pad_rows(w, top, total):
    return jnp.concatenate([jnp.zeros((top, w.shape[1]), F32), w, jnp.zeros((total - top - w.shape[0], w.shape[1]), F32)], axis=0)


def kernel(x, c, ctx, c_ctx, w_ada, b_ada, g_mix, g_ffn, w_in, w_gk2, b_gk, g_gla, sink, mu_shift, w0, w_w2, a0, w_a2,
           w_g2, k_k, k_a, r_k, gn_w, gn_b, w_branch, w_out, w_ffn1, w_ffn3, w_ffn2, g_final):
    B, T, D = x.shape
    lz = ctx.shape[1]
    depth = w_in.shape[0]
    assert D == D_MODEL and lz % ROW_TILE == 0 and T % ROW_TILE == 0 and T % GRID_W == 0
    nzt, nzc, nzb = lz // ROW_TILE, lz // SCAN_CHUNK, lz // SWA_BLOCK
    cs = _rope_tables(T, lz)
    c8 = jnp.concatenate([c, c_ctx[None], jnp.zeros((8 - (B + 1) % 8 if (B + 1) % 8 else 0, D), F32)], axis=0)
    hidx = np.arange(MIX_W) // RWKV_HEAD
    seg = jnp.asarray(hidx[:, None] == hidx[None, :], BF16)
    mod = _ada(c8, w_ada, b_ada).reshape(depth, c8.shape[0], 6, D)
    w_mix, w_gate = _mixer_weight(w_in)
    w_branch, w_out = w_branch.astype(BF16), w_out.astype(BF16)
    w_ffn1, w_ffn3, w_ffn2 = w_ffn1.astype(BF16), w_ffn3.astype(BF16), w_ffn2.astype(BF16)
    g_mix, g_ffn = g_mix[:, None], g_ffn[:, None]
    stream = _Stream((ctx, x), nzt)
    out = None
    for l in range(depth):
        last = l == depth - 1
        p_gla, p_q, p_kv, p_rwkv = _inproj(stream, l, mod, g_mix, w_mix)

        wgk = [_pad_rows(w_gk2[l, d], d * GLA_GATE_RANK, LANE).astype(BF16) for d in range(2)]
        o_f = _gla_scan(p_gla, wgk[0], b_gk[l, 0][None], nzc, False)
        ya = _gla_scan(p_gla, wgk[1], b_gk[l, 1][None], nzc, True, o_f, g_gla[l][None])

        yb = _swa(p_q, p_kv, cs, sink[l], nzb)

        zero = jnp.zeros((MIX_W,), F32)
        mu4 = jnp.concatenate([mu_shift[l], jnp.zeros((4 * MIX_W - RWKV_W,), F32)]).reshape(4, MIX_W)
        vec = jnp.stack([w0[l, 0], w0[l, 1], a0[l, 0], a0[l, 1], k_k[l], k_a[l], r_k[l].reshape(-1), zero], axis=0)
        vec = jnp.concatenate([vec, mu4, jnp.zeros((4, MIX_W), F32)], axis=0)
        ww = jnp.stack([_pad_rows(w_w2[l, d], 0, LANE) for d in range(2)]).astype(BF16)
        wa = jnp.stack([_pad_rows(w_a2[l, d], RWKV_DECAY_RANK, LANE) for d in range(2)]).astype(BF16)
        f_r, f_v, f_kk, f_g, f_gb, lw_f, k_f, b_f, lw_b, k_b, b_b = _rwkv_feat(
            p_rwkv, vec, ww, wa, w_g2[l].astype(BF16), seg, nzt)
        y_f = _rwkv_scan(f_r, f_v, f_kk, lw_f, k_f, b_f, nzc, False)
        yc = _rwkv_scan(f_r, f_v, f_kk, lw_b, k_b, b_b, nzc, True, y_f, f_g, f_gb, jnp.stack([gn_w[l], gn_b[l]]))

        off = nzt if last else 0
        s_mix = _merge(stream, l, mod, g_mix, ya, yb, yc, w_gate, w_branch, w_out, off)
        out = _ffn(s_mix, l, mod, g_ffn, w_ffn1, w_ffn3, w_ffn2, g_final[None], nzt, off, last)
        stream = _Stream((out,), nzt)
    return out
```

```python
import functools

import jax
import jax.numpy as jnp
import numpy as np
from jax import lax
from jax.experimental import pallas as pl
from jax.experimental.pallas import tpu as pltpu

F32 = jnp.float32
BF16 = jnp.bfloat16
LOG2E = 1.4426950408889634
DECAY_SCALE = 0.6065306597126334

D_MODEL = 1024
GRID_W = 64
NORM_EPS = 1e-6
NEG_INF = -1e30
MIX_W = D_MODEL // 2
N_BRANCH = 3
GLA_HEADS = 4
GLA_DV = MIX_W // GLA_HEADS
GLA_DK = GLA_DV // 2
GLA_GATE_RANK = 16
GLA_GATE_NORM = 16.0
SWA_HEAD_DIM = 64
SWA_Q_HEADS = MIX_W // SWA_HEAD_DIM
SWA_KV_HEADS = SWA_Q_HEADS // 4
SWA_GROUP = SWA_Q_HEADS // SWA_KV_HEADS
SWA_WINDOW = 128
SWA_BLOCK = 128
ROPE_FREQS = SWA_HEAD_DIM // 4
ROPE_BASE = 10000.0
RWKV_HEAD = 64
RWKV_HEADS = MIX_W // RWKV_HEAD
RWKV_DECAY_RANK = 64
RWKV_A_RANK = 64
RWKV_GATE_RANK = 128
RWKV_GN_EPS = 64e-5
FFN_HIDDEN = -(-8 * D_MODEL // (3 * 256)) * 256
GLA_SPLIT = (GLA_HEADS * GLA_DK, GLA_HEADS * GLA_DK, MIX_W, GLA_GATE_RANK, GLA_GATE_RANK, MIX_W)
SWA_SPLIT = (SWA_Q_HEADS * SWA_HEAD_DIM, SWA_KV_HEADS * SWA_HEAD_DIM, SWA_KV_HEADS * SWA_HEAD_DIM)
RWKV_SPLIT = (MIX_W, MIX_W, MIX_W, RWKV_DECAY_RANK, RWKV_A_RANK, RWKV_GATE_RANK)
GROUP_SPLIT = (sum(GLA_SPLIT), sum(SWA_SPLIT), sum(RWKV_SPLIT), N_BRANCH * D_MODEL)

LANE = 128
BF16_ROWS = 16
ROW_TILE = 256
HALO = BF16_ROWS
SCAN_CHUNK = 64
SCAN_SUB = 2
GLA_SUB = 4
RWKV_SCAN_BATCH = 4
GLA_SCAN_BATCH = 4
GLA_SPLIT_EXP_LIMIT = 80.0
PAIR_HEAD = LANE // 2
assert GLA_DK == PAIR_HEAD and RWKV_HEAD == PAIR_HEAD and SCAN_CHUNK == PAIR_HEAD
GLA_W = 2 * GLA_HEADS * GLA_DK + 2 * MIX_W + LANE
SWA_Q_W = 2 * SWA_Q_HEADS * SWA_HEAD_DIM
SWA_KV_W = 3 * SWA_KV_HEADS * SWA_HEAD_DIM
RWKV_W = sum(RWKV_SPLIT)
MIXER_W = GLA_W + SWA_Q_W + SWA_KV_W + RWKV_W
VMEM_LIMIT = 56 * 1024 * 1024


def _mm(a, b):
    return jnp.dot(a.astype(BF16), b.astype(BF16), preferred_element_type=F32)


def _mm_nt(a, b):
    return lax.dot_general(a.astype(BF16), b.astype(BF16), (((1,), (1,)), ((), ())), preferred_element_type=F32)


def _mm_tn(a, b):
    return lax.dot_general(a.astype(BF16), b.astype(BF16), (((0,), (0,)), ((), ())), preferred_element_type=F32)


def _sigmoid(x):
    return 0.5 * jnp.tanh(0.5 * x) + 0.5


def _silu(x):
    return x * _sigmoid(x)


def _softplus(x):
    return jnp.maximum(x, 0.0) + jnp.log(1.0 + jnp.exp(-jnp.abs(x)))


def _norm_mod(x, g, shift, scale):
    y = x * lax.rsqrt(jnp.mean(x * x, axis=-1, keepdims=True) + NORM_EPS)
    return (y * g) * (1.0 + scale) + shift


def _segsum(x, e):
    return jnp.dot(x.astype(BF16), e, preferred_element_type=F32)


def _params(*sem):
    return pltpu.CompilerParams(dimension_semantics=sem, vmem_limit_bytes=VMEM_LIMIT)


def _full(shape):
    nd = len(shape)
    return pl.BlockSpec(shape, lambda *_: (0,) * nd)


def _split_bf16(x):
    hi = x.astype(BF16)
    return hi, (x - hi.astype(F32)).astype(BF16)


def _ada_kernel(c_ref, w_ref, b_ref, o_ref):
    s_hi, s_lo = _split_bf16(_silu(c_ref[...]))
    w_hi, w_lo = _split_bf16(w_ref[...])
    dot = lambda a, b: jnp.dot(a, b, preferred_element_type=F32)
    o_ref[...] = dot(s_hi, w_hi) + dot(s_lo, w_hi) + dot(s_hi, w_lo) + b_ref[...]


def _ada(c8, w, b):
    depth, _, n = w.shape
    tn = 1536
    return pl.pallas_call(
        _ada_kernel,
        grid=(depth, n // tn),
        in_specs=[_full(c8.shape), pl.BlockSpec((None, D_MODEL, tn), lambda l, j: (l, 0, j)),
                  pl.BlockSpec((None, 1, tn), lambda l, j: (l, 0, j))],
        out_specs=pl.BlockSpec((None, c8.shape[0], tn), lambda l, j: (l, 0, j)),
        out_shape=jax.ShapeDtypeStruct((depth, c8.shape[0], n), F32),
        compiler_params=_params("arbitrary", "arbitrary"),
        name="ada",
    )(c8, w, b.reshape(depth, 1, n))


def _tile_batch(B):
    return 2 if B % 2 == 0 else 1


class _Stream:
    def __init__(self, arrays, nzt):
        self.arrays, self.nzt = tuple(arrays), nzt
        self.split = len(self.arrays) == 2
        self.batch = self.arrays[0].shape[0]
        self.rows = sum(a.shape[1] for a in self.arrays)
        self.ns = _tile_batch(self.batch)

    def specs(self, off=0):
        blk, nzt = (self.ns, ROW_TILE, D_MODEL), self.nzt
        if not self.split:
            return [pl.BlockSpec(blk, lambda b, i: (b, i + off, 0))]
        return [pl.BlockSpec(blk, lambda b, i: (b, jnp.minimum(i + off, nzt - 1), 0)),
                pl.BlockSpec(blk, lambda b, i: (b, jnp.maximum(i + off - nzt, 0), 0))]

    def read(self, refs, off=0):
        if not self.split:
            return refs[0][...]
        return jnp.where(pl.program_id(1) + off < self.nzt, refs[0][...], refs[1][...])


def _layer_spec(w, l, **kw):
    nd = w.ndim - 1
    return pl.BlockSpec((None,) + w.shape[1:], lambda *_: (l,) + (0,) * nd, **kw)


def _mod_specs(l, nzt, nb, ns, off=0):
    return [pl.BlockSpec((None, None, 6, D_MODEL),
                         lambda b, i, s=s: (l, jnp.where(i + off < nzt, nb, b * ns + s), 0, 0)) for s in range(ns)]


def _tok_spec(ns, width, off=0):
    return pl.BlockSpec((ns, ROW_TILE, width), lambda b, i: (b, i + off, 0))


def _inproj_kernel(stream, *refs):
    na, ns = len(stream.arrays), stream.ns
    mod_refs = refs[na:na + ns]
    g_ref, w_ref, o_gla, o_q, o_kv, o_rwkv = refs[na + ns:]
    x = stream.read(refs[:na])
    h = []
    for s in range(ns):
        m = mod_refs[s][...]
        h.append(_norm_mod(x[s], g_ref[...], m[0:1], m[1:2]).astype(BF16))
    p = [jnp.dot(h[s], w_ref[...], preferred_element_type=F32) for s in range(ns)]
    for s in range(ns):
        o0 = 0
        for o_ref in (o_gla, o_q, o_kv, o_rwkv):
            wdt = o_ref.shape[-1]
            o_ref[s] = p[s][:, o0:o0 + wdt].astype(o_ref.dtype)
            o0 += wdt


def _inproj(stream, l, mod, g, w):
    B, L, ns = stream.batch, stream.rows, stream.ns
    widths = (GLA_W, SWA_Q_W, SWA_KV_W, RWKV_W)
    return pl.pallas_call(
        functools.partial(_inproj_kernel, stream),
        grid=(B // ns, L // ROW_TILE),
        in_specs=stream.specs() + _mod_specs(l, stream.nzt, B, ns) + [_layer_spec(g, l), _layer_spec(w, l)],
        out_specs=[_tok_spec(ns, wd) for wd in widths],
        out_shape=[jax.ShapeDtypeStruct((B, L, wd), dt) for wd, dt in zip(widths, (BF16, F32, F32, BF16))],
        compiler_params=_params("parallel", "parallel"),
        name="inproj",
    )(*stream.arrays, *([mod] * ns), g, w)


def _chunk_index(reverse, nzc, nc):
    if not reverse:
        return lambda c: c
    return lambda c: jnp.where(c < nzc, nzc - 1 - c, nzc + nc - 1 - c)


def _block_order_mask(n, chunk, reverse):
    ii = lax.broadcasted_iota(jnp.int32, (n, n), 0)
    jj = lax.broadcasted_iota(jnp.int32, (n, n), 1)
    return ((ii // chunk) == (jj // chunk)) & ((jj >= ii) if reverse else (jj <= ii))


def _mm_01(m01, x):
    hi = x.astype(BF16)
    r1 = x - hi.astype(F32)
    mid = r1.astype(BF16)
    lo = (r1 - mid.astype(F32)).astype(BF16)
    dot = lambda piece: jnp.dot(m01, piece, preferred_element_type=F32)
    return dot(hi) + dot(mid) + dot(lo)


def _chunk_row(x, chunk, r):
    parts = [jnp.broadcast_to(x[j + r:j + r + 1], (chunk, x.shape[1])) for j in range(0, x.shape[0], chunk)]
    return parts[0] if len(parts) == 1 else jnp.concatenate(parts, axis=0)


def _chunk_totals(x, chunk):
    parts = [jnp.broadcast_to(jnp.sum(x[j:j + chunk], axis=0, keepdims=True), (chunk, x.shape[1]))
             for j in range(0, x.shape[0], chunk)]
    return parts[0] if len(parts) == 1 else jnp.concatenate(parts, axis=0)


def _pair_order_masks(n, reverse):
    ii = lax.broadcasted_iota(jnp.int32, (n, LANE), 0)
    jj = lax.broadcasted_iota(jnp.int32, (n, LANE), 1) & (PAIR_HEAD - 1)
    strict = (jj > ii) if reverse else (jj < ii)
    return strict, strict | (ii == jj), ii == jj


def _first_head(width, head_w):
    return (lax.broadcasted_iota(jnp.int32, (1, width), 1) & head_w) == 0


def _block_diag(z, first):
    return jnp.concatenate([jnp.where(first, z, 0.0), jnp.where(first, 0.0, z)], axis=0)


def _diag_blocks(rows, head_w_cols, cols):
    ri = lax.broadcasted_iota(jnp.int32, (rows, cols), 0) // (rows // 2)
    ci = lax.broadcasted_iota(jnp.int32, (rows, cols), 1) // head_w_cols
    return ri == ci


def _row_vector_to_column(row):
    n = row.shape[1]
    eye = lax.broadcasted_iota(jnp.int32, (n, n), 0) == lax.broadcasted_iota(jnp.int32, (n, n), 1)
    return jnp.sum(jnp.where(eye, row, 0.0), axis=1, keepdims=True)


def _gla_kernel(reverse, final, *refs):
    if final:
        p_ref, wgk_ref, bgk_ref, oprev_ref, ggla_ref, o_ref, h_ref, att_ref, cum_ref = refs
    else:
        p_ref, wgk_ref, bgk_ref, o_ref, h_ref, att_ref, cum_ref = refs

    @pl.when(pl.program_id(1) == 0)
    def _():
        h_ref[...] = jnp.zeros_like(h_ref)

    NB, R = p_ref.shape[0], p_ref.shape[1]
    C = SCAN_CHUNK
    nsub = R // C
    order = tuple(reversed(range(nsub))) if reverse else tuple(range(nsub))
    nk = GLA_HEADS * GLA_DK
    npair = GLA_HEADS // 2
    kw, vw = 2 * GLA_DK, 2 * GLA_DV
    incl2 = _pair_order_masks(C, reverse)[1]
    first_k = _first_head(kw, GLA_DK)
    first_v = _first_head(vw, GLA_DV)
    diag = _diag_blocks(kw, GLA_DV, vw)
    inclb = _block_order_mask(R, C, reverse).astype(BF16)
    q_s, q_t, q_m, k_t, k_e, e_tot, v, og = {}, {}, {}, {}, {}, {}, {}, {}
    span = None
    for nb in range(NB):
        p = p_ref[nb].astype(F32)
        q = p[:, 0:nk] * GLA_DK ** -0.5
        k = p[:, nk:2 * nk]
        v[nb] = p[:, 2 * nk:2 * nk + MIX_W]
        og[nb] = p[:, 2 * nk + MIX_W:2 * nk + 2 * MIX_W]
        gk = p[:, 2 * nk + 2 * MIX_W:]
        lg = -_softplus(-(_mm(gk, wgk_ref[...]) + bgk_ref[...])) / GLA_GATE_NORM
        cum = _mm_01(inclb, lg)
        tot = _chunk_totals(lg, C)
        mid = _chunk_row(cum, C, C // 2)
        rel = cum - mid
        part = jnp.max(jnp.abs(rel))
        span = part if span is None else jnp.maximum(span, part)
        cum_ref[nb] = cum
        q_s[nb] = q
        q_t[nb] = q * jnp.exp(cum)
        q_m[nb] = q * jnp.exp(rel)
        k_t[nb] = k * jnp.exp(-rel)
        k_e[nb] = k * jnp.exp(tot - cum)
        e_tot[nb] = jnp.exp(tot)
    ksl = lambda pr: slice(pr * kw, (pr + 1) * kw)
    vsl = lambda pr: slice(pr * vw, (pr + 1) * vw)
    rows = lambda j: slice(j * C, (j + 1) * C)
    chains = [(nb, pr) for nb in range(NB) for pr in range(npair)]
    units = [(j, nb, pr) for j in order for (nb, pr) in chains]
    qp = {(j, nb, pr): q_t[nb][rows(j), ksl(pr)] for (j, nb, pr) in units}
    vp = {(j, nb, pr): v[nb][rows(j), vsl(pr)] for (j, nb, pr) in units}
    kv ={(j, nb, pr): jnp.where(diag, _mm_tn(k_e[nb][rows(j), ksl(pr)], vp[(j, nb, pr)]), 0.0) for (j, nb, pr) in units}
    dec = {(j, nb, pr): _row_vector_to_column(e_tot[nb][j * C:j * C + 1, ksl(pr)]) for (j, nb, pr) in units}
    bdv = {u: _block_diag(vp[u], first_v) for u in units}
    safe = span <= GLA_SPLIT_EXP_LIMIT

    @pl.when(safe)
    def _():
        for idx, (j, nb, pr) in enumerate(units):
            att_ref[idx] = jnp.where(incl2, _mm_nt(q_m[nb][rows(j), ksl(pr)],
                                                   _block_diag(k_t[nb][rows(j), ksl(pr)], first_k)), 0.0)

    @pl.when(jnp.logical_not(safe))
    def _():
        tok = lax.broadcasted_iota(jnp.int32, (C, 1), 0)
        lane = lax.broadcasted_iota(jnp.int32, (1, kw), 1)
        for idx, (j, nb, pr) in enumerate(units):
            q_i = q_s[nb][rows(j), ksl(pr)]
            c_i = cum_ref[nb, rows(j), ksl(pr)]

            def columns(tile, acc, j=j, nb=nb, pr=pr, q_i=q_i, c_i=c_i):
                r0 = pl.multiple_of(j * C + tile * BF16_ROWS, BF16_ROWS)
                k_rows = p_ref[nb, pl.ds(r0, BF16_ROWS), nk + pr * kw:nk + (pr + 1) * kw].astype(F32)
                c_rows = cum_ref[nb, pl.ds(r0, BF16_ROWS), ksl(pr)]
                for r in range(BF16_ROWS):
                    t = tile * BF16_ROWS + r
                    x = q_i * k_rows[r:r + 1] * jnp.exp(jnp.minimum(c_i - c_rows[r:r + 1], 0.0))
                    s_a = jnp.sum(jnp.where(first_k, x, 0.0), axis=1, keepdims=True)
                    s_b = jnp.sum(jnp.where(first_k, 0.0, x), axis=1, keepdims=True)
                    seen = (tok <= t) if reverse else (tok >= t)
                    acc = jnp.where(lane == t, jnp.where(seen, s_a, 0.0), acc)
                    acc = jnp.where(lane == t + GLA_DK, jnp.where(seen, s_b, 0.0), acc)
                return acc

            att_ref[idx] = lax.fori_loop(0, C // BF16_ROWS, columns, jnp.zeros((C, kw), F32))

    att = {u: att_ref[idx] for idx, u in enumerate(units)}
    lhs = {u: jnp.concatenate([att[u], qp[u]], axis=1) for u in units}
    hs = {ch: h_ref[ch[0], ch[1]] for ch in chains}
    outs = {}
    for j in order:
        for ch in chains:
            u = (j,) + ch
            outs[u] = _mm(lhs[u], jnp.concatenate([bdv[u], hs[ch]], axis=0))
        for ch in chains:
            u = (j,) + ch
            hs[ch] = dec[u] * hs[ch] + kv[u]
    for ch in chains:
        h_ref[ch[0], ch[1]] = hs[ch]
    for nb in range(NB):
        o = jnp.concatenate([jnp.concatenate([outs[(j, nb, pr)] for pr in range(npair)], axis=1) for j in range(nsub)], axis=0)
        if final:
            o = o + oprev_ref[nb].astype(F32)
            g = ggla_ref[...]
            ys = []
            for h in range(GLA_HEADS):
                oh = o[:, h * GLA_DV:(h + 1) * GLA_DV]
                ys.append(oh * lax.rsqrt(jnp.mean(oh * oh, axis=-1, keepdims=True) + NORM_EPS) * g)
            o = jnp.concatenate(ys, axis=1) * _silu(og[nb])
        o_ref[nb] = o.astype(o_ref.dtype)


def _scan_batch(B, most=2):
    return max(n for n in (1, 2, 4) if n <= most and B % n == 0)


def _gla_scan(p, wgk, bgk, nzc, reverse, oprev=None, ggla=None):
    B, L, _ = p.shape
    C = SCAN_CHUNK * GLA_SUB
    NB = _scan_batch(B, GLA_SCAN_BATCH)
    nc = L // C
    nzc = nzc // GLA_SUB
    cidx = _chunk_index(reverse, nzc, nc)
    tok = lambda wd: pl.BlockSpec((NB, C, wd), lambda b, c: (b, cidx(c), 0))
    final = oprev is not None
    ins = [p, wgk, bgk]
    specs = [tok(GLA_W), _full(wgk.shape), _full(bgk.shape)]
    if final:
        ins += [oprev, ggla]
        specs += [tok(MIX_W), _full(ggla.shape)]
    return pl.pallas_call(
        functools.partial(_gla_kernel, reverse, final),
        grid=(B // NB, nc),
        in_specs=specs,
        out_specs=tok(MIX_W),
        out_shape=jax.ShapeDtypeStruct((B, L, MIX_W), F32 if final else BF16),
        scratch_shapes=[pltpu.VMEM((NB, GLA_HEADS // 2, 2 * GLA_DK, 2 * GLA_DV), F32),
                        pltpu.VMEM((GLA_SUB * NB * (GLA_HEADS // 2), SCAN_CHUNK, 2 * GLA_DK), F32),
                        pltpu.VMEM((NB, C, GLA_HEADS * GLA_DK), F32)],
        compiler_params=_params("parallel", "arbitrary"),
        name="gla_bwd" if reverse else "gla_fwd",
    )(*ins)


def _swa_kernel(nzb, nblk, q_ref, kv_ref, cs_ref, sink_ref, o_ref):
    BL = SWA_BLOCK
    nper = q_ref.shape[0] // BL
    kvw = SWA_KV_HEADS * SWA_HEAD_DIM
    qw = SWA_Q_HEADS * SWA_HEAD_DIM
    nloc = 3 * BL
    lz = nzb * BL
    nkeys = nloc + lz

    def rope(x, xs, cs, reps):
        cos = jnp.concatenate([cs[:, :LANE]] * reps, axis=1)
        sin = jnp.concatenate([cs[:, LANE:]] * reps, axis=1)
        return x * cos + xs * sin

    kvz = kv_ref[0:lz, :]
    q, keys, vals, bias = {}, {}, {}, {}
    for sub in range(nper):
        n = pl.program_id(1) * nper + sub
        qq = q_ref[sub * BL:(sub + 1) * BL, :]
        q0 = pl.multiple_of(n * BL, BL)
        q[sub] = rope(qq[:, :qw], qq[:, qw:], cs_ref[pl.ds(q0, BL), :], qw // LANE) * (SWA_HEAD_DIM ** -0.5 * LOG2E)
        k0 = pl.multiple_of(jnp.clip((n - 1) * BL, 0, (nblk - 3) * BL), BL)
        kv = kv_ref[pl.ds(k0, nloc), :]
        keys[sub] = jnp.concatenate([rope(kv[:, :kvw], kv[:, kvw:2 * kvw], cs_ref[pl.ds(k0, nloc), :], kvw // LANE),
                                     kvz[:, :kvw]], axis=0)
        vals[sub] = jnp.concatenate([kv[:, 2 * kvw:], kvz[:, 2 * kvw:]], axis=0)
        qpos = n * BL + lax.broadcasted_iota(jnp.int32, (BL, nkeys), 0)
        col = lax.broadcasted_iota(jnp.int32, (BL, nkeys), 1)
        kpos = k0 + col
        lo = jnp.where(n >= nzb, lz, nblk * BL)
        band = (jnp.abs(kpos - qpos) <= SWA_WINDOW) & (kpos >= lo)
        bias[sub] = jnp.where(band | (col >= nloc), 0.0, NEG_INF)
    chains = [(sub, h) for sub in range(nper) for h in range(SWA_Q_HEADS)]
    kvh = lambda h: slice((h // SWA_GROUP) * SWA_HEAD_DIM, (h // SWA_GROUP + 1) * SWA_HEAD_DIM)
    s = {(sub, h): _mm_nt(q[sub][:, h * SWA_HEAD_DIM:(h + 1) * SWA_HEAD_DIM], keys[sub][:, kvh(h)]) + bias[sub]
         for (sub, h) in chains}
    sk = [sink_ref[h] * LOG2E for h in range(SWA_Q_HEADS)]
    m = {c: jnp.maximum(jnp.max(s[c], axis=-1, keepdims=True), sk[c[1]]) for c in chains}
    e = {c: jnp.exp2(s[c] - m[c]) for c in chains}
    den = {c: jnp.sum(e[c], axis=-1, keepdims=True) + jnp.exp2(sk[c[1]] - m[c]) for c in chains}
    outs = {c: _mm(e[c], vals[c[0]][:, kvh(c[1])]) / den[c] for c in chains}
    for sub in range(nper):
        o_ref[sub * BL:(sub + 1) * BL, :] = jnp.concatenate([outs[(sub, h)] for h in range(SWA_Q_HEADS)], axis=1)


def _swa(pq, pkv, cs, sink, nzb):
    B, L, _ = pq.shape
    BL = SWA_BLOCK
    nblk = L // BL
    nper = 2 if nblk % 2 == 0 else 1
    assert nblk >= 3
    return pl.pallas_call(
        functools.partial(_swa_kernel, nzb, nblk),
        grid=(B, nblk // nper),
        in_specs=[pl.BlockSpec((None, nper * BL, SWA_Q_W), lambda b, n: (b, n, 0)),
                  pl.BlockSpec((None, L, SWA_KV_W), lambda b, n: (b, 0, 0)),
                  _full(cs.shape),
                  pl.BlockSpec(memory_space=pltpu.SMEM)],
        out_specs=pl.BlockSpec((None, nper * BL, MIX_W), lambda b, n: (b, n, 0)),
        out_shape=jax.ShapeDtypeStruct((B, L, MIX_W), F32),
        compiler_params=_params("parallel", "parallel"),
        name="swa",
    )(pq, pkv, cs, sink)


def _rwkv_feat_kernel(nzt, p_ref, pp_ref, pn_ref, vec_ref, ww_ref, wa_ref, wg_ref, e_ref,
                      o_r, o_v, o_kk, o_g, o_gb, o_lwf, o_kf, o_bf, o_lwb, o_kb, o_bb):
    i = pl.program_id(1)
    n = pl.num_programs(1)
    TM = p_ref.shape[0]
    first = (i == 0) | (i == nzt)
    last = (i == nzt - 1) | (i == n - 1)
    p_bf = p_ref[...]
    p_ext = jnp.concatenate([pp_ref[...], p_bf, pn_ref[...]], axis=0)
    row = lax.broadcasted_iota(jnp.int32, (TM, TM + 2 * HALO), 0)
    col = lax.broadcasted_iota(jnp.int32, (TM, TM + 2 * HALO), 1)
    lo = jnp.where(first, HALO, 0)
    hi = jnp.where(last, TM + HALO - 1, TM + 2 * HALO - 1)
    nbr = ((col == row + (HALO - 1)) & (col >= lo)) | ((col == row + (HALO + 1)) & (col <= hi))
    around = jnp.dot(jnp.where(nbr, 1.0, 0.0).astype(BF16), p_ext, preferred_element_type=F32)
    p = p_bf.astype(F32)
    vec = vec_ref[...]
    mu = jnp.concatenate([vec[8 + j:9 + j] for j in range(4)], axis=1)[:, :RWKV_W]
    pm = p * (1.0 - mu) + (0.5 * mu) * around
    W = MIX_W
    r, k, v = pm[:, :W], pm[:, W:2 * W], pm[:, 2 * W:3 * W]
    wa = pm[:, 3 * W:3 * W + LANE]
    gl = pm[:, 3 * W + LANE:]
    e = e_ref[...]
    k_k, k_a, r_k = vec[4:5], vec[5:6], vec[6:7]
    g = _mm(_sigmoid(gl), wg_ref[...])
    kkn = k * k_k
    kk = kkn * lax.rsqrt(jnp.maximum(_segsum(kkn * kkn, e), 1e-24))
    twa = jnp.tanh(wa)
    ksum = None
    for d, (o_lw, o_k, o_b) in enumerate(((o_lwf, o_kf, o_bf), (o_lwb, o_kb, o_bb))):
        a = _sigmoid(vec[2 + d:3 + d] + _mm(wa, wa_ref[d]))
        kd = k * (1.0 + (a - 1.0) * k_a)
        o_lw[...] = -DECAY_SCALE * _sigmoid(vec[d:d + 1] + _mm(twa, ww_ref[d]))
        o_k[...] = kd.astype(o_k.dtype)
        o_b[...] = (kk * a).astype(o_b.dtype)
        ksum = kd if ksum is None else ksum + kd
    o_r[...] = r.astype(o_r.dtype)
    o_v[...] = v.astype(o_v.dtype)
    o_kk[...] = kk.astype(o_kk.dtype)
    o_g[...] = g.astype(o_g.dtype)
    o_gb[...] = (_segsum(r * ksum * r_k, e) * v * g).astype(o_gb.dtype)


def _rwkv_feat(p, vec, ww, wa, wg, e, nzt):
    B, L, _ = p.shape
    TM = ROW_TILE
    rh = TM // HALO
    tok = pl.BlockSpec((None, TM, RWKV_W), lambda b, i: (b, i, 0))
    halo_p = pl.BlockSpec((None, HALO, RWKV_W), lambda b, i: (b, jnp.maximum(i * rh - 1, 0), 0))
    halo_n = pl.BlockSpec((None, HALO, RWKV_W), lambda b, i: (b, jnp.minimum((i + 1) * rh, L // HALO - 1), 0))
    out = pl.BlockSpec((None, TM, MIX_W), lambda b, i: (b, i, 0))
    dts = [BF16] * 5 + [F32, BF16, BF16] * 2
    return pl.pallas_call(
        functools.partial(_rwkv_feat_kernel, nzt),
        grid=(B, L // TM),
        in_specs=[tok, halo_p, halo_n, _full(vec.shape), _full(ww.shape), _full(wa.shape), _full(wg.shape), _full(e.shape)],
        out_specs=[out] * 11,
        out_shape=[jax.ShapeDtypeStruct((B, L, MIX_W), dt) for dt in dts],
        compiler_params=_params("parallel", "parallel"),
        name="rwkv_feat",
    )(p, p, p, vec, ww, wa, wg, e)


def _rwkv_scan_kernel(reverse, final, *refs):
    if final:
        r_ref, v_ref, kk_ref, lw_ref, k_ref, b_ref, yprev_ref, g_ref, gb_ref, gn_ref, o_ref, h_ref = refs
    else:
        r_ref, v_ref, kk_ref, lw_ref, k_ref, b_ref, o_ref, h_ref = refs

    @pl.when(pl.program_id(1) == 0)
    def _():
        h_ref[...] = jnp.zeros_like(h_ref)

    NB, R = lw_ref.shape[0], lw_ref.shape[1]
    C = SCAN_CHUNK
    nsub = R // C
    order = tuple(reversed(range(nsub))) if reverse else tuple(range(nsub))
    PW = LANE
    npair = RWKV_HEADS // 2
    strict2, incl2, ident2 = _pair_order_masks(C, reverse)
    first = _first_head(PW, RWKV_HEAD)
    diag = _diag_blocks(PW, RWKV_HEAD, PW)
    inclb = _block_order_mask(R, C, reverse).astype(BF16)
    kap_t, r_t, k_t, b_t, k_e, b_e, e_tot, v_all = {}, {}, {}, {}, {}, {}, {}, {}
    for nb in range(NB):
        lw = lw_ref[nb]
        cum = _mm_01(inclb, lw)
        tot = _chunk_totals(lw, C)
        e_inv = jnp.exp(-cum)
        e_end = jnp.exp(tot - cum)
        e_tot[nb] = jnp.exp(tot)
        kap_t[nb] = kk_ref[nb].astype(F32) * jnp.exp(cum - lw)
        r_t[nb] = r_ref[nb].astype(F32) * jnp.exp(cum)
        k_all, b_all = k_ref[nb].astype(F32), b_ref[nb].astype(F32)
        k_t[nb], b_t[nb] = k_all * e_inv, b_all * e_inv
        k_e[nb], b_e[nb] = k_all * e_end, b_all * e_end
        v_all[nb] = v_ref[nb].astype(F32)
    sl = lambda pr: slice(pr * PW, (pr + 1) * PW)
    rows = lambda j: slice(j * C, (j + 1) * C)
    pick = lambda d, u: d[u[1]][rows(u[0]), sl(u[2])]
    chains = [(nb, pr) for nb in range(NB) for pr in range(npair)]
    units = [(j,) + ch for j in order for ch in chains]
    vp = {u: pick(v_all, u) for u in units}
    G = {u: _mm_nt(jnp.concatenate([pick(kap_t, u), pick(r_t, u)], axis=0),
                   jnp.concatenate([_block_diag(pick(b_t, u), first), _block_diag(pick(k_t, u), first)], axis=0))
         for u in units}
    a_ab = {u: jnp.where(strict2, G[u][:C, :PW], 0.0) for u in units}
    a_ak = {u: jnp.where(strict2, G[u][:C, PW:], 0.0) for u in units}
    a_rb = {u: jnp.where(incl2, G[u][C:, :PW], 0.0) for u in units}
    a_rk = {u: jnp.where(incl2, G[u][C:, PW:], 0.0) for u in units}
    eye2 = jnp.where(ident2, 1.0, 0.0)
    t_inv = {u: eye2 - a_ab[u] for u in units}
    pw = {u: _mm(a_ab[u], _block_diag(a_ab[u], first)) for u in units}
    span = 2
    while span < C:
        span *= 2
        if span < C:
            prod = {u: _mm(jnp.concatenate([t_inv[u], pw[u]], axis=0), _block_diag(pw[u], first)) for u in units}
            t_inv = {u: t_inv[u] + prod[u][:C] for u in units}
            pw = {u: prod[u][C:] for u in units}
        else:
            t_inv = {u: t_inv[u] + _mm(t_inv[u], _block_diag(pw[u], first)) for u in units}
    bdv = {u: _block_diag(vp[u], first) for u in units}
    ulhs = {u: jnp.concatenate([a_ak[u], pick(kap_t, u)], axis=1) for u in units}
    ylhs = {u: jnp.concatenate([a_rk[u], -a_rb[u], pick(r_t, u)], axis=1) for u in units}
    hlhs = {u: jnp.concatenate([pick(k_e, u), -pick(b_e, u)], axis=0) for u in units}
    dec = {u: _row_vector_to_column(e_tot[u[1]][u[0] * C:u[0] * C + 1, sl(u[2])]) for u in units}
    hs = {ch: h_ref[ch[0], ch[1]] for ch in chains}
    ys = {}
    for j in order:
        units_j = [(j,) + ch for ch in chains]
        rhs = {u: _mm(ulhs[u], jnp.concatenate([bdv[u], hs[u[1:]]], axis=0)) for u in units_j}
        u_p = {u: _mm(t_inv[u], _block_diag(rhs[u], first)) for u in units_j}
        for u in units_j:
            ys[u] = _mm(ylhs[u], jnp.concatenate([bdv[u], _block_diag(u_p[u], first), hs[u[1:]]], axis=0))
        for u in units_j:
            upd = _mm_tn(hlhs[u], jnp.concatenate([vp[u], u_p[u]], axis=0))
            hs[u[1:]] = dec[u] * hs[u[1:]] + jnp.where(diag, upd, 0.0)
    for ch in chains:
        h_ref[ch[0], ch[1]] = hs[ch]
    for nb in range(NB):
        if final:
            yprev = yprev_ref[nb]
            outs = []
            for pr in range(npair):
                y = jnp.concatenate([ys[(j, nb, pr)] for j in range(nsub)], axis=0) + yprev[:, sl(pr)]
                s_a = jnp.sum(jnp.where(first, y, 0.0), axis=-1, keepdims=True)
                s_b = jnp.sum(jnp.where(first, 0.0, y), axis=-1, keepdims=True)
                yc = y - jnp.where(first, s_a, s_b) * (1.0 / RWKV_HEAD)
                q = yc * yc
                v_a = jnp.sum(jnp.where(first, q, 0.0), axis=-1, keepdims=True)
                v_b = jnp.sum(jnp.where(first, 0.0, q), axis=-1, keepdims=True)
                outs.append(yc * lax.rsqrt(jnp.where(first, v_a, v_b) * (1.0 / RWKV_HEAD) + RWKV_GN_EPS))
            gn = gn_ref[...]
            o_ref[nb] = (jnp.concatenate(outs, axis=1) * gn[0:1] + gn[1:2]) * g_ref[nb].astype(F32) + gb_ref[nb].astype(F32)
        else:
            o_ref[nb] = jnp.concatenate(
                [jnp.concatenate([ys[(j, nb, pr)] for pr in range(npair)], axis=1) for j in range(nsub)], axis=0)


def _rwkv_scan(r, v, kk, lw, k, b, nzc, reverse, yprev=None, g=None, gb=None, gn=None):
    B, L, _ = r.shape
    C = SCAN_CHUNK * SCAN_SUB
    NB = _scan_batch(B, RWKV_SCAN_BATCH)
    nc = L // C
    nzc = nzc // SCAN_SUB
    cidx = _chunk_index(reverse, nzc, nc)
    tok = pl.BlockSpec((NB, C, MIX_W), lambda b_, c: (b_, cidx(c), 0))
    final = yprev is not None
    ins = [r, v, kk, lw, k, b]
    specs = [tok] * 6
    if final:
        ins += [yprev, g, gb, gn]
        specs += [tok] * 3 + [_full(gn.shape)]
    return pl.pallas_call(
        functools.partial(_rwkv_scan_kernel, reverse, final),
        grid=(B // NB, nc),
        in_specs=specs,
        out_specs=tok,
        out_shape=jax.ShapeDtypeStruct((B, L, MIX_W), F32),
        scratch_shapes=[pltpu.VMEM((NB, RWKV_HEADS // 2, LANE, LANE), F32)],
        compiler_params=_params("parallel", "arbitrary"),
        name="rwkv_bwd" if reverse else "rwkv_fwd",
    )(*ins)


def _merge_kernel(stream, off, *refs):
    na, ns = len(stream.arrays), stream.ns
    mod_refs = refs[na:na + ns]
    g_ref, ya_ref, yb_ref, yc_ref, wg_ref, wb_ref, wo_ref, o_ref = refs[na + ns:]
    x = stream.read(refs[:na], off)
    m = [mod_refs[s][...] for s in range(ns)]
    h = [_norm_mod(x[s], g_ref[...], m[s][0:1], m[s][1:2]).astype(BF16) for s in range(ns)]
    gates = [jnp.dot(h[s], wg_ref[...], preferred_element_type=F32) for s in range(ns)]
    acc = [None] * ns
    for i, y_ref in enumerate((ya_ref, yb_ref, yc_ref)):
        for s in range(ns):
            t = _mm(y_ref[s], wb_ref[i]) / (1.0 + jnp.exp(-gates[s][:, i * D_MODEL:(i + 1) * D_MODEL]))
            acc[s] = t if acc[s] is None else acc[s] + t
    for s in range(ns):
        o_ref[s] = x[s] + m[s][2:3] * _mm(acc[s], wo_ref[...])


def _merge(stream, l, mod, g, ya, yb, yc, wg, wb, wo, off):
    B, L, ns = stream.batch, stream.rows, stream.ns
    nt = L // ROW_TILE - off
    return pl.pallas_call(
        functools.partial(_merge_kernel, stream, off),
        grid=(B // ns, nt),
        in_specs=stream.specs(off) + _mod_specs(l, stream.nzt, B, ns, off) + [
            _layer_spec(g, l), _tok_spec(ns, MIX_W, off), _tok_spec(ns, MIX_W, off), _tok_spec(ns, MIX_W, off),
            _layer_spec(wg, l), _layer_spec(wb, l), _layer_spec(wo, l)],
        out_specs=_tok_spec(ns, D_MODEL),
        out_shape=jax.ShapeDtypeStruct((B, nt * ROW_TILE, D_MODEL), F32),
        compiler_params=_params("parallel", "parallel"),
        name="merge",
    )(*stream.arrays, *([mod] * ns), g, ya, yb, yc, wg, wb, wo)


def _ffn_kernel(final, ns, s_ref, *refs):
    mod_refs = refs[:ns]
    g_ref, w1_ref, w3_ref, w2_ref, gf_ref, o_ref = refs[ns:]
    m = [mod_refs[s][...] for s in range(ns)]
    x = [s_ref[s] for s in range(ns)]
    h = [_norm_mod(x[s], g_ref[...], m[s][3:4], m[s][4:5]).astype(BF16) for s in range(ns)]
    a = [jnp.dot(h[s], w1_ref[...], preferred_element_type=F32) for s in range(ns)]
    b = [jnp.dot(h[s], w3_ref[...], preferred_element_type=F32) for s in range(ns)]
    for s in range(ns):
        o = x[s] + m[s][5:6] * _mm(_silu(a[s]) * b[s], w2_ref[...])
        if final:
            o = o * lax.rsqrt(jnp.mean(o * o, axis=-1, keepdims=True) + NORM_EPS) * gf_ref[...]
        o_ref[s] = o


def _ffn(s, l, mod, g, w1, w3, w2, gf, nzt, off, final):
    B, L, _ = s.shape
    ns = _tile_batch(B)
    once = pl.Buffered(1)
    return pl.pallas_call(
        functools.partial(_ffn_kernel, final, ns),
        grid=(B // ns, L // ROW_TILE),
        in_specs=[_tok_spec(ns, D_MODEL)] + _mod_specs(l, nzt, B, ns, off) + [
            _layer_spec(g, l), _layer_spec(w1, l, pipeline_mode=once), _layer_spec(w3, l, pipeline_mode=once),
            _layer_spec(w2, l, pipeline_mode=once), _full((1, D_MODEL))],
        out_specs=_tok_spec(ns, D_MODEL),
        out_shape=jax.ShapeDtypeStruct((B, L, D_MODEL), F32),
        compiler_params=_params("parallel", "parallel"),
        name="ffn",
    )(s, *([mod] * ns), g, w1, w3, w2, gf)


def _rope_tables(T, lz):
    rows = T // GRID_W
    row = jnp.repeat(jnp.arange(rows), GRID_W).astype(F32)
    col = jnp.tile(jnp.arange(GRID_W), rows).astype(F32)
    inv = ROPE_BASE ** (-jnp.arange(ROPE_FREQS, dtype=F32) / ROPE_FREQS)
    ar, ac = row[:, None] * inv, col[:, None] * inv
    cos = jnp.concatenate([jnp.cos(ar), jnp.cos(ar), jnp.cos(ac), jnp.cos(ac)], axis=1)
    sin = jnp.concatenate([-jnp.sin(ar), jnp.sin(ar), -jnp.sin(ac), jnp.sin(ac)], axis=1)
    reps = LANE // SWA_HEAD_DIM
    x_tab = jnp.concatenate([jnp.tile(cos, (1, reps)), jnp.tile(sin, (1, reps))], axis=1)
    z_tab = jnp.concatenate([jnp.ones((lz, LANE), F32), jnp.zeros((lz, LANE), F32)], axis=1)
    return jnp.concatenate([z_tab, x_tab], axis=0)


def _partner_cols(w):
    return w.reshape(w.shape[:-1] + (-1, 2, ROPE_FREQS))[..., ::-1, :].reshape(w.shape)


def _mixer_weight(w_in):
    nk = GLA_HEADS * GLA_DK
    a = 2 * nk + MIX_W
    r2 = 2 * GLA_GATE_RANK
    head, gk, tail = w_in[..., :a].astype(BF16), w_in[..., a:a + r2].astype(BF16), w_in[..., a + r2:].astype(BF16)
    qw = SWA_Q_HEADS * SWA_HEAD_DIM
    kvw = SWA_KV_HEADS * SWA_HEAD_DIM
    o = np.cumsum([0, MIX_W, qw, kvw, kvw, RWKV_W])
    og, q, k, v, rwkv = (tail[..., o[i]:o[i + 1]] for i in range(5))
    pad = jnp.zeros(w_in.shape[:-1] + (LANE - r2,), BF16)
    w = jnp.concatenate([head, og, gk, pad, q, _partner_cols(q), k, _partner_cols(k), v, rwkv], axis=-1)
    return w, tail[..., o[5]:]


def _pad_rows(w, top, total):
    return jnp.concatenate([jnp.zeros((top, w.shape[1]), F32), w, jnp.zeros((total - top - w.shape[0], w.shape[1]), F32)], axis=0)


def kernel(x, c, ctx, c_ctx, w_ada, b_ada, g_mix, g_ffn, w_in, w_gk2, b_gk, g_gla, sink, mu_shift, w0, w_w2, a0, w_a2,
           w_g2, k_k, k_a, r_k, gn_w, gn_b, w_branch, w_out, w_ffn1, w_ffn3, w_ffn2, g_final):
    B, T, D = x.shape
    lz = ctx.shape[1]
    depth = w_in.shape[0]
    step = max(ROW_TILE, SCAN_CHUNK * GLA_SUB, SCAN_CHUNK * SCAN_SUB, SWA_BLOCK)
    assert D == D_MODEL and lz % step == 0 and T % step == 0 and T % GRID_W == 0
    nzt, nzc, nzb = lz // ROW_TILE, lz // SCAN_CHUNK, lz // SWA_BLOCK
    cs = _rope_tables(T, lz)
    c8 = jnp.concatenate([c, c_ctx[None], jnp.zeros((8 - (B + 1) % 8 if (B + 1) % 8 else 0, D), F32)], axis=0)
    hidx = np.arange(MIX_W) // RWKV_HEAD
    seg = jnp.asarray(hidx[:, None] == hidx[None, :], BF16)
    mod = _ada(c8, w_ada, b_ada).reshape(depth, c8.shape[0], 6, D)
    w_mix, w_gate = _mixer_weight(w_in)
    w_branch, w_out = w_branch.astype(BF16), w_out.astype(BF16)
    w_ffn1, w_ffn3, w_ffn2 = w_ffn1.astype(BF16), w_ffn3.astype(BF16), w_ffn2.astype(BF16)
    g_mix, g_ffn = g_mix[:, None], g_ffn[:, None]
    stream = _Stream((ctx, x), nzt)
    out = None
    for l in range(depth):
        last = l == depth - 1
        p_gla, p_q, p_kv, p_rwkv = _inproj(stream, l, mod, g_mix, w_mix)

        wgk = [_pad_rows(w_gk2[l, d], d * GLA_GATE_RANK, LANE).astype(BF16) for d in range(2)]
        o_f = _gla_scan(p_gla, wgk[0], b_gk[l, 0][None], nzc, False)
        ya = _gla_scan(p_gla, wgk[1], b_gk[l, 1][None], nzc, True, o_f, g_gla[l][None])

        yb = _swa(p_q, p_kv, cs, sink[l], nzb)

        zero = jnp.zeros((MIX_W,), F32)
        mu4 = jnp.concatenate([mu_shift[l], jnp.zeros((4 * MIX_W - RWKV_W,), F32)]).reshape(4, MIX_W)
        vec = jnp.stack([w0[l, 0], w0[l, 1], a0[l, 0], a0[l, 1], k_k[l], k_a[l], r_k[l].reshape(-1), zero], axis=0)
        vec = jnp.concatenate([vec, mu4, jnp.zeros((4, MIX_W), F32)], axis=0)
        ww = jnp.stack([_pad_rows(w_w2[l, d], 0, LANE) for d in range(2)]).astype(BF16)
        wa = jnp.stack([_pad_rows(w_a2[l, d], RWKV_DECAY_RANK, LANE) for d in range(2)]).astype(BF16)
        f_r, f_v, f_kk, f_g, f_gb, lw_f, k_f, b_f, lw_b, k_b, b_b = _rwkv_feat(
            p_rwkv, vec, ww, wa, w_g2[l].astype(BF16), seg, nzt)
        y_f = _rwkv_scan(f_r, f_v, f_kk, lw_f, k_f, b_f, nzc, False)
        yc = _rwkv_scan(f_r, f_v, f_kk, lw_b, k_b, b_b, nzc, True, y_f, f_g, f_gb, jnp.stack([gn_w[l], gn_b[l]]))

        off = nzt if last else 0
        s_mix = _merge(stream, l, mod, g_mix, ya, yb, yc, w_gate, w_branch, w_out, off)
        out = _ffn(s_mix, l, mod, g_ffn, w_ffn1, w_ffn3, w_ffn2, g_final[None], nzt, off, last)
        stream = _Stream((out,), nzt)
    return out
```

```python
import functools

import jax
import jax.numpy as jnp
import numpy as np
from jax import lax
from jax.experimental import pallas as pl
from jax.experimental.pallas import tpu as pltpu

F32 = jnp.float32
BF16 = jnp.bfloat16
LOG2E = 1.4426950408889634
DECAY_SCALE = 0.6065306597126334

D_MODEL = 1024
GRID_W = 64
NORM_EPS = 1e-6
NEG_INF = -1e30
MIX_W = D_MODEL // 2
N_BRANCH = 3
GLA_HEADS = 4
GLA_DV = MIX_W // GLA_HEADS
GLA_DK = GLA_DV // 2
GLA_GATE_RANK = 16
GLA_GATE_NORM = 16.0
SWA_HEAD_DIM = 64
SWA_Q_HEADS = MIX_W // SWA_HEAD_DIM
SWA_KV_HEADS = SWA_Q_HEADS // 4
SWA_GROUP = SWA_Q_HEADS // SWA_KV_HEADS
SWA_WINDOW = 128
SWA_BLOCK = 128
ROPE_FREQS = SWA_HEAD_DIM // 4
ROPE_BASE = 10000.0
RWKV_HEAD = 64
RWKV_HEADS = MIX_W // RWKV_HEAD
RWKV_DECAY_RANK = 64
RWKV_A_RANK = 64
RWKV_GATE_RANK = 128
RWKV_GN_EPS = 64e-5
FFN_HIDDEN = -(-8 * D_MODEL // (3 * 256)) * 256
GLA_SPLIT = (GLA_HEADS * GLA_DK, GLA_HEADS * GLA_DK, MIX_W, GLA_GATE_RANK, GLA_GATE_RANK, MIX_W)
SWA_SPLIT = (SWA_Q_HEADS * SWA_HEAD_DIM, SWA_KV_HEADS * SWA_HEAD_DIM, SWA_KV_HEADS * SWA_HEAD_DIM)
RWKV_SPLIT = (MIX_W, MIX_W, MIX_W, RWKV_DECAY_RANK, RWKV_A_RANK, RWKV_GATE_RANK)
GROUP_SPLIT = (sum(GLA_SPLIT), sum(SWA_SPLIT), sum(RWKV_SPLIT), N_BRANCH * D_MODEL)

LANE = 128
BF16_ROWS = 16
ROW_TILE = 256
HALO = BF16_ROWS
SCAN_CHUNK = 64
SCAN_SUB = 4
GLA_SUB = 4
RWKV_SCAN_BATCH = 4
GLA_SCAN_BATCH = 4
GLA_SPLIT_EXP_LIMIT = 80.0
PAIR_HEAD = LANE // 2
assert GLA_DK == PAIR_HEAD and RWKV_HEAD == PAIR_HEAD and SCAN_CHUNK == PAIR_HEAD
GLA_W = 2 * GLA_HEADS * GLA_DK + 2 * MIX_W + LANE
SWA_Q_W = 2 * SWA_Q_HEADS * SWA_HEAD_DIM
SWA_KV_W = 3 * SWA_KV_HEADS * SWA_HEAD_DIM
RWKV_W = sum(RWKV_SPLIT)
MIXER_W = GLA_W + SWA_Q_W + SWA_KV_W + RWKV_W
VMEM_LIMIT = 56 * 1024 * 1024


def _mm(a, b):
    return jnp.dot(a.astype(BF16), b.astype(BF16), preferred_element_type=F32)


def _mm_nt(a, b):
    return lax.dot_general(a.astype(BF16), b.astype(BF16), (((1,), (1,)), ((), ())), preferred_element_type=F32)


def _mm_tn(a, b):
    return lax.dot_general(a.astype(BF16), b.astype(BF16), (((0,), (0,)), ((), ())), preferred_element_type=F32)


def _sigmoid(x):
    return 0.5 * jnp.tanh(0.5 * x) + 0.5


def _silu(x):
    return x * _sigmoid(x)


def _softplus(x):
    return jnp.maximum(x, 0.0) + jnp.log(1.0 + jnp.exp(-jnp.abs(x)))


def _norm_mod(x, g, shift, scale):
    y = x * lax.rsqrt(jnp.mean(x * x, axis=-1, keepdims=True) + NORM_EPS)
    return (y * g) * (1.0 + scale) + shift


def _segsum(x, e):
    return jnp.dot(x.astype(BF16), e, preferred_element_type=F32)


def _params(*sem):
    return pltpu.CompilerParams(dimension_semantics=sem, vmem_limit_bytes=VMEM_LIMIT)


def _full(shape):
    nd = len(shape)
    return pl.BlockSpec(shape, lambda *_: (0,) * nd)


def _split_bf16(x):
    hi = x.astype(BF16)
    return hi, (x - hi.astype(F32)).astype(BF16)


def _ada_kernel(c_ref, w_ref, b_ref, o_ref):
    s_hi, s_lo = _split_bf16(_silu(c_ref[...]))
    w_hi, w_lo = _split_bf16(w_ref[...])
    dot = lambda a, b: jnp.dot(a, b, preferred_element_type=F32)
    o_ref[...] = dot(s_hi, w_hi) + dot(s_lo, w_hi) + dot(s_hi, w_lo) + b_ref[...]


def _ada(c8, w, b):
    depth, _, n = w.shape
    tn = 1536
    return pl.pallas_call(
        _ada_kernel,
        grid=(depth, n // tn),
        in_specs=[_full(c8.shape), pl.BlockSpec((None, D_MODEL, tn), lambda l, j: (l, 0, j)),
                  pl.BlockSpec((None, 1, tn), lambda l, j: (l, 0, j))],
        out_specs=pl.BlockSpec((None, c8.shape[0], tn), lambda l, j: (l, 0, j)),
        out_shape=jax.ShapeDtypeStruct((depth, c8.shape[0], n), F32),
        compiler_params=_params("arbitrary", "arbitrary"),
        name="ada",
    )(c8, w, b.reshape(depth, 1, n))


def _tile_batch(B):
    return 2 if B % 2 == 0 else 1


class _Stream:
    def __init__(self, arrays, nzt):
        self.arrays, self.nzt = tuple(arrays), nzt
        self.split = len(self.arrays) == 2
        self.batch = self.arrays[0].shape[0]
        self.rows = sum(a.shape[1] for a in self.arrays)
        self.ns = _tile_batch(self.batch)

    def specs(self, off=0):
        blk, nzt = (self.ns, ROW_TILE, D_MODEL), self.nzt
        if not self.split:
            return [pl.BlockSpec(blk, lambda b, i: (b, i + off, 0))]
        return [pl.BlockSpec(blk, lambda b, i: (b, jnp.minimum(i + off, nzt - 1), 0)),
                pl.BlockSpec(blk, lambda b, i: (b, jnp.maximum(i + off - nzt, 0), 0))]

    def read(self, refs, off=0):
        if not self.split:
            return refs[0][...]
        return jnp.where(pl.program_id(1) + off < self.nzt, refs[0][...], refs[1][...])


def _layer_spec(w, l, **kw):
    nd = w.ndim - 1
    return pl.BlockSpec((None,) + w.shape[1:], lambda *_: (l,) + (0,) * nd, **kw)


def _mod_specs(l, nzt, nb, ns, off=0):
    return [pl.BlockSpec((None, None, 6, D_MODEL),
                         lambda b, i, s=s: (l, jnp.where(i + off < nzt, nb, b * ns + s), 0, 0)) for s in range(ns)]


def _tok_spec(ns, width, off=0):
    return pl.BlockSpec((ns, ROW_TILE, width), lambda b, i: (b, i + off, 0))


def _inproj_kernel(stream, *refs):
    na, ns = len(stream.arrays), stream.ns
    mod_refs = refs[na:na + ns]
    g_ref, w_ref, o_gla, o_q, o_kv, o_rwkv = refs[na + ns:]
    x = stream.read(refs[:na])
    h = []
    for s in range(ns):
        m = mod_refs[s][...]
        h.append(_norm_mod(x[s], g_ref[...], m[0:1], m[1:2]).astype(BF16))
    p = [jnp.dot(h[s], w_ref[...], preferred_element_type=F32) for s in range(ns)]
    for s in range(ns):
        o0 = 0
        for o_ref in (o_gla, o_q, o_kv, o_rwkv):
            wdt = o_ref.shape[-1]
            o_ref[s] = p[s][:, o0:o0 + wdt].astype(o_ref.dtype)
            o0 += wdt


def _inproj(stream, l, mod, g, w):
    B, L, ns = stream.batch, stream.rows, stream.ns
    widths = (GLA_W, SWA_Q_W, SWA_KV_W, RWKV_W)
    return pl.pallas_call(
        functools.partial(_inproj_kernel, stream),
        grid=(B // ns, L // ROW_TILE),
        in_specs=stream.specs() + _mod_specs(l, stream.nzt, B, ns) + [_layer_spec(g, l), _layer_spec(w, l)],
        out_specs=[_tok_spec(ns, wd) for wd in widths],
        out_shape=[jax.ShapeDtypeStruct((B, L, wd), dt) for wd, dt in zip(widths, (BF16, F32, F32, BF16))],
        compiler_params=_params("parallel", "parallel"),
        name="inproj",
    )(*stream.arrays, *([mod] * ns), g, w)


def _chunk_index(reverse, nzc, nc):
    if not reverse:
        return lambda c: c
    return lambda c: jnp.where(c < nzc, nzc - 1 - c, nzc + nc - 1 - c)


def _block_order_mask(n, chunk, reverse):
    ii = lax.broadcasted_iota(jnp.int32, (n, n), 0)
    jj = lax.broadcasted_iota(jnp.int32, (n, n), 1)
    return ((ii // chunk) == (jj // chunk)) & ((jj >= ii) if reverse else (jj <= ii))


def _mm_01(m01, x):
    hi = x.astype(BF16)
    r1 = x - hi.astype(F32)
    mid = r1.astype(BF16)
    lo = (r1 - mid.astype(F32)).astype(BF16)
    dot = lambda piece: jnp.dot(m01, piece, preferred_element_type=F32)
    return dot(hi) + dot(mid) + dot(lo)


def _chunk_row(x, chunk, r):
    parts = [jnp.broadcast_to(x[j + r:j + r + 1], (chunk, x.shape[1])) for j in range(0, x.shape[0], chunk)]
    return parts[0] if len(parts) == 1 else jnp.concatenate(parts, axis=0)


def _chunk_totals(x, chunk):
    parts = [jnp.broadcast_to(jnp.sum(x[j:j + chunk], axis=0, keepdims=True), (chunk, x.shape[1]))
             for j in range(0, x.shape[0], chunk)]
    return parts[0] if len(parts) == 1 else jnp.concatenate(parts, axis=0)


def _pair_order_masks(n, reverse):
    ii = lax.broadcasted_iota(jnp.int32, (n, LANE), 0)
    jj = lax.broadcasted_iota(jnp.int32, (n, LANE), 1) & (PAIR_HEAD - 1)
    strict = (jj > ii) if reverse else (jj < ii)
    return strict, strict | (ii == jj), ii == jj


def _first_head(width, head_w):
    return (lax.broadcasted_iota(jnp.int32, (1, width), 1) & head_w) == 0


def _block_diag(z, first):
    return jnp.concatenate([jnp.where(first, z, 0.0), jnp.where(first, 0.0, z)], axis=0)


def _diag_blocks(rows, head_w_cols, cols):
    ri = lax.broadcasted_iota(jnp.int32, (rows, cols), 0) // (rows // 2)
    ci = lax.broadcasted_iota(jnp.int32, (rows, cols), 1) // head_w_cols
    return ri == ci


def _row_vector_to_column(row):
    n = row.shape[1]
    eye = lax.broadcasted_iota(jnp.int32, (n, n), 0) == lax.broadcasted_iota(jnp.int32, (n, n), 1)
    return jnp.sum(jnp.where(eye, row, 0.0), axis=1, keepdims=True)


def _gla_kernel(reverse, final, *refs):
    if final:
        p_ref, wgk_ref, bgk_ref, oprev_ref, ggla_ref, o_ref, h_ref, att_ref, cum_ref = refs
    else:
        p_ref, wgk_ref, bgk_ref, o_ref, h_ref, att_ref, cum_ref = refs

    @pl.when(pl.program_id(1) == 0)
    def _():
        h_ref[...] = jnp.zeros_like(h_ref)

    NB, R = p_ref.shape[0], p_ref.shape[1]
    C = SCAN_CHUNK
    nsub = R // C
    order = tuple(reversed(range(nsub))) if reverse else tuple(range(nsub))
    nk = GLA_HEADS * GLA_DK
    npair = GLA_HEADS // 2
    kw, vw = 2 * GLA_DK, 2 * GLA_DV
    incl2 = _pair_order_masks(C, reverse)[1]
    first_k = _first_head(kw, GLA_DK)
    first_v = _first_head(vw, GLA_DV)
    diag = _diag_blocks(kw, GLA_DV, vw)
    inclb = _block_order_mask(R, C, reverse).astype(BF16)
    q_s, q_t, q_m, k_t, k_e, e_tot, v, og = {}, {}, {}, {}, {}, {}, {}, {}
    span = None
    for nb in range(NB):
        p = p_ref[nb].astype(F32)
        q = p[:, 0:nk] * GLA_DK ** -0.5
        k = p[:, nk:2 * nk]
        v[nb] = p[:, 2 * nk:2 * nk + MIX_W]
        og[nb] = p[:, 2 * nk + MIX_W:2 * nk + 2 * MIX_W]
        gk = p[:, 2 * nk + 2 * MIX_W:]
        lg = -_softplus(-(_mm(gk, wgk_ref[...]) + bgk_ref[...])) / GLA_GATE_NORM
        cum = _mm_01(inclb, lg)
        tot = _chunk_totals(lg, C)
        mid = _chunk_row(cum, C, C // 2)
        rel = cum - mid
        part = jnp.max(jnp.abs(rel))
        span = part if span is None else jnp.maximum(span, part)
        cum_ref[nb] = cum
        q_s[nb] = q
        q_t[nb] = q * jnp.exp(cum)
        q_m[nb] = q * jnp.exp(rel)
        k_t[nb] = k * jnp.exp(-rel)
        k_e[nb] = k * jnp.exp(tot - cum)
        e_tot[nb] = jnp.exp(tot)
    ksl = lambda pr: slice(pr * kw, (pr + 1) * kw)
    vsl = lambda pr: slice(pr * vw, (pr + 1) * vw)
    rows = lambda j: slice(j * C, (j + 1) * C)
    chains = [(nb, pr) for nb in range(NB) for pr in range(npair)]
    units = [(j, nb, pr) for j in order for (nb, pr) in chains]
    qp = {(j, nb, pr): q_t[nb][rows(j), ksl(pr)] for (j, nb, pr) in units}
    vp = {(j, nb, pr): v[nb][rows(j), vsl(pr)] for (j, nb, pr) in units}
    kv ={(j, nb, pr): jnp.where(diag, _mm_tn(k_e[nb][rows(j), ksl(pr)], vp[(j, nb, pr)]), 0.0) for (j, nb, pr) in units}
    dec = {(j, nb, pr): _row_vector_to_column(e_tot[nb][j * C:j * C + 1, ksl(pr)]) for (j, nb, pr) in units}
    bdv = {u: _block_diag(vp[u], first_v) for u in units}
    safe = span <= GLA_SPLIT_EXP_LIMIT

    @pl.when(safe)
    def _():
        for idx, (j, nb, pr) in enumerate(units):
            att_ref[idx] = jnp.where(incl2, _mm_nt(q_m[nb][rows(j), ksl(pr)],
                                                   _block_diag(k_t[nb][rows(j), ksl(pr)], first_k)), 0.0)

    @pl.when(jnp.logical_not(safe))
    def _():
        tok = lax.broadcasted_iota(jnp.int32, (C, 1), 0)
        lane = lax.broadcasted_iota(jnp.int32, (1, kw), 1)
        for idx, (j, nb, pr) in enumerate(units):
            q_i = q_s[nb][rows(j), ksl(pr)]
            c_i = cum_ref[nb, rows(j), ksl(pr)]

            def columns(tile, acc, j=j, nb=nb, pr=pr, q_i=q_i, c_i=c_i):
                r0 = pl.multiple_of(j * C + tile * BF16_ROWS, BF16_ROWS)
                k_rows = p_ref[nb, pl.ds(r0, BF16_ROWS), nk + pr * kw:nk + (pr + 1) * kw].astype(F32)
                c_rows = cum_ref[nb, pl.ds(r0, BF16_ROWS), ksl(pr)]
                for r in range(BF16_ROWS):
                    t = tile * BF16_ROWS + r
                    x = q_i * k_rows[r:r + 1] * jnp.exp(jnp.minimum(c_i - c_rows[r:r + 1], 0.0))
                    s_a = jnp.sum(jnp.where(first_k, x, 0.0), axis=1, keepdims=True)
                    s_b = jnp.sum(jnp.where(first_k, 0.0, x), axis=1, keepdims=True)
                    seen = (tok <= t) if reverse else (tok >= t)
                    acc = jnp.where(lane == t, jnp.where(seen, s_a, 0.0), acc)
                    acc = jnp.where(lane == t + GLA_DK, jnp.where(seen, s_b, 0.0), acc)
                return acc

            att_ref[idx] = lax.fori_loop(0, C // BF16_ROWS, columns, jnp.zeros((C, kw), F32))

    att = {u: att_ref[idx] for idx, u in enumerate(units)}
    lhs = {u: jnp.concatenate([att[u], qp[u]], axis=1) for u in units}
    hs = {ch: h_ref[ch[0], ch[1]] for ch in chains}
    outs = {}
    for j in order:
        for ch in chains:
            u = (j,) + ch
            outs[u] = _mm(lhs[u], jnp.concatenate([bdv[u], hs[ch]], axis=0))
        for ch in chains:
            u = (j,) + ch
            hs[ch] = dec[u] * hs[ch] + kv[u]
    for ch in chains:
        h_ref[ch[0], ch[1]] = hs[ch]
    for nb in range(NB):
        o = jnp.concatenate([jnp.concatenate([outs[(j, nb, pr)] for pr in range(npair)], axis=1) for j in range(nsub)], axis=0)
        if final:
            o = o + oprev_ref[nb].astype(F32)
            g = ggla_ref[...]
            ys = []
            for h in range(GLA_HEADS):
                oh = o[:, h * GLA_DV:(h + 1) * GLA_DV]
                ys.append(oh * lax.rsqrt(jnp.mean(oh * oh, axis=-1, keepdims=True) + NORM_EPS) * g)
            o = jnp.concatenate(ys, axis=1) * _silu(og[nb])
        o_ref[nb] = o.astype(o_ref.dtype)


def _scan_batch(B, most=2):
    return max(n for n in (1, 2, 4) if n <= most and B % n == 0)


def _gla_scan(p, wgk, bgk, nzc, reverse, oprev=None, ggla=None):
    B, L, _ = p.shape
    C = SCAN_CHUNK * GLA_SUB
    NB = _scan_batch(B, GLA_SCAN_BATCH)
    nc = L // C
    nzc = nzc // GLA_SUB
    cidx = _chunk_index(reverse, nzc, nc)
    tok = lambda wd: pl.BlockSpec((NB, C, wd), lambda b, c: (b, cidx(c), 0))
    final = oprev is not None
    ins = [p, wgk, bgk]
    specs = [tok(GLA_W), _full(wgk.shape), _full(bgk.shape)]
    if final:
        ins += [oprev, ggla]
        specs += [tok(MIX_W), _full(ggla.shape)]
    return pl.pallas_call(
        functools.partial(_gla_kernel, reverse, final),
        grid=(B // NB, nc),
        in_specs=specs,
        out_specs=tok(MIX_W),
        out_shape=jax.ShapeDtypeStruct((B, L, MIX_W), F32 if final else BF16),
        scratch_shapes=[pltpu.VMEM((NB, GLA_HEADS // 2, 2 * GLA_DK, 2 * GLA_DV), F32),
                        pltpu.VMEM((GLA_SUB * NB * (GLA_HEADS // 2), SCAN_CHUNK, 2 * GLA_DK), F32),
                        pltpu.VMEM((NB, C, GLA_HEADS * GLA_DK), F32)],
        compiler_params=_params("parallel", "arbitrary"),
        name="gla_bwd" if reverse else "gla_fwd",
    )(*ins)


def _swa_kernel(nzb, nblk, q_ref, kv_ref, cs_ref, sink_ref, o_ref):
    BL = SWA_BLOCK
    nper = q_ref.shape[0] // BL
    kvw = SWA_KV_HEADS * SWA_HEAD_DIM
    qw = SWA_Q_HEADS * SWA_HEAD_DIM
    nloc = 3 * BL
    lz = nzb * BL
    nkeys = nloc + lz

    def rope(x, xs, cs, reps):
        cos = jnp.concatenate([cs[:, :LANE]] * reps, axis=1)
        sin = jnp.concatenate([cs[:, LANE:]] * reps, axis=1)
        return x * cos + xs * sin

    kvz = kv_ref[0:lz, :]
    q, keys, vals, bias = {}, {}, {}, {}
    for sub in range(nper):
        n = pl.program_id(1) * nper + sub
        qq = q_ref[sub * BL:(sub + 1) * BL, :]
        q0 = pl.multiple_of(n * BL, BL)
        q[sub] = rope(qq[:, :qw], qq[:, qw:], cs_ref[pl.ds(q0, BL), :], qw // LANE) * (SWA_HEAD_DIM ** -0.5 * LOG2E)
        k0 = pl.multiple_of(jnp.clip((n - 1) * BL, 0, (nblk - 3) * BL), BL)
        kv = kv_ref[pl.ds(k0, nloc), :]
        keys[sub] = jnp.concatenate([rope(kv[:, :kvw], kv[:, kvw:2 * kvw], cs_ref[pl.ds(k0, nloc), :], kvw // LANE),
                                     kvz[:, :kvw]], axis=0)
        vals[sub] = jnp.concatenate([kv[:, 2 * kvw:], kvz[:, 2 * kvw:]], axis=0)
        qpos = n * BL + lax.broadcasted_iota(jnp.int32, (BL, nkeys), 0)
        col = lax.broadcasted_iota(jnp.int32, (BL, nkeys), 1)
        kpos = k0 + col
        lo = jnp.where(n >= nzb, lz, nblk * BL)
        band = (jnp.abs(kpos - qpos) <= SWA_WINDOW) & (kpos >= lo)
        bias[sub] = jnp.where(band | (col >= nloc), 0.0, NEG_INF)
    chains = [(sub, h) for sub in range(nper) for h in range(SWA_Q_HEADS)]
    kvh = lambda h: slice((h // SWA_GROUP) * SWA_HEAD_DIM, (h // SWA_GROUP + 1) * SWA_HEAD_DIM)
    s = {(sub, h): _mm_nt(q[sub][:, h * SWA_HEAD_DIM:(h + 1) * SWA_HEAD_DIM], keys[sub][:, kvh(h)]) + bias[sub]
         for (sub, h) in chains}
    sk = [sink_ref[h] * LOG2E for h in range(SWA_Q_HEADS)]
    m = {c: jnp.maximum(jnp.max(s[c], axis=-1, keepdims=True), sk[c[1]]) for c in chains}
    e = {c: jnp.exp2(s[c] - m[c]) for c in chains}
    den = {c: jnp.sum(e[c], axis=-1, keepdims=True) + jnp.exp2(sk[c[1]] - m[c]) for c in chains}
    outs = {c: _mm(e[c], vals[c[0]][:, kvh(c[1])]) / den[c] for c in chains}
    for sub in range(nper):
        o_ref[sub * BL:(sub + 1) * BL, :] = jnp.concatenate([outs[(sub, h)] for h in range(SWA_Q_HEADS)], axis=1)


def _swa(pq, pkv, cs, sink, nzb):
    B, L, _ = pq.shape
    BL = SWA_BLOCK
    nblk = L // BL
    nper = 2 if nblk % 2 == 0 else 1
    assert nblk >= 3
    return pl.pallas_call(
        functools.partial(_swa_kernel, nzb, nblk),
        grid=(B, nblk // nper),
        in_specs=[pl.BlockSpec((None, nper * BL, SWA_Q_W), lambda b, n: (b, n, 0)),
                  pl.BlockSpec((None, L, SWA_KV_W), lambda b, n: (b, 0, 0)),
                  _full(cs.shape),
                  pl.BlockSpec(memory_space=pltpu.SMEM)],
        out_specs=pl.BlockSpec((None, nper * BL, MIX_W), lambda b, n: (b, n, 0)),
        out_shape=jax.ShapeDtypeStruct((B, L, MIX_W), F32),
        compiler_params=_params("parallel", "parallel"),
        name="swa",
    )(pq, pkv, cs, sink)


def _rwkv_feat_kernel(nzt, p_ref, pp_ref, pn_ref, vec_ref, ww_ref, wa_ref, wg_ref, e_ref,
                      o_r, o_v, o_kk, o_g, o_gb, o_lwf, o_kf, o_bf, o_lwb, o_kb, o_bb):
    i = pl.program_id(1)
    n = pl.num_programs(1)
    TM = p_ref.shape[0]
    first = (i == 0) | (i == nzt)
    last = (i == nzt - 1) | (i == n - 1)
    p_bf = p_ref[...]
    p_ext = jnp.concatenate([pp_ref[...], p_bf, pn_ref[...]], axis=0)
    row = lax.broadcasted_iota(jnp.int32, (TM, TM + 2 * HALO), 0)
    col = lax.broadcasted_iota(jnp.int32, (TM, TM + 2 * HALO), 1)
    lo = jnp.where(first, HALO, 0)
    hi = jnp.where(last, TM + HALO - 1, TM + 2 * HALO - 1)
    nbr = ((col == row + (HALO - 1)) & (col >= lo)) | ((col == row + (HALO + 1)) & (col <= hi))
    around = jnp.dot(jnp.where(nbr, 1.0, 0.0).astype(BF16), p_ext, preferred_element_type=F32)
    p = p_bf.astype(F32)
    vec = vec_ref[...]
    mu = jnp.concatenate([vec[8 + j:9 + j] for j in range(4)], axis=1)[:, :RWKV_W]
    pm = p * (1.0 - mu) + (0.5 * mu) * around
    W = MIX_W
    r, k, v = pm[:, :W], pm[:, W:2 * W], pm[:, 2 * W:3 * W]
    wa = pm[:, 3 * W:3 * W + LANE]
    gl = pm[:, 3 * W + LANE:]
    e = e_ref[...]
    k_k, k_a, r_k = vec[4:5], vec[5:6], vec[6:7]
    g = _mm(_sigmoid(gl), wg_ref[...])
    kkn = k * k_k
    kk = kkn * lax.rsqrt(jnp.maximum(_segsum(kkn * kkn, e), 1e-24))
    twa = jnp.tanh(wa)
    ksum = None
    for d, (o_lw, o_k, o_b) in enumerate(((o_lwf, o_kf, o_bf), (o_lwb, o_kb, o_bb))):
        a = _sigmoid(vec[2 + d:3 + d] + _mm(wa, wa_ref[d]))
        kd = k * (1.0 + (a - 1.0) * k_a)
        o_lw[...] = -DECAY_SCALE * _sigmoid(vec[d:d + 1] + _mm(twa, ww_ref[d]))
        o_k[...] = kd.astype(o_k.dtype)
        o_b[...] = (kk * a).astype(o_b.dtype)
        ksum = kd if ksum is None else ksum + kd
    o_r[...] = r.astype(o_r.dtype)
    o_v[...] = v.astype(o_v.dtype)
    o_kk[...] = kk.astype(o_kk.dtype)
    o_g[...] = g.astype(o_g.dtype)
    o_gb[...] = (_segsum(r * ksum * r_k, e) * v * g).astype(o_gb.dtype)


def _rwkv_feat(p, vec, ww, wa, wg, e, nzt):
    B, L, _ = p.shape
    TM = ROW_TILE
    rh = TM // HALO
    tok = pl.BlockSpec((None, TM, RWKV_W), lambda b, i: (b, i, 0))
    halo_p = pl.BlockSpec((None, HALO, RWKV_W), lambda b, i: (b, jnp.maximum(i * rh - 1, 0), 0))
    halo_n = pl.BlockSpec((None, HALO, RWKV_W), lambda b, i: (b, jnp.minimum((i + 1) * rh, L // HALO - 1), 0))
    out = pl.BlockSpec((None, TM, MIX_W), lambda b, i: (b, i, 0))
    dts = [BF16] * 5 + [F32, BF16, BF16] * 2
    return pl.pallas_call(
        functools.partial(_rwkv_feat_kernel, nzt),
        grid=(B, L // TM),
        in_specs=[tok, halo_p, halo_n, _full(vec.shape), _full(ww.shape), _full(wa.shape), _full(wg.shape), _full(e.shape)],
        out_specs=[out] * 11,
        out_shape=[jax.ShapeDtypeStruct((B, L, MIX_W), dt) for dt in dts],
        compiler_params=_params("parallel", "parallel"),
        name="rwkv_feat",
    )(p, p, p, vec, ww, wa, wg, e)


def _rwkv_scan_kernel(reverse, final, *refs):
    if final:
        r_ref, v_ref, kk_ref, lw_ref, k_ref, b_ref, yprev_ref, g_ref, gb_ref, gn_ref, o_ref, h_ref = refs
    else:
        r_ref, v_ref, kk_ref, lw_ref, k_ref, b_ref, o_ref, h_ref = refs

    @pl.when(pl.program_id(1) == 0)
    def _():
        h_ref[...] = jnp.zeros_like(h_ref)

    NB, R = lw_ref.shape[0], lw_ref.shape[1]
    C = SCAN_CHUNK
    nsub = R // C
    order = tuple(reversed(range(nsub))) if reverse else tuple(range(nsub))
    PW = LANE
    npair = RWKV_HEADS // 2
    strict2, incl2, ident2 = _pair_order_masks(C, reverse)
    first = _first_head(PW, RWKV_HEAD)
    diag = _diag_blocks(PW, RWKV_HEAD, PW)
    inclb = _block_order_mask(R, C, reverse).astype(BF16)
    kap_t, r_t, k_t, b_t, k_e, b_e, e_tot, v_all = {}, {}, {}, {}, {}, {}, {}, {}
    for nb in range(NB):
        lw = lw_ref[nb]
        cum = _mm_01(inclb, lw)
        tot = _chunk_totals(lw, C)
        e_inv = jnp.exp(-cum)
        e_end = jnp.exp(tot - cum)
        e_tot[nb] = jnp.exp(tot)
        kap_t[nb] = kk_ref[nb].astype(F32) * jnp.exp(cum - lw)
        r_t[nb] = r_ref[nb].astype(F32) * jnp.exp(cum)
        k_all, b_all = k_ref[nb].astype(F32), b_ref[nb].astype(F32)
        k_t[nb], b_t[nb] = k_all * e_inv, b_all * e_inv
        k_e[nb], b_e[nb] = k_all * e_end, b_all * e_end
        v_all[nb] = v_ref[nb].astype(F32)
    sl = lambda pr: slice(pr * PW, (pr + 1) * PW)
    rows = lambda j: slice(j * C, (j + 1) * C)
    pick = lambda d, u: d[u[1]][rows(u[0]), sl(u[2])]
    chains = [(nb, pr) for nb in range(NB) for pr in range(npair)]
    units = [(j,) + ch for j in order for ch in chains]
    vp = {u: pick(v_all, u) for u in units}
    G = {u: _mm_nt(jnp.concatenate([pick(kap_t, u), pick(r_t, u)], axis=0),
                   jnp.concatenate([_block_diag(pick(b_t, u), first), _block_diag(pick(k_t, u), first)], axis=0))
         for u in units}
    a_ab = {u: jnp.where(strict2, G[u][:C, :PW], 0.0) for u in units}
    a_ak = {u: jnp.where(strict2, G[u][:C, PW:], 0.0) for u in units}
    a_rb = {u: jnp.where(incl2, G[u][C:, :PW], 0.0) for u in units}
    a_rk = {u: jnp.where(incl2, G[u][C:, PW:], 0.0) for u in units}
    eye2 = jnp.where(ident2, 1.0, 0.0)
    t_inv = {u: eye2 - a_ab[u] for u in units}
    pw = {u: _mm(a_ab[u], _block_diag(a_ab[u], first)) for u in units}
    span = 2
    while span < C:
        span *= 2
        if span < C:
            prod = {u: _mm(jnp.concatenate([t_inv[u], pw[u]], axis=0), _block_diag(pw[u], first)) for u in units}
            t_inv = {u: t_inv[u] + prod[u][:C] for u in units}
            pw = {u: prod[u][C:] for u in units}
        else:
            t_inv = {u: t_inv[u] + _mm(t_inv[u], _block_diag(pw[u], first)) for u in units}
    bdv = {u: _block_diag(vp[u], first) for u in units}
    ulhs = {u: jnp.concatenate([a_ak[u], pick(kap_t, u)], axis=1) for u in units}
    ylhs = {u: jnp.concatenate([a_rk[u], -a_rb[u], pick(r_t, u)], axis=1) for u in units}
    hlhs = {u: jnp.concatenate([pick(k_e, u), -pick(b_e, u)], axis=0) for u in units}
    dec = {u: _row_vector_to_column(e_tot[u[1]][u[0] * C:u[0] * C + 1, sl(u[2])]) for u in units}
    hs = {ch: h_ref[ch[0], ch[1]] for ch in chains}
    ys = {}
    for j in order:
        units_j = [(j,) + ch for ch in chains]
        rhs = {u: _mm(ulhs[u], jnp.concatenate([bdv[u], hs[u[1:]]], axis=0)) for u in units_j}
        u_p = {u: _mm(t_inv[u], _block_diag(rhs[u], first)) for u in units_j}
        for u in units_j:
            ys[u] = _mm(ylhs[u], jnp.concatenate([bdv[u], _block_diag(u_p[u], first), hs[u[1:]]], axis=0))
        for u in units_j:
            upd = _mm_tn(hlhs[u], jnp.concatenate([vp[u], u_p[u]], axis=0))
            hs[u[1:]] = dec[u] * hs[u[1:]] + jnp.where(diag, upd, 0.0)
    for ch in chains:
        h_ref[ch[0], ch[1]] = hs[ch]
    for nb in range(NB):
        if final:
            yprev = yprev_ref[nb]
            outs = []
            for pr in range(npair):
                y = jnp.concatenate([ys[(j, nb, pr)] for j in range(nsub)], axis=0) + yprev[:, sl(pr)]
                s_a = jnp.sum(jnp.where(first, y, 0.0), axis=-1, keepdims=True)
                s_b = jnp.sum(jnp.where(first, 0.0, y), axis=-1, keepdims=True)
                yc = y - jnp.where(first, s_a, s_b) * (1.0 / RWKV_HEAD)
                q = yc * yc
                v_a = jnp.sum(jnp.where(first, q, 0.0), axis=-1, keepdims=True)
                v_b = jnp.sum(jnp.where(first, 0.0, q), axis=-1, keepdims=True)
                outs.append(yc * lax.rsqrt(jnp.where(first, v_a, v_b) * (1.0 / RWKV_HEAD) + RWKV_GN_EPS))
            gn = gn_ref[...]
            o_ref[nb] = (jnp.concatenate(outs, axis=1) * gn[0:1] + gn[1:2]) * g_ref[nb].astype(F32) + gb_ref[nb].astype(F32)
        else:
            o_ref[nb] = jnp.concatenate(
                [jnp.concatenate([ys[(j, nb, pr)] for pr in range(npair)], axis=1) for j in range(nsub)], axis=0)


def _rwkv_scan(r, v, kk, lw, k, b, nzc, reverse, yprev=None, g=None, gb=None, gn=None):
    B, L, _ = r.shape
    C = SCAN_CHUNK * SCAN_SUB
    NB = _scan_batch(B, RWKV_SCAN_BATCH)
    nc = L // C
    nzc = nzc // SCAN_SUB
    cidx = _chunk_index(reverse, nzc, nc)
    tok = pl.BlockSpec((NB, C, MIX_W), lambda b_, c: (b_, cidx(c), 0))
    final = yprev is not None
    ins = [r, v, kk, lw, k, b]
    specs = [tok] * 6
    if final:
        ins += [yprev, g, gb, gn]
        specs += [tok] * 3 + [_full(gn.shape)]
    return pl.pallas_call(
        functools.partial(_rwkv_scan_kernel, reverse, final),
        grid=(B // NB, nc),
        in_specs=specs,
        out_specs=tok,
        out_shape=jax.ShapeDtypeStruct((B, L, MIX_W), F32),
        scratch_shapes=[pltpu.VMEM((NB, RWKV_HEADS // 2, LANE, LANE), F32)],
        compiler_params=_params("parallel", "arbitrary"),
        name="rwkv_bwd" if reverse else "rwkv_fwd",
    )(*ins)


def _merge_kernel(stream, off, *refs):
    na, ns = len(stream.arrays), stream.ns
    mod_refs = refs[na:na + ns]
    g_ref, ya_ref, yb_ref, yc_ref, wg_ref, wb_ref, wo_ref, o_ref = refs[na + ns:]
    x = stream.read(refs[:na], off)
    m = [mod_refs[s][...] for s in range(ns)]
    h = [_norm_mod(x[s], g_ref[...], m[s][0:1], m[s][1:2]).astype(BF16) for s in range(ns)]
    gates = [jnp.dot(h[s], wg_ref[...], preferred_element_type=F32) for s in range(ns)]
    acc = [None] * ns
    for i, y_ref in enumerate((ya_ref, yb_ref, yc_ref)):
        for s in range(ns):
            t = _mm(y_ref[s], wb_ref[i]) / (1.0 + jnp.exp(-gates[s][:, i * D_MODEL:(i + 1) * D_MODEL]))
            acc[s] = t if acc[s] is None else acc[s] + t
    for s in range(ns):
        o_ref[s] = x[s] + m[s][2:3] * _mm(acc[s], wo_ref[...])


def _merge(stream, l, mod, g, ya, yb, yc, wg, wb, wo, off):
    B, L, ns = stream.batch, stream.rows, stream.ns
    nt = L // ROW_TILE - off
    return pl.pallas_call(
        functools.partial(_merge_kernel, stream, off),
        grid=(B // ns, nt),
        in_specs=stream.specs(off) + _mod_specs(l, stream.nzt, B, ns, off) + [
            _layer_spec(g, l), _tok_spec(ns, MIX_W, off), _tok_spec(ns, MIX_W, off), _tok_spec(ns, MIX_W, off),
            _layer_spec(wg, l), _layer_spec(wb, l), _layer_spec(wo, l)],
        out_specs=_tok_spec(ns, D_MODEL),
        out_shape=jax.ShapeDtypeStruct((B, nt * ROW_TILE, D_MODEL), F32),
        compiler_params=_params("parallel", "parallel"),
        name="merge",
    )(*stream.arrays, *([mod] * ns), g, ya, yb, yc, wg, wb, wo)


def _ffn_kernel(final, ns, s_ref, *refs):
    mod_refs = refs[:ns]
    g_ref, w1_ref, w3_ref, w2_ref, gf_ref, o_ref = refs[ns:]
    m = [mod_refs[s][...] for s in range(ns)]
    x = [s_ref[s] for s in range(ns)]
    h = [_norm_mod(x[s], g_ref[...], m[s][3:4], m[s][4:5]).astype(BF16) for s in range(ns)]
    a = [jnp.dot(h[s], w1_ref[...], preferred_element_type=F32) for s in range(ns)]
    b = [jnp.dot(h[s], w3_ref[...], preferred_element_type=F32) for s in range(ns)]
    for s in range(ns):
        o = x[s] + m[s][5:6] * _mm(_silu(a[s]) * b[s], w2_ref[...])
        if final:
            o = o * lax.rsqrt(jnp.mean(o * o, axis=-1, keepdims=True) + NORM_EPS) * gf_ref[...]
        o_ref[s] = o


def _ffn(s, l, mod, g, w1, w3, w2, gf, nzt, off, final):
    B, L, _ = s.shape
    ns = _tile_batch(B)
    once = pl.Buffered(1)
    return pl.pallas_call(
        functools.partial(_ffn_kernel, final, ns),
        grid=(B // ns, L // ROW_TILE),
        in_specs=[_tok_spec(ns, D_MODEL)] + _mod_specs(l, nzt, B, ns, off) + [
            _layer_spec(g, l), _layer_spec(w1, l, pipeline_mode=once), _layer_spec(w3, l, pipeline_mode=once),
            _layer_spec(w2, l, pipeline_mode=once), _full((1, D_MODEL))],
        out_specs=_tok_spec(ns, D_MODEL),
        out_shape=jax.ShapeDtypeStruct((B, L, D_MODEL), F32),
        compiler_params=_params("parallel", "parallel"),
        name="ffn",
    )(s, *([mod] * ns), g, w1, w3, w2, gf)


def _rope_tables(T, lz):
    rows = T // GRID_W
    row = jnp.repeat(jnp.arange(rows), GRID_W).astype(F32)
    col = jnp.tile(jnp.arange(GRID_W), rows).astype(F32)
    inv = ROPE_BASE ** (-jnp.arange(ROPE_FREQS, dtype=F32) / ROPE_FREQS)
    ar, ac = row[:, None] * inv, col[:, None] * inv
    cos = jnp.concatenate([jnp.cos(ar), jnp.cos(ar), jnp.cos(ac), jnp.cos(ac)], axis=1)
    sin = jnp.concatenate([-jnp.sin(ar), jnp.sin(ar), -jnp.sin(ac), jnp.sin(ac)], axis=1)
    reps = LANE // SWA_HEAD_DIM
    x_tab = jnp.concatenate([jnp.tile(cos, (1, reps)), jnp.tile(sin, (1, reps))], axis=1)
    z_tab = jnp.concatenate([jnp.ones((lz, LANE), F32), jnp.zeros((lz, LANE), F32)], axis=1)
    return jnp.concatenate([z_tab, x_tab], axis=0)


def _partner_cols(w):
    return w.reshape(w.shape[:-1] + (-1, 2, ROPE_FREQS))[..., ::-1, :].reshape(w.shape)


def _mixer_weight(w_in):
    nk = GLA_HEADS * GLA_DK
    a = 2 * nk + MIX_W
    r2 = 2 * GLA_GATE_RANK
    head, gk, tail = w_in[..., :a].astype(BF16), w_in[..., a:a + r2].astype(BF16), w_in[..., a + r2:].astype(BF16)
    qw = SWA_Q_HEADS * SWA_HEAD_DIM
    kvw = SWA_KV_HEADS * SWA_HEAD_DIM
    o = np.cumsum([0, MIX_W, qw, kvw, kvw, RWKV_W])
    og, q, k, v, rwkv = (tail[..., o[i]:o[i + 1]] for i in range(5))
    pad = jnp.zeros(w_in.shape[:-1] + (LANE - r2,), BF16)
    w = jnp.concatenate([head, og, gk, pad, q, _partner_cols(q), k, _partner_cols(k), v, rwkv], axis=-1)
    return w, tail[..., o[5]:]


def _pad_rows(w, top, total):
    return jnp.concatenate([jnp.zeros((top, w.shape[1]), F32), w, jnp.zeros((total - top - w.shape[0], w.shape[1]), F32)], axis=0)


def kernel(x, c, ctx, c_ctx, w_ada, b_ada, g_mix, g_ffn, w_in, w_gk2, b_gk, g_gla, sink, mu_shift, w0, w_w2, a0, w_a2,
           w_g2, k_k, k_a, r_k, gn_w, gn_b, w_branch, w_out, w_ffn1, w_ffn3, w_ffn2, g_final):
    B, T, D = x.shape
    lz = ctx.shape[1]
    depth = w_in.shape[0]
    step = max(ROW_TILE, SCAN_CHUNK * GLA_SUB, SCAN_CHUNK * SCAN_SUB, SWA_BLOCK)
    assert D == D_MODEL and lz % step == 0 and T % step == 0 and T % GRID_W == 0
    nzt, nzc, nzb = lz // ROW_TILE, lz // SCAN_CHUNK, lz // SWA_BLOCK
    cs = _rope_tables(T, lz)
    c8 = jnp.concatenate([c, c_ctx[None], jnp.zeros((8 - (B + 1) % 8 if (B + 1) % 8 else 0, D), F32)], axis=0)
    hidx = np.arange(MIX_W) // RWKV_HEAD
    seg = jnp.asarray(hidx[:, None] == hidx[None, :], BF16)
    mod = _ada(c8, w_ada, b_ada).reshape(depth, c8.shape[0], 6, D)
    w_mix, w_gate = _mixer_weight(w_in)
    w_branch, w_out = w_branch.astype(BF16), w_out.astype(BF16)
    w_ffn1, w_ffn3, w_ffn2 = w_ffn1.astype(BF16), w_ffn3.astype(BF16), w_ffn2.astype(BF16)
    g_mix, g_ffn = g_mix[:, None], g_ffn[:, None]
    stream = _Stream((ctx, x), nzt)
    out = None
    for l in range(depth):
        last = l == depth - 1
        p_gla, p_q, p_kv, p_rwkv = _inproj(stream, l, mod, g_mix, w_mix)

        wgk = [_pad_rows(w_gk2[l, d], d * GLA_GATE_RANK, LANE).astype(BF16) for d in range(2)]
        o_f = _gla_scan(p_gla, wgk[0], b_gk[l, 0][None], nzc, False)
        ya = _gla_scan(p_gla, wgk[1], b_gk[l, 1][None], nzc, True, o_f, g_gla[l][None])

        yb = _swa(p_q, p_kv, cs, sink[l], nzb)

        zero = jnp.zeros((MIX_W,), F32)
        mu4 = jnp.concatenate([mu_shift[l], jnp.zeros((4 * MIX_W - RWKV_W,), F32)]).reshape(4, MIX_W)
        vec = jnp.stack([w0[l, 0], w0[l, 1], a0[l, 0], a0[l, 1], k_k[l], k_a[l], r_k[l].reshape(-1), zero], axis=0)
        vec = jnp.concatenate([vec, mu4, jnp.zeros((4, MIX_W), F32)], axis=0)
        ww = jnp.stack([_pad_rows(w_w2[l, d], 0, LANE) for d in range(2)]).astype(BF16)
        wa = jnp.stack([_pad_rows(w_a2[l, d], RWKV_DECAY_RANK, LANE) for d in range(2)]).astype(BF16)
        f_r, f_v, f_kk, f_g, f_gb, lw_f, k_f, b_f, lw_b, k_b, b_b = _rwkv_feat(
            p_rwkv, vec, ww, wa, w_g2[l].astype(BF16), seg, nzt)
        y_f = _rwkv_scan(f_r, f_v, f_kk, lw_f, k_f, b_f, nzc, False)
        yc = _rwkv_scan(f_r, f_v, f_kk, lw_b, k_b, b_b, nzc, True, y_f, f_g, f_gb, jnp.stack([gn_w[l], gn_b[l]]))

        off = nzt if last else 0
        s_mix = _merge(stream, l, mod, g_mix, ya, yb, yc, w_gate, w_branch, w_out, off)
        out = _ffn(s_mix, l, mod, g_ffn, w_ffn1, w_ffn3, w_ffn2, g_final[None], nzt, off, last)
        stream = _Stream((out,), nzt)
    return out
```
